```python
import jax
import jax.numpy as jnp
from jax import lax

D_MODEL = 1024
BATCH = 8
SEQ = 4096
DEPTH = 4

GRID_W = 64
CTX_LEN = 256
EPS = 1e-6

GLA_HEADS = 4
GLA_DK = 64
GLA_DV = 128
GLA_QK = GLA_HEADS * GLA_DK
GLA_V = GLA_HEADS * GLA_DV
GLA_RANK = 16
GLA_TAU = 16.0
GLA_CHUNK = 64

FNET_GROUPS = 4
FNET_GD = 64
FNET_W = FNET_GROUPS * FNET_GD

RG_HEADS = 4
RG_HD = 64
RG_W = RG_HEADS * RG_HD
RG_C = 8.0
CONV_W = 4
CONV_LEFT = 1

MIX_W = GLA_V + FNET_W + RG_W
D_FF = -(-8 * D_MODEL // (3 * 256)) * 256

OFF_K = GLA_QK
OFF_V = OFF_K + GLA_QK
OFF_DEC = OFF_V + GLA_V
OFF_OG = OFF_DEC + 2 * GLA_RANK
OFF_F = OFF_OG + GLA_V
OFF_RX = OFF_F + FNET_W
OFF_RG = OFF_RX + RG_W
N_IN = OFF_RG + RG_W
IN_SPLITS = (OFF_K, OFF_V, OFF_DEC, OFF_OG, OFF_F, OFF_RX, OFF_RG)

kernel_name = 'hybrid_gla_fnet_rglru_prefix_dit'


def rmsnorm(x, g):
    xf = x.astype(jnp.float32)
    y = xf * lax.rsqrt(jnp.mean(xf * xf, axis=-1, keepdims=True) + EPS)
    return (y * g.astype(jnp.float32)).astype(x.dtype)


def modulate(h, shift, scale):
    return h * (1 + scale) + shift


def rev(t):
    return jnp.flip(t, axis=1)


def swiglu(h, w_gate, w_up, w_down):
    return (jax.nn.silu(h @ w_gate) * (h @ w_up)) @ w_down


def gla_prepare(z_q, z_k, z_v, z_dec, w_dec, b_dec):
    B, T, _ = z_q.shape
    f32 = jnp.float32
    q = z_q.astype(f32).reshape(B, T, GLA_HEADS, GLA_DK) * (GLA_DK ** -0.5)
    k = z_k.astype(f32).reshape(B, T, GLA_HEADS, GLA_DK)
    v = z_v.astype(f32).reshape(B, T, GLA_HEADS, GLA_DV)
    lr = z_dec.astype(f32).reshape(B, T, 2, GLA_RANK)
    logit = jnp.einsum('btdr,drk->btdk', lr, w_dec.astype(f32)) + b_dec.astype(f32)
    log_a = (jax.nn.log_sigmoid(logit) / GLA_TAU).reshape(B, T, 2, GLA_HEADS, GLA_DK)
    return q, k, v, log_a[:, :, 0], log_a[:, :, 1]


def gla_scan(q, k, v, log_a, s0):
    B, T, H, _ = q.shape
    N = T // GLA_CHUNK

    def chunks(t):
        return t.reshape(B, N, GLA_CHUNK, H, t.shape[-1]).transpose(1, 0, 3, 2, 4)

    qc, kc, vc, gc = chunks(q), chunks(k), chunks(v), chunks(log_a)
    b = jnp.cumsum(gc, axis=3)
    b_last = b[:, :, :, -1:, :]
    q_t = qc * jnp.exp(b)
    k_t = kc * jnp.exp(-b)
    k_end = kc * jnp.exp(b_last - b)
    mask = jnp.tril(jnp.ones((GLA_CHUNK, GLA_CHUNK), dtype=bool))
    attn = jnp.where(mask, jnp.einsum('nbhik,nbhjk->nbhij', q_t, k_t), 0.0)
    o_intra = jnp.einsum('nbhij,nbhjv->nbhiv', attn, vc)

    def step(s, inp):
        q_n, k_n, v_n, d_n = inp
        o_n = jnp.einsum('bhik,bhkv->bhiv', q_n, s)
        s_new = s * d_n[:, :, 0, :, None] + jnp.einsum('bhjk,bhjv->bhkv', k_n, v_n)
        return s_new, o_n

    s_fin, o_inter = lax.scan(step, s0, (q_t, k_end, vc, jnp.exp(b_last)))
    o = (o_intra + o_inter).transpose(1, 0, 3, 2, 4).reshape(B, T, H, v.shape[-1])
    return o, s_fin


def gla_final_state(k, v, log_a):
    b = jnp.cumsum(log_a, axis=1)
    w = jnp.exp(b[:, -1:] - b)
    return jnp.einsum('bthk,bthv->bhkv', k * w, v)


def gla_norm_gate(o, z_og, g_gla):
    B, T = o.shape[:2]
    o = o * lax.rsqrt(jnp.mean(o * o, axis=-1, keepdims=True) + EPS) * g_gla.astype(jnp.float32)
    return o.reshape(B, T, GLA_V) * jax.nn.silu(z_og.astype(jnp.float32))


def fourier_mix(z_f):
    B, T, _ = z_f.shape
    fg = z_f.astype(jnp.float32).reshape(B, T, FNET_GROUPS, FNET_GD)
    y = jnp.fft.fft2(fg, axes=(1, 3), norm='ortho').real
    return y.reshape(B, T, FNET_W)


def short_conv(u, w, b):
    L = u.shape[2]
    up = jnp.pad(u, ((0, 0), (0, 0), (CONV_LEFT, CONV_W - 1 - CONV_LEFT), (0, 0)))
    w = w.astype(jnp.float32)
    out = b.astype(jnp.float32)
    for j in range(CONV_W):
        out = out + up[:, :, j:j + L] * w[j]
    return out


def rglru_coeffs(u, w_a, b_a, w_x, b_x, lam):
    B, T, _ = u.shape
    f32 = jnp.float32
    uh = u.reshape(B, T, RG_HEADS, RG_HD)
    r = jax.nn.sigmoid(jnp.einsum('bthi,hij->bthj', uh, w_a.astype(f32)).reshape(B, T, RG_W) + b_a.astype(f32))
    i = jax.nn.sigmoid(jnp.einsum('bthi,hij->bthj', uh, w_x.astype(f32)).reshape(B, T, RG_W) + b_x.astype(f32))
    log_a = -RG_C * r * jax.nn.softplus(-lam.astype(f32))
    return jnp.exp(log_a), jnp.sqrt(-jnp.expm1(2.0 * log_a)) * (i * u)


def _lin_combine(left, right):
    a_l, b_l = left
    a_r, b_r = right
    return a_l * a_r, a_r * b_l + b_r


def linear_scan(a, bx, h0):
    bx = bx.at[:, 0].add(a[:, 0] * h0)
    return lax.associative_scan(_lin_combine, (a, bx), axis=1)[1]


def rglru_bidir(u, w_rg_a, b_rg_a, w_rg_x, b_rg_x, rg_lam, h0_f, h0_b):
    a_f, bx_f = rglru_coeffs(u, w_rg_a[0], b_rg_a[0], w_rg_x[0], b_rg_x[0], rg_lam[0])
    a_b, bx_b = rglru_coeffs(u, w_rg_a[1], b_rg_a[1], w_rg_x[1], b_rg_x[1], rg_lam[1])
    h_f = linear_scan(a_f, bx_f, h0_f)
    h_b_rev = linear_scan(rev(a_b), rev(bx_b), h0_b)
    return h_f, h_b_rev


def combine_groups(o_gla, z_og, g_gla, z_f, h_rg, z_rg):
    y_gla = gla_norm_gate(o_gla, z_og, g_gla)
    y_f = fourier_mix(z_f)
    y_rg = h_rg * jax.nn.gelu(z_rg.astype(jnp.float32))
    return jnp.concatenate([y_gla, y_f, y_rg], axis=-1).astype(z_og.dtype)


def context_mixer(zc, w_dec, b_dec, g_gla, w_conv, b_conv, w_rg_a, b_rg_a, w_rg_x, b_rg_x, rg_lam, with_output):
    z_q, z_k, z_v, z_dec, z_og, z_f, z_rx, z_rg = jnp.split(zc, IN_SPLITS, axis=-1)
    B = zc.shape[0]
    q, k, v, la_f, la_b = gla_prepare(z_q, z_k, z_v, z_dec, w_dec, b_dec)
    u = short_conv(z_rx.astype(jnp.float32)[:, None], w_conv, b_conv)[:, 0]
    zeros_h = jnp.zeros((B, RG_W), jnp.float32)
    h_f, h_b_rev = rglru_bidir(u, w_rg_a, b_rg_a, w_rg_x, b_rg_x, rg_lam, zeros_h, zeros_h)
    if not with_output:
        states = (gla_final_state(k, v, la_f), gla_final_state(rev(k), rev(v), rev(la_b)),
                  h_f[:, -1], h_b_rev[:, -1])
        return states, None
    s0 = jnp.zeros((B, GLA_HEADS, GLA_DK, GLA_DV), jnp.float32)
    o_f, s_f = gla_scan(q, k, v, la_f, s0)
    o_b_rev, s_b = gla_scan(rev(q), rev(k), rev(v), rev(la_b), s0)
    y = combine_groups(o_f + rev(o_b_rev), z_og, g_gla, z_f, h_f + rev(h_b_rev), z_rg)
    return (s_f, s_b, h_f[:, -1], h_b_rev[:, -1]), y


def latent_mixer(zl, states, w_dec, b_dec, g_gla, w_conv, b_conv, w_rg_a, b_rg_a, w_rg_x, b_rg_x, rg_lam):
    z_q, z_k, z_v, z_dec, z_og, z_f, z_rx, z_rg = jnp.split(zl, IN_SPLITS, axis=-1)
    B, T, _ = zl.shape
    rows = T // GRID_W
    s_f, s_b, hf0, hb0 = states
    q, k, v, la_f, la_b = gla_prepare(z_q, z_k, z_v, z_dec, w_dec, b_dec)
    o_f, _ = gla_scan(q, k, v, la_f, s_f)
    o_b_rev, _ = gla_scan(rev(q), rev(k), rev(v), rev(la_b), s_b)
    u = short_conv(z_rx.astype(jnp.float32).reshape(B, rows, GRID_W, RG_W), w_conv, b_conv).reshape(B, T, RG_W)
    h_f, h_b_rev = rglru_bidir(u, w_rg_a, b_rg_a, w_rg_x, b_rg_x, rg_lam, hf0, hb0)
    return combine_groups(o_f + rev(o_b_rev), z_og, g_gla, z_f, h_f + rev(h_b_rev), z_rg)


def setup_inputs(seed: int = 0) -> dict:
    key = jax.random.key(seed)
    ks = jax.random.split(key, 26)
    D = D_MODEL

    def nrm(k, shape, s):
        return jax.random.normal(k, shape, jnp.float32) * s

    u = jax.random.uniform(ks[19], (DEPTH, 2, RG_W), jnp.float32, 0.9, 0.999)
    p = u ** (1.0 / RG_C)
    rg_lam = jnp.log(p) - jnp.log1p(-p)
    return {
        'x': nrm(ks[0], (BATCH, SEQ, D), 1.0),
        'c': nrm(ks[1], (BATCH, D), 1.0),
        'ctx': nrm(ks[2], (BATCH, CTX_LEN, D), 1.0),
        'c_ctx': nrm(ks[3], (D,), 1.0),
        'w_ada': nrm(ks[4], (DEPTH, D, 6 * D), D ** -0.5),
        'b_ada': nrm(ks[5], (DEPTH, 6 * D), 0.01),
        'g_pre_mix': 1.0 + nrm(ks[6], (DEPTH, D), 0.05),
        'g_post_mix': 1.0 + nrm(ks[7], (DEPTH, D), 0.05),
        'g_pre_ffn': 1.0 + nrm(ks[8], (DEPTH, D), 0.05),
        'g_post_ffn': 1.0 + nrm(ks[9], (DEPTH, D), 0.05),
        'w_in': nrm(ks[10], (DEPTH, D, N_IN), D ** -0.5),
        'w_dec': nrm(ks[11], (DEPTH, 2, GLA_RANK, GLA_QK), GLA_RANK ** -0.5),
        'b_dec': nrm(ks[12], (DEPTH, 2, GLA_QK), 0.1),
        'g_gla': 1.0 + nrm(ks[13], (DEPTH, GLA_DV), 0.05),
        'w_conv': nrm(ks[14], (DEPTH, CONV_W, RG_W), CONV_W ** -0.5),
        'b_conv': nrm(ks[15], (DEPTH, RG_W), 0.01),
        'w_rg_a': nrm(ks[16], (DEPTH, 2, RG_HEADS, RG_HD, RG_HD), RG_HD ** -0.5),
        'b_rg_a': nrm(ks[17], (DEPTH, 2, RG_W), 0.01),
        'w_rg_x': nrm(ks[18], (DEPTH, 2, RG_HEADS, RG_HD, RG_HD), RG_HD ** -0.5),
        'b_rg_x': nrm(ks[20], (DEPTH, 2, RG_W), 0.01),
        'rg_lam': rg_lam,
        'w_out': nrm(ks[21], (DEPTH, MIX_W, D), MIX_W ** -0.5),
        'w_ffn_gate': nrm(ks[22], (DEPTH, D, D_FF), D ** -0.5),
        'w_ffn_up': nrm(ks[23], (DEPTH, D, D_FF), D ** -0.5),
        'w_ffn_down': nrm(ks[24], (DEPTH, D_FF, D), D_FF ** -0.5),
    }


def reference(x, c, ctx, c_ctx, w_ada, b_ada, g_pre_mix, g_post_mix, g_pre_ffn, g_post_ffn,
              w_in, w_dec, b_dec, g_gla, w_conv, b_conv, w_rg_a, b_rg_a, w_rg_x, b_rg_x,
              rg_lam, w_out, w_ffn_gate, w_ffn_up, w_ffn_down):
    silu_c = jax.nn.silu(c)
    silu_cc = jax.nn.silu(c_ctx)
    h_ctx = ctx
    for l in range(DEPTH):
        last = l == DEPTH - 1
        mod = (silu_c @ w_ada[l] + b_ada[l])[:, None, :]
        mod_c = silu_cc @ w_ada[l] + b_ada[l]
        sh1, sc1, gt1, sh2, sc2, gt2 = jnp.split(mod, 6, axis=-1)
        csh1, csc1, cgt1, csh2, csc2, cgt2 = jnp.split(mod_c, 6, axis=-1)

        hc = modulate(rmsnorm(h_ctx, g_pre_mix[l]), csh1, csc1)
        states, yc = context_mixer(hc @ w_in[l], w_dec[l], b_dec[l], g_gla[l], w_conv[l], b_conv[l],
                                   w_rg_a[l], b_rg_a[l], w_rg_x[l], b_rg_x[l], rg_lam[l],
                                   with_output=not last)
        hl = modulate(rmsnorm(x, g_pre_mix[l]), sh1, sc1)
        yl = latent_mixer(hl @ w_in[l], states, w_dec[l], b_dec[l], g_gla[l], w_conv[l], b_conv[l],
                          w_rg_a[l], b_rg_a[l], w_rg_x[l], b_rg_x[l], rg_lam[l])
        x = x + gt1 * rmsnorm(yl @ w_out[l], g_post_mix[l])

        hf = modulate(rmsnorm(x, g_pre_ffn[l]), sh2, sc2)
        x = x + gt2 * rmsnorm(swiglu(hf, w_ffn_gate[l], w_ffn_up[l], w_ffn_down[l]), g_post_ffn[l])

        if not last:
            h_ctx = h_ctx + cgt1 * rmsnorm(yc @ w_out[l], g_post_mix[l])
            hfc = modulate(rmsnorm(h_ctx, g_pre_ffn[l]), csh2, csc2)
            h_ctx = h_ctx + cgt2 * rmsnorm(swiglu(hfc, w_ffn_gate[l], w_ffn_up[l], w_ffn_down[l]), g_post_ffn[l])
    return x
```

```python
import functools

import jax
import jax.numpy as jnp
import numpy as np
from jax import lax
from jax.experimental import pallas as pl
from jax.experimental.pallas import tpu as pltpu

F32 = jnp.float32
BF16 = jnp.bfloat16

D_MODEL = 1024
DEPTH = 4
GRID_W = 64
EPS = 1e-6

GLA_HEADS = 4
GLA_DK = 64
GLA_DV = 128
GLA_QK = GLA_HEADS * GLA_DK
GLA_V = GLA_HEADS * GLA_DV
GLA_RANK = 16
GLA_TAU = 16.0
GLA_CHUNK = 64
FNET_W = 256
FNET_GD = 64
RG_HEADS = 4
RG_HD = 64
RG_W = 256
RG_C = 8.0
CONV_W = 4
D_FF = 2816

LANES = 128
SUBLANES = 8
DEC_PAD = LANES
N_IN_PAD = 2 * GLA_QK + 2 * GLA_V + FNET_W + 2 * RG_W + DEC_PAD
MOD_ROWS = 16
VMEM_LIMIT = 56 * 1024 * 1024

_NT = (((1,), (1,)), ((), ()))
_TN = (((0,), (0,)), ((), ()))


def _rms(x, g):
    return x * lax.rsqrt(jnp.mean(x * x, axis=-1, keepdims=True) + EPS) * g


def _const_spec(shape):
    nd = len(shape)
    return pl.BlockSpec(shape, lambda *_: (0,) * nd, pipeline_mode=pl.Buffered(1))


def _ada_body(c_ref, w_ref, b_ref, o_ref):
    c = c_ref[...]
    s = (c * jax.nn.sigmoid(c)).astype(BF16)
    o_ref[0] = jnp.dot(s, w_ref[0].astype(BF16), preferred_element_type=F32) + b_ref[0]


def _ada(c_all, w_ada, b_ada):
    depth, d, n = w_ada.shape
    tn = 1536
    return pl.pallas_call(
        _ada_body,
        out_shape=jax.ShapeDtypeStruct((depth, MOD_ROWS, n), F32),
        grid=(depth, n // tn),
        in_specs=[
            pl.BlockSpec((MOD_ROWS, d), lambda l, j: (0, 0)),
            pl.BlockSpec((1, d, tn), lambda l, j: (l, 0, j)),
            pl.BlockSpec((1, 1, tn), lambda l, j: (l, 0, j)),
        ],
        out_specs=pl.BlockSpec((1, MOD_ROWS, tn), lambda l, j: (l, 0, j)),
        compiler_params=pltpu.CompilerParams(vmem_limit_bytes=VMEM_LIMIT),
        name="ada",
    )(c_all, w_ada, b_ada.reshape(depth, 1, n))


_IN_WIDTHS = (GLA_QK, GLA_QK, GLA_V, GLA_V, FNET_W, RG_W, RG_W, DEC_PAD)


def _inproj_body(x_ref, mod_ref, g_ref, w_ref, *out_refs):
    x = x_ref[0]
    h = _rms(x, g_ref[...]) * (1.0 + mod_ref[0, 1:2, :]) + mod_ref[0, 0:1, :]
    h = h.astype(BF16)
    off = 0
    for o_ref, wd in zip(out_refs, _IN_WIDTHS):
        o_ref[0] = jnp.dot(h, w_ref[:, off:off + wd], preferred_element_type=F32)
        off += wd


def _inproj(x, mod, mod_row, g, w_in_p, tm):
    bsz, t, d = x.shape
    if mod_row is None:
        mod_map = lambda b, i: (b, 0, 0)
    else:
        mod_map = lambda b, i: (mod_row, 0, 0)
    return pl.pallas_call(
        _inproj_body,
        out_shape=[jax.ShapeDtypeStruct((bsz, t, wd), F32) for wd in _IN_WIDTHS],
        grid=(bsz, t // tm),
        in_specs=[
            pl.BlockSpec((1, tm, d), lambda b, i: (b, i, 0)),
            pl.BlockSpec((1, 6, d), mod_map),
            _const_spec((1, d)),
            _const_spec((d, N_IN_PAD)),
        ],
        out_specs=[pl.BlockSpec((1, tm, wd), lambda b, i: (b, i, 0)) for wd in _IN_WIDTHS],
        compiler_params=pltpu.CompilerParams(vmem_limit_bytes=VMEM_LIMIT),
        name="inproj",
    )(x, mod, g, w_in_p)


def _gla_body(q_ref, k_ref, v_ref, og_ref, dec_ref, wd_ref, bd_ref, g_ref, s0_ref, *rest, t, with_output):
    if with_output:
        y_ref, sfin_ref, la_ref, s_ref, o_ref = rest
    else:
        sfin_ref, la_ref, s_ref = rest
    c = GLA_CHUNK
    n_chunks = t // c
    hp = 2 * GLA_DK
    vp = 2 * GLA_DV
    rows = min(t, 512)

    for r0 in range(0, t, rows):
        code = dec_ref[0, r0:r0 + rows, :].astype(BF16)
        logit = jnp.dot(code, wd_ref[0], preferred_element_type=F32) + bd_ref[0]
        la_ref[r0:r0 + rows, :] = jax.nn.log_sigmoid(logit) * (1.0 / GLA_TAU)

    zero_blk = jnp.zeros((GLA_DK, GLA_DV), F32)
    for dirn in range(2):
        s_ref[dirn, 0:GLA_DK, 0:GLA_DV] = s0_ref[0, dirn, 0]
        s_ref[dirn, 0:GLA_DK, GLA_DV:vp] = zero_blk
        s_ref[dirn, GLA_DK:hp, 0:GLA_DV] = zero_blk
        s_ref[dirn, GLA_DK:hp, GLA_DV:vp] = s0_ref[0, dirn, 1]
    if with_output:
        o_ref[...] = jnp.zeros_like(o_ref)

    row = lax.broadcasted_iota(jnp.int32, (c, hp), 0)
    lane = lax.broadcasted_iota(jnp.int32, (c, hp), 1)
    head0 = lane < GLA_DK
    ar = lax.broadcasted_iota(jnp.int32, (c, c), 0)
    ac = lax.broadcasted_iota(jnp.int32, (c, c), 1)
    sr = lax.broadcasted_iota(jnp.int32, (hp, vp), 0)
    sc = lax.broadcasted_iota(jnp.int32, (hp, vp), 1)
    diag_blk = (sr < GLA_DK) == (sc < GLA_DV)
    scale = GLA_DK ** -0.5

    def chunk_step(n, carry):
        for dirn in range(2):
            idx = n if dirn == 0 else n_chunks - 1 - n
            r0 = pl.multiple_of(idx * c, c)
            k = k_ref[0, pl.ds(r0, c), :]
            vb = v_ref[0, pl.ds(r0, c), :].astype(BF16)
            la = la_ref[pl.ds(r0, c), dirn * hp:(dirn + 1) * hp]
            b = la
            for s in (1, 2, 4, 8, 16, 32):
                if dirn == 0:
                    b = b + jnp.where(row >= s, pltpu.roll(b, s, 0), 0.0)
                else:
                    b = b + jnp.where(row < c - s, pltpu.roll(b, c - s, 0), 0.0)
            b_all = b[c - 1:c, :] if dirn == 0 else b[0:1, :]
            ke = (k * jnp.exp(b_all - b)).astype(BF16)
            s_old = s_ref[dirn]
            if with_output:
                q = q_ref[0, pl.ds(r0, c), :]
                qt = (q * scale * jnp.exp(b)).astype(BF16)
                kt = (k * jnp.exp(-b)).astype(BF16)
                zk = jnp.zeros_like(kt)
                causal = (ar >= ac) if dirn == 0 else (ar <= ac)
                a0 = lax.dot_general(qt, jnp.where(head0, kt, zk), _NT, preferred_element_type=F32)
                a1 = lax.dot_general(qt, jnp.where(head0, zk, kt), _NT, preferred_element_type=F32)
                a0 = jnp.where(causal, a0, 0.0).astype(BF16)
                a1 = jnp.where(causal, a1, 0.0).astype(BF16)
                o_intra = jnp.concatenate(
                    [jnp.dot(a0, vb[:, 0:GLA_DV], preferred_element_type=F32),
                     jnp.dot(a1, vb[:, GLA_DV:vp], preferred_element_type=F32)], axis=1)
                o_inter = jnp.dot(qt, s_old.astype(BF16), preferred_element_type=F32)
                o_ref[pl.ds(r0, c), :] += o_intra + o_inter
            upd = lax.dot_general(ke, vb, _TN, preferred_element_type=F32)
            dcol = jnp.broadcast_to(jnp.exp(b_all), (hp, hp)).T
            dfull = jnp.concatenate([dcol, dcol], axis=1)
            s_ref[dirn] = s_old * dfull + jnp.where(diag_blk, upd, 0.0)
        return carry

    lax.fori_loop(0, n_chunks, chunk_step, 0)

    for dirn in range(2):
        sfin_ref[0, dirn, 0] = s_ref[dirn, 0:GLA_DK, 0:GLA_DV]
        sfin_ref[0, dirn, 1] = s_ref[dirn, GLA_DK:hp, GLA_DV:vp]

    if with_output:
        g = g_ref[...]
        for r0 in range(0, t, rows):
            o = o_ref[r0:r0 + rows, :]
            og = og_ref[0, r0:r0 + rows, :]
            o0 = _rms(o[:, 0:GLA_DV], g)
            o1 = _rms(o[:, GLA_DV:vp], g)
            gate = og * jax.nn.sigmoid(og)
            y_ref[0, r0:r0 + rows, :] = (jnp.concatenate([o0, o1], axis=1) * gate).astype(y_ref.dtype)


def _gla(zq, zk, zv, zog, zdec, wd_p, bd_p, g_gla, s0, with_output):
    bsz, t, _ = zq.shape
    hp, vp = 2 * GLA_DK, 2 * GLA_DV
    seq = lambda wd: pl.BlockSpec((1, t, wd), lambda b, p: (b, 0, p))
    st_spec = pl.BlockSpec((1, 2, 2, GLA_DK, GLA_DV), lambda b, p: (b, 0, p, 0, 0))
    out_shape = [jax.ShapeDtypeStruct((bsz, 2, GLA_HEADS, GLA_DK, GLA_DV), F32)]
    out_specs = [st_spec]
    scratch = [pltpu.VMEM((t, 2 * hp), F32), pltpu.VMEM((2, hp, vp), F32)]
    if with_output:
        out_shape = [jax.ShapeDtypeStruct((bsz, t, GLA_V), BF16)] + out_shape
        out_specs = [seq(vp)] + out_specs
        scratch = scratch + [pltpu.VMEM((t, vp), F32)]
    outs = pl.pallas_call(
        functools.partial(_gla_body, t=t, with_output=with_output),
        out_shape=out_shape,
        grid=(bsz, 2),
        in_specs=[
            seq(hp), seq(hp), seq(vp), seq(vp),
            pl.BlockSpec((1, t, DEC_PAD), lambda b, p: (b, 0, 0)),
            pl.BlockSpec((1, DEC_PAD, 2 * hp), lambda b, p: (p, 0, 0)),
            pl.BlockSpec((1, 1, 2 * hp), lambda b, p: (p, 0, 0)),
            pl.BlockSpec((1, GLA_DV), lambda b, p: (0, 0)),
            st_spec,
        ],
        out_specs=out_specs,
        scratch_shapes=scratch,
        compiler_params=pltpu.CompilerParams(vmem_limit_bytes=VMEM_LIMIT),
        name="gla",
    )(zq, zk, zv, zog, zdec, wd_p, bd_p, g_gla, s0)
    if with_output:
        return outs[0], outs[1]
    return None, outs[0]


def _rg_body(rx_ref, rg_ref, wc_ref, bc_ref, wg_ref, bg_ref, lam_ref, h0_ref, *rest, t, line, with_output):
    if with_output:
        y_ref, hfin_ref, af_ref, bf_ref, ab_ref, bb_ref = rest
    else:
        hfin_ref, af_ref, bf_ref, ab_ref, bb_ref = rest
    w = RG_W
    rows = min(t, 512)
    grp = SUBLANES
    ridx = lax.broadcasted_iota(jnp.int32, (rows, w), 0)
    pos = ridx & (line - 1)
    sub = ridx & (grp - 1)
    wc = wc_ref[...]
    for r0 in range(0, t, rows):
        x = rx_ref[0, r0:r0 + rows, :]
        xm1 = jnp.where(pos >= 1, pltpu.roll(x, 1, 0), 0.0)
        xp1 = jnp.where(pos <= line - 2, pltpu.roll(x, rows - 1, 0), 0.0)
        xp2 = jnp.where(pos <= line - 3, pltpu.roll(x, rows - 2, 0), 0.0)
        u = bc_ref[...] + xm1 * wc[0:1, :] + x * wc[1:2, :] + xp1 * wc[2:3, :] + xp2 * wc[3:4, :]
        gates = jnp.dot(u.astype(BF16), wg_ref[...], preferred_element_type=F32) + bg_ref[...]
        for dirn, (a_ref, b_ref) in enumerate(((af_ref, bf_ref), (ab_ref, bb_ref))):
            r = jax.nn.sigmoid(gates[:, (2 * dirn) * w:(2 * dirn + 1) * w])
            i = jax.nn.sigmoid(gates[:, (2 * dirn + 1) * w:(2 * dirn + 2) * w])
            log_a = (-RG_C) * r * jax.nn.softplus(-lam_ref[dirn:dirn + 1, :])
            a = jnp.exp(log_a)
            bx = jnp.sqrt(-jnp.tanh(log_a) * (a * a + 1.0)) * (i * u)
            for s in (1, 2, 4):
                if dirn == 0:
                    ok = sub >= s
                    sh = s
                else:
                    ok = sub <= grp - 1 - s
                    sh = rows - s
                bx = jnp.where(ok, a * pltpu.roll(bx, sh, 0) + bx, bx)
                a = jnp.where(ok, a * pltpu.roll(a, sh, 0), a)
            a_ref[r0:r0 + rows, :] = a
            b_ref[r0:r0 + rows, :] = bx

    n_grp = t // grp

    def grp_step(i, carry):
        hf, hb = carry
        r0 = pl.multiple_of(i * grp, grp)
        hfull = bf_ref[pl.ds(r0, grp), :] + af_ref[pl.ds(r0, grp), :] * hf
        bf_ref[pl.ds(r0, grp), :] = hfull
        r1 = pl.multiple_of((n_grp - 1 - i) * grp, grp)
        hbull = bb_ref[pl.ds(r1, grp), :] + ab_ref[pl.ds(r1, grp), :] * hb
        bb_ref[pl.ds(r1, grp), :] = hbull
        return hfull[grp - 1:grp, :], hbull[0:1, :]

    hf, hb = lax.fori_loop(0, n_grp, grp_step, (h0_ref[0, 0:1, :], h0_ref[0, 1:2, :]))
    hfin_ref[0, 0:1, :] = hf
    hfin_ref[0, 1:2, :] = hb
    hfin_ref[0, 2:grp, :] = jnp.zeros((grp - 2, w), F32)

    if with_output:
        for r0 in range(0, t, rows):
            h = bf_ref[r0:r0 + rows, :] + bb_ref[r0:r0 + rows, :]
            y_ref[0, r0:r0 + rows, :] = (h * jax.nn.gelu(rg_ref[0, r0:r0 + rows, :])).astype(y_ref.dtype)


def _rg(zrx, zrg, w_conv, b_conv, w_gate, b_gate, lam, h0, line, with_output):
    bsz, t, w = zrx.shape
    seq = pl.BlockSpec((1, t, w), lambda b: (b, 0, 0))
    st_spec = pl.BlockSpec((1, SUBLANES, w), lambda b: (b, 0, 0))
    out_shape = [jax.ShapeDtypeStruct((bsz, SUBLANES, w), F32)]
    out_specs = [st_spec]
    if with_output:
        out_shape = [jax.ShapeDtypeStruct((bsz, t, w), BF16)] + out_shape
        out_specs = [seq] + out_specs
    outs = pl.pallas_call(
        functools.partial(_rg_body, t=t, line=line, with_output=with_output),
        out_shape=out_shape,
        grid=(bsz,),
        in_specs=[
            seq, seq,
            _const_spec((CONV_W, w)), _const_spec((1, w)),
            _const_spec((w, 4 * w)), _const_spec((1, 4 * w)),
            _const_spec((2, w)),
            st_spec,
        ],
        out_specs=out_specs,
        scratch_shapes=[pltpu.VMEM((t, w), F32) for _ in range(4)],
        compiler_params=pltpu.CompilerParams(vmem_limit_bytes=VMEM_LIMIT),
        name="rglru",
    )(zrx, zrg, w_conv, b_conv, w_gate, b_gate, lam, h0)
    if with_output:
        return outs[0], outs[1]
    return None, outs[0]


def _dft_mats(n):
    j = np.arange(n)
    ang = 2.0 * np.pi * ((j[:, None] * j[None, :]) % n) / n
    return np.cos(ang), np.sin(ang)


def _fourier_consts(t):
    n2 = GRID_W
    n1 = t // n2
    cc, sc = _dft_mats(FNET_GD)
    eye2 = np.eye(LANES // FNET_GD)
    m3 = np.concatenate([np.kron(eye2, cc), -np.kron(eye2, sc)], axis=0) / 8.0
    c2, s2 = _dft_mats(n2)
    consts = {"m3": jnp.asarray(m3, F32)}
    if n1 == n2:
        consts["cs"] = jnp.asarray(np.concatenate([c2, s2], axis=0) / 8.0, F32)
        consts["m2"] = jnp.asarray(np.block([[c2, -s2], [s2, c2]]) / 8.0, F32)
        wk = (np.arange(n2)[:, None] * np.arange(n1)[None, :]).reshape(-1)
        ang = 2.0 * np.pi * wk / t
        consts["twc"] = jnp.asarray(np.repeat(np.cos(ang)[:, None], LANES, axis=1), F32)
        consts["tws"] = jnp.asarray(np.repeat(np.sin(ang)[:, None], LANES, axis=1), F32)
    else:
        ct, st = _dft_mats(t)
        consts["cs"] = jnp.asarray(np.concatenate([ct, st], axis=0) / np.sqrt(t), F32)
    return consts


def _fft_body(x_ref, cs_ref, m2_ref, m3_ref, twc_ref, tws_ref, y_ref, sr_ref, si_ref, dr_ref, di_ref, *, t):
    n = GRID_W
    cs = cs_ref[...].astype(BF16)
    m2 = m2_ref[...].astype(BF16)
    m3 = m3_ref[...].astype(BF16)

    def stage1(w, carry):
        xw = x_ref[0, pl.ds(w, n, stride=n), :].astype(BF16)
        a = jnp.dot(cs, xw, preferred_element_type=F32)
        ar, ai = a[0:n], a[n:2 * n]
        r0 = pl.multiple_of(w * n, n)
        tc = twc_ref[pl.ds(r0, n), :]
        ts = tws_ref[pl.ds(r0, n), :]
        sr_ref[pl.ds(r0, n), :] = ar * tc - ai * ts
        si_ref[pl.ds(r0, n), :] = ar * ts + ai * tc
        return carry

    lax.fori_loop(0, n, stage1, 0)

    def stage2(k1, carry):
        br = sr_ref[pl.ds(k1, n, stride=n), :]
        bi = si_ref[pl.ds(k1, n, stride=n), :]
        xb = jnp.concatenate([br, bi], axis=0).astype(BF16)
        d = jnp.dot(m2, xb, preferred_element_type=F32)
        dr_ref[pl.ds(k1, n, stride=n), :] = d[0:n]
        di_ref[pl.ds(k1, n, stride=n), :] = d[n:2 * n]
        return carry

    lax.fori_loop(0, n, stage2, 0)

    rows = 512
    for r0 in range(0, t, rows):
        d = jnp.concatenate([dr_ref[r0:r0 + rows, :], di_ref[r0:r0 + rows, :]], axis=1).astype(BF16)
        y_ref[0, r0:r0 + rows, :] = jnp.dot(d, m3, preferred_element_type=F32).astype(y_ref.dtype)


def _dense_dft_body(x_ref, cs_ref, m3_ref, y_ref, *, t):
    a = jnp.dot(cs_ref[...].astype(BF16), x_ref[0].astype(BF16), preferred_element_type=F32)
    d = jnp.concatenate([a[0:t], a[t:2 * t]], axis=1).astype(BF16)
    y_ref[0] = jnp.dot(d, m3_ref[...].astype(BF16), preferred_element_type=F32).astype(y_ref.dtype)


def _fourier(zf, consts):
    bsz, t, w = zf.shape
    seq = pl.BlockSpec((1, t, LANES), lambda b, j: (b, 0, j))
    common = dict(
        out_shape=jax.ShapeDtypeStruct((bsz, t, w), BF16),
        grid=(bsz, w // LANES),
        out_specs=seq,
        compiler_params=pltpu.CompilerParams(vmem_limit_bytes=VMEM_LIMIT),
    )
    if "m2" in consts:
        return pl.pallas_call(
            functools.partial(_fft_body, t=t),
            in_specs=[seq, _const_spec(consts["cs"].shape), _const_spec(consts["m2"].shape),
                      _const_spec(consts["m3"].shape), _const_spec((t, LANES)), _const_spec((t, LANES))],
            scratch_shapes=[pltpu.VMEM((t, LANES), F32) for _ in range(4)],
            name="fourier_fft", **common,
        )(zf, consts["cs"], consts["m2"], consts["m3"], consts["twc"], consts["tws"])
    return pl.pallas_call(
        functools.partial(_dense_dft_body, t=t),
        in_specs=[seq, _const_spec(consts["cs"].shape), _const_spec(consts["m3"].shape)],
        name="fourier_dense", **common,
    )(zf, consts["cs"], consts["m3"])


_FF_SPLIT = 2


def _outffn_body(yg_ref, yf_ref, yr_ref, x_ref, mod_ref, gpm_ref, gpf_ref, gqf_ref,
                 wo_ref, wg_ref, wu_ref, wd_ref, o_ref):
    y = jnp.concatenate([yg_ref[0], yf_ref[0], yr_ref[0]], axis=1)
    mix = jnp.dot(y, wo_ref[...], preferred_element_type=F32)
    x1 = x_ref[0] + mod_ref[0, 2:3, :] * _rms(mix, gpm_ref[...])
    hf = (_rms(x1, gpf_ref[...]) * (1.0 + mod_ref[0, 4:5, :]) + mod_ref[0, 3:4, :]).astype(BF16)
    fw = D_FF // _FF_SPLIT
    acc = None
    for j in range(_FF_SPLIT):
        gate = jnp.dot(hf, wg_ref[:, j * fw:(j + 1) * fw], preferred_element_type=F32)
        up = jnp.dot(hf, wu_ref[:, j * fw:(j + 1) * fw], preferred_element_type=F32)
        act = (gate * jax.nn.sigmoid(gate) * up).astype(BF16)
        part = jnp.dot(act, wd_ref[j * fw:(j + 1) * fw, :], preferred_element_type=F32)
        acc = part if acc is None else acc + part
    o_ref[0] = x1 + mod_ref[0, 5:6, :] * _rms(acc, gqf_ref[...])


def _outffn(yg, yf, yr, x, mod, mod_row, gpm, gpf, gqf, wo, wg, wu, wd, tm):
    bsz, t, d = x.shape
    if mod_row is None:
        mod_map = lambda b, i: (b, 0, 0)
    else:
        mod_map = lambda b, i: (mod_row, 0, 0)
    tile = lambda wdt: pl.BlockSpec((1, tm, wdt), lambda b, i: (b, i, 0))
    return pl.pallas_call(
        _outffn_body,
        out_shape=jax.ShapeDtypeStruct((bsz, t, d), F32),
        grid=(bsz, t // tm),
        in_specs=[
            tile(GLA_V), tile(FNET_W), tile(RG_W), tile(d),
            pl.BlockSpec((1, 6, d), mod_map),
            _const_spec((1, d)), _const_spec((1, d)), _const_spec((1, d)),
            _const_spec(wo.shape), _const_spec(wg.shape), _const_spec(wu.shape), _const_spec(wd.shape),
        ],
        out_specs=tile(d),
        compiler_params=pltpu.CompilerParams(vmem_limit_bytes=VMEM_LIMIT),
        name="outffn",
    )(yg, yf, yr, x, mod, gpm, gpf, gqf, wo, wg, wu, wd)


def _prep_w_in(w_in_l):
    off_k = GLA_QK
    off_v = off_k + GLA_QK
    off_dec = off_v + GLA_V
    off_og = off_dec + 2 * GLA_RANK
    parts = [w_in_l[:, 0:off_dec], w_in_l[:, off_og:], w_in_l[:, off_dec:off_og],
             jnp.zeros((w_in_l.shape[0], DEC_PAD - 2 * GLA_RANK), w_in_l.dtype)]
    return jnp.concatenate(parts, axis=1).astype(BF16)


def _prep_dec(w_dec_l, b_dec_l):
    hp = 2 * GLA_DK
    ws, bs = [], []
    for p in range(2):
        wp = jnp.zeros((DEC_PAD, 2 * hp), F32)
        for dirn in range(2):
            wp = wp.at[dirn * GLA_RANK:(dirn + 1) * GLA_RANK, dirn * hp:(dirn + 1) * hp].set(
                w_dec_l[dirn, :, p * hp:(p + 1) * hp])
        ws.append(wp)
        bs.append(jnp.concatenate([b_dec_l[0, p * hp:(p + 1) * hp], b_dec_l[1, p * hp:(p + 1) * hp]])[None, :])
    return jnp.stack(ws).astype(BF16), jnp.stack(bs)


def _block_diag(w4):
    h, hd, _ = w4.shape
    out = jnp.zeros((h * hd, h * hd), w4.dtype)
    for i in range(h):
        out = out.at[i * hd:(i + 1) * hd, i * hd:(i + 1) * hd].set(w4[i])
    return out


def _prep_rg(w_a, b_a, w_x, b_x):
    wg = jnp.concatenate([_block_diag(w_a[0]), _block_diag(w_x[0]), _block_diag(w_a[1]), _block_diag(w_x[1])], axis=1)
    bg = jnp.concatenate([b_a[0], b_x[0], b_a[1], b_x[1]])[None, :]
    return wg.astype(BF16), bg


def _mixer(h, mod, mod_row, lw, consts, s0, h0, line, tm, with_output):
    zq, zk, zv, zog, zf, zrx, zrg, zdec = _inproj(h, mod, mod_row, lw["g_pre_mix"], lw["w_in"], tm)
    yg, s_fin = _gla(zq, zk, zv, zog, zdec, lw["wd"], lw["bd"], lw["g_gla"], s0, with_output)
    yr, h_fin = _rg(zrx, zrg, lw["w_conv"], lw["b_conv"], lw["w_gate"], lw["b_gate"], lw["lam"], h0, line, with_output)
    yf = _fourier(zf, consts) if with_output else None
    return (yg, yf, yr), s_fin, h_fin


def kernel(x, c, ctx, c_ctx, w_ada, b_ada, g_pre_mix, g_post_mix, g_pre_ffn, g_post_ffn, w_in, w_dec, b_dec, g_gla,
           w_conv, b_conv, w_rg_a, b_rg_a, w_rg_x, b_rg_x, rg_lam, w_out, w_ffn_gate, w_ffn_up, w_ffn_down):
    bsz, t, d = x.shape
    t_ctx = ctx.shape[1]
    depth = w_ada.shape[0]

    c_all = jnp.zeros((MOD_ROWS, d), F32).at[0:bsz].set(c).at[bsz].set(c_ctx)
    mod_all = _ada(c_all, w_ada, b_ada).reshape(depth, MOD_ROWS, 6, d)

    consts_lat = _fourier_consts(t)
    consts_ctx = _fourier_consts(t_ctx)
    s_zero = jnp.zeros((bsz, 2, GLA_HEADS, GLA_DK, GLA_DV), F32)
    h_zero = jnp.zeros((bsz, SUBLANES, RG_W), F32)
    row = lambda v: v[None, :]

    h_ctx = ctx
    for l in range(depth):
        last = l == depth - 1
        wd_p, bd_p = _prep_dec(w_dec[l], b_dec[l])
        w_gate, b_gate = _prep_rg(w_rg_a[l], b_rg_a[l], w_rg_x[l], b_rg_x[l])
        lw = dict(g_pre_mix=row(g_pre_mix[l]), w_in=_prep_w_in(w_in[l]), wd=wd_p, bd=bd_p, g_gla=row(g_gla[l]),
                  w_conv=w_conv[l], b_conv=row(b_conv[l]), w_gate=w_gate, b_gate=b_gate, lam=rg_lam[l])
        ffn = (row(g_post_mix[l]), row(g_pre_ffn[l]), row(g_post_ffn[l]), w_out[l].astype(BF16),
               w_ffn_gate[l].astype(BF16), w_ffn_up[l].astype(BF16), w_ffn_down[l].astype(BF16))
        mod = mod_all[l]

        ys_c, s_ctx, hs_ctx = _mixer(h_ctx, mod, bsz, lw, consts_ctx, s_zero, h_zero, t_ctx, t_ctx, not last)
        ys_l, _, _ = _mixer(x, mod, None, lw, consts_lat, s_ctx, hs_ctx, GRID_W, 512, True)
        x = _outffn(*ys_l, x, mod, None, *ffn, 512)
        if not last:
            h_ctx = _outffn(*ys_c, h_ctx, mod, bsz, *ffn, t_ctx)
    return x
```

```python
import functools

import jax
import jax.numpy as jnp
import numpy as np
from jax import lax
from jax.experimental import pallas as pl
from jax.experimental.pallas import tpu as pltpu

F32 = jnp.float32
BF16 = jnp.bfloat16

D_MODEL = 1024
DEPTH = 4
GRID_W = 64
EPS = 1e-6

GLA_HEADS = 4
GLA_DK = 64
GLA_DV = 128
GLA_QK = GLA_HEADS * GLA_DK
GLA_V = GLA_HEADS * GLA_DV
GLA_RANK = 16
GLA_TAU = 16.0
GLA_CHUNK = 64
FNET_W = 256
FNET_GD = 64
RG_HEADS = 4
RG_HD = 64
RG_W = 256
RG_C = 8.0
CONV_W = 4
D_FF = 2816

LANES = 128
SUBLANES = 8
DEC_PAD = LANES
N_IN_PAD = 2 * GLA_QK + 2 * GLA_V + FNET_W + 2 * RG_W + DEC_PAD
MOD_ROWS = 16
VMEM_LIMIT = 56 * 1024 * 1024

_NT = (((1,), (1,)), ((), ()))
_TN = (((0,), (0,)), ((), ()))


def _rms(x, g):
    return x * lax.rsqrt(jnp.mean(x * x, axis=-1, keepdims=True) + EPS) * g


def _const_spec(shape):
    nd = len(shape)
    return pl.BlockSpec(shape, lambda *_: (0,) * nd, pipeline_mode=pl.Buffered(1))


def _ada_body(c_ref, w_ref, b_ref, o_ref):
    c = c_ref[...]
    s = (c * jax.nn.sigmoid(c)).astype(BF16)
    o_ref[0] = jnp.dot(s, w_ref[0].astype(BF16), preferred_element_type=F32) + b_ref[0]


def _ada(c_all, w_ada, b_ada):
    depth, d, n = w_ada.shape
    tn = 1536
    return pl.pallas_call(
        _ada_body,
        out_shape=jax.ShapeDtypeStruct((depth, MOD_ROWS, n), F32),
        grid=(depth, n // tn),
        in_specs=[
            pl.BlockSpec((MOD_ROWS, d), lambda l, j: (0, 0)),
            pl.BlockSpec((1, d, tn), lambda l, j: (l, 0, j)),
            pl.BlockSpec((1, 1, tn), lambda l, j: (l, 0, j)),
        ],
        out_specs=pl.BlockSpec((1, MOD_ROWS, tn), lambda l, j: (l, 0, j)),
        compiler_params=pltpu.CompilerParams(vmem_limit_bytes=VMEM_LIMIT),
        name="ada",
    )(c_all, w_ada, b_ada.reshape(depth, 1, n))


_IN_WIDTHS = (GLA_QK, GLA_QK, GLA_V, GLA_V, FNET_W, RG_W, RG_W, DEC_PAD)


def _inproj_body(x_ref, mod_ref, g_ref, w_ref, *out_refs):
    x = x_ref[0]
    h = _rms(x, g_ref[...]) * (1.0 + mod_ref[0, 1:2, :]) + mod_ref[0, 0:1, :]
    h = h.astype(BF16)
    off = 0
    for o_ref, wd in zip(out_refs, _IN_WIDTHS):
        o_ref[0] = jnp.dot(h, w_ref[:, off:off + wd], preferred_element_type=F32)
        off += wd


def _inproj(x, mod, mod_row, g, w_in_p, tm):
    bsz, t, d = x.shape
    if mod_row is None:
        mod_map = lambda b, i: (b, 0, 0)
    else:
        mod_map = lambda b, i: (mod_row, 0, 0)
    return pl.pallas_call(
        _inproj_body,
        out_shape=[jax.ShapeDtypeStruct((bsz, t, wd), F32) for wd in _IN_WIDTHS],
        grid=(bsz, t // tm),
        in_specs=[
            pl.BlockSpec((1, tm, d), lambda b, i: (b, i, 0)),
            pl.BlockSpec((1, 6, d), mod_map),
            _const_spec((1, d)),
            _const_spec((d, N_IN_PAD)),
        ],
        out_specs=[pl.BlockSpec((1, tm, wd), lambda b, i: (b, i, 0)) for wd in _IN_WIDTHS],
        compiler_params=pltpu.CompilerParams(vmem_limit_bytes=VMEM_LIMIT),
        name="inproj",
    )(x, mod, g, w_in_p)


GLA_BLOCK = 256
GLA_RINGS = 4


def _chunk_cumsum(x, reverse):
    rows, lanes = x.shape
    g = SUBLANES
    c = GLA_CHUNK
    x3 = x.reshape(rows // g, g, lanes)
    sub = lax.broadcasted_iota(jnp.int32, x3.shape, 1)
    for s in (1, 2, 4):
        if reverse:
            x3 = x3 + jnp.where(sub < g - s, pltpu.roll(x3, g - s, 1), 0.0)
        else:
            x3 = x3 + jnp.where(sub >= s, pltpu.roll(x3, s, 1), 0.0)
    edge = x3[:, 0:1, :] if reverse else x3[:, g - 1:g, :]
    tot = jnp.broadcast_to(edge, x3.shape).reshape(rows, lanes)
    pos = lax.broadcasted_iota(jnp.int32, (rows, lanes), 0) & (c - 1)
    acc = tot
    for s in (8, 16, 32):
        if reverse:
            acc = acc + jnp.where(pos < c - s, pltpu.roll(acc, rows - s, 0), 0.0)
        else:
            acc = acc + jnp.where(pos >= s, pltpu.roll(acc, s, 0), 0.0)
    sums = x3.reshape(rows, lanes) + (acc - tot)
    parts = []
    for ci in range(rows // c):
        r = ci * c if reverse else ci * c + c - 1
        parts.append(jnp.broadcast_to(acc[r:r + 1, :], (c, lanes)))
    total = parts[0] if len(parts) == 1 else jnp.concatenate(parts, axis=0)
    return sums, total


def _gla_body(q_ref, k_ref, v_ref, og_ref, dec_ref, wd_ref, bd_ref, g_ref, s0_ref, *rest, t, with_output):
    nr = GLA_RINGS
    if with_output:
        y_ref, sfin_ref, u_ref, dl_ref, qt_ref, o_ref = rest[:6]
        bufs = rest[6:]
        rings = tuple((bufs[r], bufs[nr + r], bufs[2 * nr + r], bufs[3 * nr + r]) for r in range(nr))
    else:
        sfin_ref, u_ref, dl_ref = rest[:3]
        bufs = rest[3:]
        rings = tuple((bufs[r], bufs[nr + r], None, None) for r in range(nr))
    c = GLA_CHUNK
    hp = 2 * GLA_DK
    vp = 2 * GLA_DV
    rows = min(t, GLA_BLOCK)
    cpb = rows // c
    n_blk = t // rows
    scale = GLA_DK ** -0.5

    lane = lax.broadcasted_iota(jnp.int32, (rows, hp), 1)
    head0 = lane < GLA_DK
    ar = lax.broadcasted_iota(jnp.int32, (rows, rows), 0)
    ac = lax.broadcasted_iota(jnp.int32, (rows, rows), 1)
    same_chunk = (ar // c) == (ac // c)
    wd = wd_ref[0]
    bd = bd_ref[0]

    def prepare(blk, ring):
        vb_ring, ke_ring, qt_ring, kt_ring = ring
        r0 = pl.multiple_of(blk * rows, rows)
        code = dec_ref[0, pl.ds(r0, rows), :].astype(BF16)
        logit = jnp.dot(code, wd, preferred_element_type=F32) + bd
        la = (jnp.minimum(logit, 0.0) - jnp.log(1.0 + jnp.exp(-jnp.abs(logit)))) * (1.0 / GLA_TAU)
        k = k_ref[0, pl.ds(r0, rows), :]
        vb_ring[...] = v_ref[0, pl.ds(r0, rows), :].astype(BF16)
        if with_output:
            q = q_ref[0, pl.ds(r0, rows), :] * scale
        for dirn in range(2):
            b, b_all = _chunk_cumsum(la[:, dirn * hp:(dirn + 1) * hp], reverse=dirn == 1)
            ke_ring[dirn] = (k * jnp.exp(b_all - b)).astype(BF16)
            for ci in range(cpb):
                dl_ref[dirn, blk * cpb + ci] = jnp.exp(b_all[ci * c:ci * c + SUBLANES, :])
            if with_output:
                qt = (q * jnp.exp(b)).astype(BF16)
                qt_ref[dirn, pl.ds(r0, rows), :] = qt
                qt_ring[dirn] = qt
                kt_ring[dirn] = (k * jnp.exp(-b)).astype(BF16)

    def multiply(blk, ring):
        vb_ring, ke_ring, qt_ring, kt_ring = ring
        r0 = pl.multiple_of(blk * rows, rows)
        vb = vb_ring[...]
        for dirn in range(2):
            ke = ke_ring[dirn]
            for ci in range(cpb):
                sl = slice(ci * c, (ci + 1) * c)
                upd = lax.dot_general(ke[sl], vb[sl], _TN, preferred_element_type=F32)
                u_ref[dirn, blk * cpb + ci, 0] = upd[0:GLA_DK, 0:GLA_DV]
                u_ref[dirn, blk * cpb + ci, 1] = upd[GLA_DK:hp, GLA_DV:vp]
            if not with_output:
                continue
            qt = qt_ring[dirn]
            kt = kt_ring[dirn]
            zk = jnp.zeros((rows, hp), BF16)
            keep = same_chunk & ((ar >= ac) if dirn == 0 else (ar <= ac))
            a0 = lax.dot_general(qt, jnp.where(head0, kt, zk), _NT, preferred_element_type=F32)
            a1 = lax.dot_general(qt, jnp.where(head0, zk, kt), _NT, preferred_element_type=F32)
            a0 = jnp.where(keep, a0, 0.0).astype(BF16)
            a1 = jnp.where(keep, a1, 0.0).astype(BF16)
            o_intra = jnp.concatenate(
                [jnp.dot(a0, vb[:, 0:GLA_DV], preferred_element_type=F32),
                 jnp.dot(a1, vb[:, GLA_DV:vp], preferred_element_type=F32)], axis=1)
            if dirn == 0:
                o_ref[pl.ds(r0, rows), :] = o_intra
            else:
                o_ref[pl.ds(r0, rows), :] += o_intra

    per = min(n_blk, nr)

    def pre_step(j, carry):
        i = per * j
        prepare(i, rings[0])
        for r in range(per):
            if r + 1 < per:
                prepare(i + r + 1, rings[r + 1])
            multiply(i + r, rings[r])
        return carry

    lax.fori_loop(0, n_blk // per, pre_step, 0)

    zs = jnp.zeros((GLA_DK, GLA_DV), BF16)
    for dirn in range(2):
        def seq_step(i, carry, dirn=dirn):
            s0, s1 = carry
            blk = i if dirn == 0 else n_blk - 1 - i
            r0 = pl.multiple_of(blk * rows, rows)
            for cj in range(cpb):
                ci = cj if dirn == 0 else cpb - 1 - cj
                chunk = blk * cpb + ci
                if with_output:
                    rs = pl.ds(r0 + ci * c, c)
                    sbd = jnp.concatenate(
                        [jnp.concatenate([s0.astype(BF16), zs], axis=1),
                         jnp.concatenate([zs, s1.astype(BF16)], axis=1)], axis=0)
                    o_ref[rs, :] += jnp.dot(qt_ref[dirn, rs, :], sbd, preferred_element_type=F32)
                drow = dl_ref[dirn, chunk][0:1, :]
                dcol = jnp.broadcast_to(drow, (hp, hp)).T
                s0 = s0 * dcol[0:GLA_DK, :] + u_ref[dirn, chunk, 0]
                s1 = s1 * dcol[GLA_DK:hp, :] + u_ref[dirn, chunk, 1]
            return s0, s1

        s0, s1 = lax.fori_loop(0, n_blk, seq_step, (s0_ref[0, dirn, 0], s0_ref[0, dirn, 1]), unroll=min(4, n_blk))
        sfin_ref[0, dirn, 0] = s0
        sfin_ref[0, dirn, 1] = s1

    if with_output:
        g = g_ref[...]

        def gate_step(blk, carry):
            rs = pl.ds(pl.multiple_of(blk * rows, rows), rows)
            o = o_ref[rs, :]
            og = og_ref[0, rs, :]
            o0 = _rms(o[:, 0:GLA_DV], g)
            o1 = _rms(o[:, GLA_DV:vp], g)
            gate = og * jax.nn.sigmoid(og)
            y_ref[0, rs, :] = (jnp.concatenate([o0, o1], axis=1) * gate).astype(y_ref.dtype)
            return carry

        lax.fori_loop(0, n_blk, gate_step, 0)


def _gla(zq, zk, zv, zog, zdec, wd_p, bd_p, g_gla, s0, with_output):
    bsz, t, _ = zq.shape
    hp, vp = 2 * GLA_DK, 2 * GLA_DV
    n_chunks = t // GLA_CHUNK
    seq = lambda wd: pl.BlockSpec((1, t, wd), lambda b, p: (b, 0, p))
    st_spec = pl.BlockSpec((1, 2, 2, GLA_DK, GLA_DV), lambda b, p: (b, 0, p, 0, 0))
    out_shape = [jax.ShapeDtypeStruct((bsz, 2, GLA_HEADS, GLA_DK, GLA_DV), F32)]
    out_specs = [st_spec]
    rows = min(t, GLA_BLOCK)
    nr = GLA_RINGS
    scratch = [pltpu.VMEM((2, n_chunks, 2, GLA_DK, GLA_DV), F32), pltpu.VMEM((2, n_chunks, SUBLANES, hp), F32)]
    staging = [pltpu.VMEM((rows, vp), BF16) for _ in range(nr)] + [pltpu.VMEM((2, rows, hp), BF16) for _ in range(nr)]
    if with_output:
        out_shape = [jax.ShapeDtypeStruct((bsz, t, GLA_V), BF16)] + out_shape
        out_specs = [seq(vp)] + out_specs
        scratch = scratch + [pltpu.VMEM((2, t, hp), BF16), pltpu.VMEM((t, vp), F32)]
        staging = staging + [pltpu.VMEM((2, rows, hp), BF16) for _ in range(2 * nr)]
    scratch = scratch + staging
    outs = pl.pallas_call(
        functools.partial(_gla_body, t=t, with_output=with_output),
        out_shape=out_shape,
        grid=(bsz, 2),
        in_specs=[
            seq(hp), seq(hp), seq(vp), seq(vp),
            pl.BlockSpec((1, t, DEC_PAD), lambda b, p: (b, 0, 0)),
            pl.BlockSpec((1, DEC_PAD, 2 * hp), lambda b, p: (p, 0, 0)),
            pl.BlockSpec((1, 1, 2 * hp), lambda b, p: (p, 0, 0)),
            pl.BlockSpec((1, GLA_DV), lambda b, p: (0, 0)),
            st_spec,
        ],
        out_specs=out_specs,
        scratch_shapes=scratch,
        compiler_params=pltpu.CompilerParams(vmem_limit_bytes=VMEM_LIMIT),
        name="gla",
    )(zq, zk, zv, zog, zdec, wd_p, bd_p, g_gla, s0)
    if with_output:
        return outs[0], outs[1]
    return None, outs[0]


RG_BLOCK = 256


def _sigmoid(x):
    return 0.5 * jnp.tanh(0.5 * x) + 0.5


def _rg_body(rx_ref, rg_ref, wc_ref, bc_ref, wg_ref, bg_ref, lam_ref, h0_ref, *rest, t, line, with_output):
    if with_output:
        y_ref, hfin_ref, af_ref, bf_ref, ab_ref, bb_ref = rest
    else:
        hfin_ref, af_ref, bf_ref, ab_ref, bb_ref = rest
    w = RG_W
    rows = min(t, RG_BLOCK)
    grp = SUBLANES
    pos = lax.broadcasted_iota(jnp.int32, (rows, w), 0) & (line - 1)
    sub = lax.broadcasted_iota(jnp.int32, (rows // grp, grp, w), 1)
    wc = wc_ref[...]
    bc = bc_ref[...]
    wg = wg_ref[...]
    bg = bg_ref[...]
    sp = [jax.nn.softplus(-lam_ref[d:d + 1, :]) * (-RG_C) for d in range(2)]

    def pre_step(blk, carry):
        rs = pl.ds(pl.multiple_of(blk * rows, rows), rows)
        x = rx_ref[0, rs, :]
        xm1 = jnp.where(pos >= 1, pltpu.roll(x, 1, 0), 0.0)
        xp1 = jnp.where(pos <= line - 2, pltpu.roll(x, rows - 1, 0), 0.0)
        xp2 = jnp.where(pos <= line - 3, pltpu.roll(x, rows - 2, 0), 0.0)
        u = bc + xm1 * wc[0:1, :] + x * wc[1:2, :] + xp1 * wc[2:3, :] + xp2 * wc[3:4, :]
        gates = jnp.dot(u.astype(BF16), wg, preferred_element_type=F32) + bg
        for dirn, (a_ref, b_ref) in enumerate(((af_ref, bf_ref), (ab_ref, bb_ref))):
            r = _sigmoid(gates[:, (2 * dirn) * w:(2 * dirn + 1) * w])
            i = _sigmoid(gates[:, (2 * dirn + 1) * w:(2 * dirn + 2) * w])
            log_a = r * sp[dirn]
            a = jnp.exp(log_a)
            bx = jnp.sqrt(-jnp.tanh(log_a) * (a * a + 1.0)) * (i * u)
            a3 = a.reshape(rows // grp, grp, w)
            b3 = bx.reshape(rows // grp, grp, w)
            for s in (1, 2, 4):
                if dirn == 0:
                    ok = sub >= s
                    sh = s
                else:
                    ok = sub <= grp - 1 - s
                    sh = grp - s
                b3 = jnp.where(ok, a3 * pltpu.roll(b3, sh, 1) + b3, b3)
                a3 = jnp.where(ok, a3 * pltpu.roll(a3, sh, 1), a3)
            a_ref[rs, :] = a3.reshape(rows, w)
            b_ref[rs, :] = b3.reshape(rows, w)
        return carry

    lax.fori_loop(0, t // rows, pre_step, 0)

    n_grp = t // grp

    def grp_step(i, carry):
        hf, hb = carry
        r0 = pl.multiple_of(i * grp, grp)
        hfull = bf_ref[pl.ds(r0, grp), :] + af_ref[pl.ds(r0, grp), :] * hf
        bf_ref[pl.ds(r0, grp), :] = hfull
        r1 = pl.multiple_of((n_grp - 1 - i) * grp, grp)
        hbull = bb_ref[pl.ds(r1, grp), :] + ab_ref[pl.ds(r1, grp), :] * hb
        bb_ref[pl.ds(r1, grp), :] = hbull
        return hfull[grp - 1:grp, :], hbull[0:1, :]

    hf, hb = lax.fori_loop(0, n_grp, grp_step, (h0_ref[0, 0:1, :], h0_ref[0, 1:2, :]), unroll=4)
    hfin_ref[0, 0:1, :] = hf
    hfin_ref[0, 1:2, :] = hb
    hfin_ref[0, 2:grp, :] = jnp.zeros((grp - 2, w), F32)

    if with_output:
        def out_step(blk, carry):
            rs = pl.ds(pl.multiple_of(blk * rows, rows), rows)
            h = bf_ref[rs, :] + bb_ref[rs, :]
            y_ref[0, rs, :] = (h * jax.nn.gelu(rg_ref[0, rs, :])).astype(y_ref.dtype)
            return carry

        lax.fori_loop(0, t // rows, out_step, 0)


def _rg(zrx, zrg, w_conv, b_conv, w_gate, b_gate, lam, h0, line, with_output):
    bsz, t, w = zrx.shape
    seq = pl.BlockSpec((1, t, w), lambda b: (b, 0, 0))
    st_spec = pl.BlockSpec((1, SUBLANES, w), lambda b: (b, 0, 0))
    out_shape = [jax.ShapeDtypeStruct((bsz, SUBLANES, w), F32)]
    out_specs = [st_spec]
    if with_output:
        out_shape = [jax.ShapeDtypeStruct((bsz, t, w), BF16)] + out_shape
        out_specs = [seq] + out_specs
    outs = pl.pallas_call(
        functools.partial(_rg_body, t=t, line=line, with_output=with_output),
        out_shape=out_shape,
        grid=(bsz,),
        in_specs=[
            seq, seq,
            _const_spec((CONV_W, w)), _const_spec((1, w)),
            _const_spec((w, 4 * w)), _const_spec((1, 4 * w)),
            _const_spec((2, w)),
            st_spec,
        ],
        out_specs=out_specs,
        scratch_shapes=[pltpu.VMEM((t, w), F32) for _ in range(4)],
        compiler_params=pltpu.CompilerParams(vmem_limit_bytes=VMEM_LIMIT),
        name="rglru",
    )(zrx, zrg, w_conv, b_conv, w_gate, b_gate, lam, h0)
    if with_output:
        return outs[0], outs[1]
    return None, outs[0]


FFT_PITCH = 72


def _dft_mats(n):
    j = np.arange(n)
    ang = 2.0 * np.pi * ((j[:, None] * j[None, :]) % n) / n
    return np.cos(ang), np.sin(ang)


def _fourier_consts(t):
    n2 = GRID_W
    n1 = t // n2
    cc, sc = _dft_mats(FNET_GD)
    eye2 = np.eye(LANES // FNET_GD)
    m3 = np.concatenate([np.kron(eye2, cc), -np.kron(eye2, sc)], axis=0) / 8.0
    c2, s2 = _dft_mats(n2)
    consts = {"m3": jnp.asarray(m3, F32)}
    if n1 == n2:
        consts["cs"] = jnp.asarray(np.concatenate([c2, s2], axis=0) / 8.0, F32)
        consts["m2"] = jnp.asarray(np.block([[c2, -s2], [s2, c2]]) / 8.0, F32)
        wk = (np.arange(n2)[:, None] * np.arange(n1)[None, :]).reshape(-1)
        ang = 2.0 * np.pi * wk / t
        consts["twc"] = jnp.asarray(np.repeat(np.cos(ang)[:, None], LANES, axis=1), F32)
        consts["tws"] = jnp.asarray(np.repeat(np.sin(ang)[:, None], LANES, axis=1), F32)
    else:
        ct, st = _dft_mats(t)
        consts["cs"] = jnp.asarray(np.concatenate([ct, st], axis=0) / np.sqrt(t), F32)
    return consts


def _fft_body(x_ref, cs_ref, m2_ref, m3_ref, twc_ref, tws_ref, y_ref, sr_ref, si_ref, dr_ref, di_ref, *, t):
    n = GRID_W
    p = FFT_PITCH
    cs = cs_ref[...].astype(BF16)
    m2 = m2_ref[...].astype(BF16)
    m3 = m3_ref[...].astype(BF16)

    def stage1(i, carry):
        w0 = 2 * i
        xw = jnp.concatenate([x_ref[0, pl.ds(w0 + j, n, stride=n), :] for j in range(2)], axis=1)
        a = jnp.dot(cs, xw.astype(BF16), preferred_element_type=F32)
        for j in range(2):
            ar = a[0:n, j * LANES:(j + 1) * LANES]
            ai = a[n:2 * n, j * LANES:(j + 1) * LANES]
            r0 = pl.multiple_of((w0 + j) * n, n)
            tc = twc_ref[pl.ds(r0, n), :]
            ts = tws_ref[pl.ds(r0, n), :]
            sr_ref[pl.ds(w0 + j, n, stride=p), :] = ar * tc - ai * ts
            si_ref[pl.ds(w0 + j, n, stride=p), :] = ar * ts + ai * tc
        return carry

    lax.fori_loop(0, n // 2, stage1, 0, unroll=4)

    def stage2(i, carry):
        k0 = 2 * i
        cols = []
        for j in range(2):
            r0 = pl.multiple_of((k0 + j) * p, SUBLANES)
            cols.append(jnp.concatenate([sr_ref[pl.ds(r0, n), :], si_ref[pl.ds(r0, n), :]], axis=0))
        xb = jnp.concatenate(cols, axis=1).astype(BF16)
        d = jnp.dot(m2, xb, preferred_element_type=F32)
        for j in range(2):
            dr_ref[pl.ds(k0 + j, n, stride=p), :] = d[0:n, j * LANES:(j + 1) * LANES]
            di_ref[pl.ds(k0 + j, n, stride=p), :] = d[n:2 * n, j * LANES:(j + 1) * LANES]
        return carry

    lax.fori_loop(0, n // 2, stage2, 0, unroll=4)

    per = 8

    def stage3(i, carry):
        blocks = []
        for j in range(per):
            r0 = pl.multiple_of((i * per + j) * p, SUBLANES)
            blocks.append(jnp.concatenate([dr_ref[pl.ds(r0, n), :], di_ref[pl.ds(r0, n), :]], axis=1))
        d = jnp.concatenate(blocks, axis=0).astype(BF16)
        rs = pl.ds(pl.multiple_of(i * per * n, per * n), per * n)
        y_ref[0, rs, :] = jnp.dot(d, m3, preferred_element_type=F32).astype(y_ref.dtype)
        return carry

    lax.fori_loop(0, n // per, stage3, 0)


def _dense_dft_body(x_ref, cs_ref, m3_ref, y_ref, *, t):
    a = jnp.dot(cs_ref[...].astype(BF16), x_ref[0].astype(BF16), preferred_element_type=F32)
    d = jnp.concatenate([a[0:t], a[t:2 * t]], axis=1).astype(BF16)
    y_ref[0] = jnp.dot(d, m3_ref[...].astype(BF16), preferred_element_type=F32).astype(y_ref.dtype)


def _fourier(zf, consts):
    bsz, t, w = zf.shape
    seq = pl.BlockSpec((1, t, LANES), lambda b, j: (b, 0, j))
    common = dict(
        out_shape=jax.ShapeDtypeStruct((bsz, t, w), BF16),
        grid=(bsz, w // LANES),
        out_specs=seq,
        compiler_params=pltpu.CompilerParams(vmem_limit_bytes=VMEM_LIMIT),
    )
    if "m2" in consts:
        return pl.pallas_call(
            functools.partial(_fft_body, t=t),
            in_specs=[seq, _const_spec(consts["cs"].shape), _const_spec(consts["m2"].shape),
                      _const_spec(consts["m3"].shape), _const_spec((t, LANES)), _const_spec((t, LANES))],
            scratch_shapes=[pltpu.VMEM((GRID_W * FFT_PITCH, LANES), F32) for _ in range(4)],
            name="fourier_fft", **common,
        )(zf, consts["cs"], consts["m2"], consts["m3"], consts["twc"], consts["tws"])
    return pl.pallas_call(
        functools.partial(_dense_dft_body, t=t),
        in_specs=[seq, _const_spec(consts["cs"].shape), _const_spec(consts["m3"].shape)],
        name="fourier_dense", **common,
    )(zf, consts["cs"], consts["m3"])


_FF_SPLIT = 2


def _outffn_body(yg_ref, yf_ref, yr_ref, x_ref, mod_ref, gpm_ref, gpf_ref, gqf_ref,
                 wo_ref, wg_ref, wu_ref, wd_ref, o_ref):
    y = jnp.concatenate([yg_ref[0], yf_ref[0], yr_ref[0]], axis=1)
    mix = jnp.dot(y, wo_ref[...], preferred_element_type=F32)
    x1 = x_ref[0] + mod_ref[0, 2:3, :] * _rms(mix, gpm_ref[...])
    hf = (_rms(x1, gpf_ref[...]) * (1.0 + mod_ref[0, 4:5, :]) + mod_ref[0, 3:4, :]).astype(BF16)
    fw = D_FF // _FF_SPLIT
    acc = None
    for j in range(_FF_SPLIT):
        gate = jnp.dot(hf, wg_ref[:, j * fw:(j + 1) * fw], preferred_element_type=F32)
        up = jnp.dot(hf, wu_ref[:, j * fw:(j + 1) * fw], preferred_element_type=F32)
        act = (gate * jax.nn.sigmoid(gate) * up).astype(BF16)
        part = jnp.dot(act, wd_ref[j * fw:(j + 1) * fw, :], preferred_element_type=F32)
        acc = part if acc is None else acc + part
    o_ref[0] = x1 + mod_ref[0, 5:6, :] * _rms(acc, gqf_ref[...])


def _outffn(yg, yf, yr, x, mod, mod_row, gpm, gpf, gqf, wo, wg, wu, wd, tm):
    bsz, t, d = x.shape
    if mod_row is None:
        mod_map = lambda b, i: (b, 0, 0)
    else:
        mod_map = lambda b, i: (mod_row, 0, 0)
    tile = lambda wdt: pl.BlockSpec((1, tm, wdt), lambda b, i: (b, i, 0))
    return pl.pallas_call(
        _outffn_body,
        out_shape=jax.ShapeDtypeStruct((bsz, t, d), F32),
        grid=(bsz, t // tm),
        in_specs=[
            tile(GLA_V), tile(FNET_W), tile(RG_W), tile(d),
            pl.BlockSpec((1, 6, d), mod_map),
            _const_spec((1, d)), _const_spec((1, d)), _const_spec((1, d)),
            _const_spec(wo.shape), _const_spec(wg.shape), _const_spec(wu.shape), _const_spec(wd.shape),
        ],
        out_specs=tile(d),
        compiler_params=pltpu.CompilerParams(vmem_limit_bytes=VMEM_LIMIT),
        name="outffn",
    )(yg, yf, yr, x, mod, gpm, gpf, gqf, wo, wg, wu, wd)


def _prep_w_in(w_in_l):
    off_k = GLA_QK
    off_v = off_k + GLA_QK
    off_dec = off_v + GLA_V
    off_og = off_dec + 2 * GLA_RANK
    parts = [w_in_l[:, 0:off_dec], w_in_l[:, off_og:], w_in_l[:, off_dec:off_og],
             jnp.zeros((w_in_l.shape[0], DEC_PAD - 2 * GLA_RANK), w_in_l.dtype)]
    return jnp.concatenate(parts, axis=1).astype(BF16)


def _prep_dec(w_dec_l, b_dec_l):
    hp = 2 * GLA_DK
    ws, bs = [], []
    for p in range(2):
        wp = jnp.zeros((DEC_PAD, 2 * hp), F32)
        for dirn in range(2):
            wp = wp.at[dirn * GLA_RANK:(dirn + 1) * GLA_RANK, dirn * hp:(dirn + 1) * hp].set(
                w_dec_l[dirn, :, p * hp:(p + 1) * hp])
        ws.append(wp)
        bs.append(jnp.concatenate([b_dec_l[0, p * hp:(p + 1) * hp], b_dec_l[1, p * hp:(p + 1) * hp]])[None, :])
    return jnp.stack(ws).astype(BF16), jnp.stack(bs)


def _block_diag(w4):
    h, hd, _ = w4.shape
    out = jnp.zeros((h * hd, h * hd), w4.dtype)
    for i in range(h):
        out = out.at[i * hd:(i + 1) * hd, i * hd:(i + 1) * hd].set(w4[i])
    return out


def _prep_rg(w_a, b_a, w_x, b_x):
    wg = jnp.concatenate([_block_diag(w_a[0]), _block_diag(w_x[0]), _block_diag(w_a[1]), _block_diag(w_x[1])], axis=1)
    bg = jnp.concatenate([b_a[0], b_x[0], b_a[1], b_x[1]])[None, :]
    return wg.astype(BF16), bg


def _mixer(h, mod, mod_row, lw, consts, s0, h0, line, tm, with_output):
    zq, zk, zv, zog, zf, zrx, zrg, zdec = _inproj(h, mod, mod_row, lw["g_pre_mix"], lw["w_in"], tm)
    yg, s_fin = _gla(zq, zk, zv, zog, zdec, lw["wd"], lw["bd"], lw["g_gla"], s0, with_output)
    yr, h_fin = _rg(zrx, zrg, lw["w_conv"], lw["b_conv"], lw["w_gate"], lw["b_gate"], lw["lam"], h0, line, with_output)
    yf = _fourier(zf, consts) if with_output else None
    return (yg, yf, yr), s_fin, h_fin


def kernel(x, c, ctx, c_ctx, w_ada, b_ada, g_pre_mix, g_post_mix, g_pre_ffn, g_post_ffn, w_in, w_dec, b_dec, g_gla,
           w_conv, b_conv, w_rg_a, b_rg_a, w_rg_x, b_rg_x, rg_lam, w_out, w_ffn_gate, w_ffn_up, w_ffn_down):
    bsz, t, d = x.shape
    t_ctx = ctx.shape[1]
    depth = w_ada.shape[0]

    c_all = jnp.zeros((MOD_ROWS, d), F32).at[0:bsz].set(c).at[bsz].set(c_ctx)
    mod_all = _ada(c_all, w_ada, b_ada).reshape(depth, MOD_ROWS, 6, d)

    consts_lat = _fourier_consts(t)
    consts_ctx = _fourier_consts(t_ctx)
    s_zero = jnp.zeros((bsz, 2, GLA_HEADS, GLA_DK, GLA_DV), F32)
    h_zero = jnp.zeros((bsz, SUBLANES, RG_W), F32)
    row = lambda v: v[None, :]

    h_ctx = ctx
    for l in range(depth):
        last = l == depth - 1
        wd_p, bd_p = _prep_dec(w_dec[l], b_dec[l])
        w_gate, b_gate = _prep_rg(w_rg_a[l], b_rg_a[l], w_rg_x[l], b_rg_x[l])
        lw = dict(g_pre_mix=row(g_pre_mix[l]), w_in=_prep_w_in(w_in[l]), wd=wd_p, bd=bd_p, g_gla=row(g_gla[l]),
                  w_conv=w_conv[l], b_conv=row(b_conv[l]), w_gate=w_gate, b_gate=b_gate, lam=rg_lam[l])
        ffn = (row(g_post_mix[l]), row(g_pre_ffn[l]), row(g_post_ffn[l]), w_out[l].astype(BF16),
               w_ffn_gate[l].astype(BF16), w_ffn_up[l].astype(BF16), w_ffn_down[l].astype(BF16))
        mod = mod_all[l]

        ys_c, s_ctx, hs_ctx = _mixer(h_ctx, mod, bsz, lw, consts_ctx, s_zero, h_zero, t_ctx, t_ctx, not last)
        ys_l, _, _ = _mixer(x, mod, None, lw, consts_lat, s_ctx, hs_ctx, GRID_W, 512, True)
        x = _outffn(*ys_l, x, mod, None, *ffn, 512)
        if not last:
            h_ctx = _outffn(*ys_c, h_ctx, mod, bsz, *ffn, t_ctx)
    return x
```

```python
import functools

import jax
import jax.numpy as jnp
import numpy as np
from jax import lax
from jax.experimental import pallas as pl
from jax.experimental.pallas import tpu as pltpu

F32 = jnp.float32
BF16 = jnp.bfloat16

D_MODEL = 1024
DEPTH = 4
GRID_W = 64
EPS = 1e-6

GLA_HEADS = 4
GLA_DK = 64
GLA_DV = 128
GLA_QK = GLA_HEADS * GLA_DK
GLA_V = GLA_HEADS * GLA_DV
GLA_RANK = 16
GLA_TAU = 16.0
GLA_CHUNK = 64
FNET_W = 256
FNET_GD = 64
RG_HEADS = 4
RG_HD = 64
RG_W = 256
RG_C = 8.0
CONV_W = 4
D_FF = 2816

LANES = 128
SUBLANES = 8
DEC_PAD = LANES
N_IN_PAD = 2 * GLA_QK + 2 * GLA_V + FNET_W + 2 * RG_W + DEC_PAD
MOD_ROWS = 16
VMEM_LIMIT = 56 * 1024 * 1024

_NT = (((1,), (1,)), ((), ()))
_TN = (((0,), (0,)), ((), ()))


def _rms(x, g):
    return x * lax.rsqrt(jnp.mean(x * x, axis=-1, keepdims=True) + EPS) * g


def _const_spec(shape):
    nd = len(shape)
    return pl.BlockSpec(shape, lambda *_: (0,) * nd, pipeline_mode=pl.Buffered(1))


def _layer_spec(arr, l):
    nd = arr.ndim - 1
    return pl.BlockSpec((None,) + arr.shape[1:], lambda *_: (l,) + (0,) * nd, pipeline_mode=pl.Buffered(1))


def _ada_body(c_ref, w_ref, b_ref, o_ref):
    c = c_ref[...]
    s = (c * jax.nn.sigmoid(c)).astype(BF16)
    o_ref[0] = jnp.dot(s, w_ref[0].astype(BF16), preferred_element_type=F32) + b_ref[0]


def _ada(c_all, w_ada, b_ada):
    depth, d, n = w_ada.shape
    tn = 1536
    return pl.pallas_call(
        _ada_body,
        out_shape=jax.ShapeDtypeStruct((depth, MOD_ROWS, n), F32),
        grid=(depth, n // tn),
        in_specs=[
            pl.BlockSpec((MOD_ROWS, d), lambda l, j: (0, 0)),
            pl.BlockSpec((1, d, tn), lambda l, j: (l, 0, j)),
            pl.BlockSpec((1, 1, tn), lambda l, j: (l, 0, j)),
        ],
        out_specs=pl.BlockSpec((1, MOD_ROWS, tn), lambda l, j: (l, 0, j)),
        compiler_params=pltpu.CompilerParams(vmem_limit_bytes=VMEM_LIMIT),
        name="ada",
    )(c_all, w_ada, b_ada.reshape(depth, 1, n))


ROW_SPLIT = 2

_IN_WIDTHS = (GLA_QK, GLA_QK, GLA_V, GLA_V, FNET_W, RG_W, RG_W, DEC_PAD)
_IN_F = 4


def _inproj_body(x_ref, mod_ref, g_ref, w_ref, *out_refs, f_pitch):
    tm = x_ref.shape[1]
    sub = tm // ROW_SPLIT
    for r in range(ROW_SPLIT):
        rs = slice(r * sub, (r + 1) * sub)
        h = _rms(x_ref[0, rs, :], g_ref[...]) * (1.0 + mod_ref[0, 1:2, :]) + mod_ref[0, 0:1, :]
        h = h.astype(BF16)
        off = 0
        for idx, (o_ref, wd) in enumerate(zip(out_refs, _IN_WIDTHS)):
            z = jnp.dot(h, w_ref[:, off:off + wd], preferred_element_type=F32)
            off += wd
            if idx == _IN_F and f_pitch != GRID_W:
                pad = jnp.zeros((f_pitch - GRID_W, wd), F32)
                for ln in range(sub // GRID_W):
                    base = (r * (sub // GRID_W) + ln) * f_pitch
                    o_ref[0, base:base + GRID_W, :] = z[ln * GRID_W:(ln + 1) * GRID_W, :]
                    o_ref[0, base + GRID_W:base + f_pitch, :] = pad
            else:
                o_ref[0, rs, :] = z


def _inproj(x, mod, mod_row, l, g, w_in_p, tm, f_pitch):
    bsz, t, d = x.shape
    if mod_row is None:
        mod_map = lambda b, i: (l, b, 0, 0)
    else:
        mod_map = lambda b, i: (l, mod_row, 0, 0)
    rows_of = lambda idx, n: n // GRID_W * f_pitch if idx == _IN_F else n
    return pl.pallas_call(
        functools.partial(_inproj_body, f_pitch=f_pitch),
        out_shape=[jax.ShapeDtypeStruct((bsz, rows_of(i, t), wd), F32) for i, wd in enumerate(_IN_WIDTHS)],
        grid=(bsz, t // tm),
        in_specs=[
            pl.BlockSpec((1, tm, d), lambda b, i: (b, i, 0)),
            pl.BlockSpec((None, 1, 6, d), mod_map),
            _layer_spec(g, l),
            _layer_spec(w_in_p, l),
        ],
        out_specs=[pl.BlockSpec((1, rows_of(i, tm), wd), lambda b, i: (b, i, 0)) for i, wd in enumerate(_IN_WIDTHS)],
        compiler_params=pltpu.CompilerParams(vmem_limit_bytes=VMEM_LIMIT),
        name="inproj",
    )(x, mod, g, w_in_p)


GLA_BLOCK = 256
GLA_RINGS = 4


def _chunk_cumsum(x, reverse):
    rows, lanes = x.shape
    g = SUBLANES
    c = GLA_CHUNK
    x3 = x.reshape(rows // g, g, lanes)
    sub = lax.broadcasted_iota(jnp.int32, x3.shape, 1)
    for s in (1, 2, 4):
        if reverse:
            x3 = x3 + jnp.where(sub < g - s, pltpu.roll(x3, g - s, 1), 0.0)
        else:
            x3 = x3 + jnp.where(sub >= s, pltpu.roll(x3, s, 1), 0.0)
    edge = x3[:, 0:1, :] if reverse else x3[:, g - 1:g, :]
    tot = jnp.broadcast_to(edge, x3.shape).reshape(rows, lanes)
    pos = lax.broadcasted_iota(jnp.int32, (rows, lanes), 0) & (c - 1)
    acc = tot
    for s in (8, 16, 32):
        if reverse:
            acc = acc + jnp.where(pos < c - s, pltpu.roll(acc, rows - s, 0), 0.0)
        else:
            acc = acc + jnp.where(pos >= s, pltpu.roll(acc, s, 0), 0.0)
    sums = x3.reshape(rows, lanes) + (acc - tot)
    parts = []
    for ci in range(rows // c):
        r = ci * c if reverse else ci * c + c - 1
        parts.append(jnp.broadcast_to(acc[r:r + 1, :], (c, lanes)))
    total = parts[0] if len(parts) == 1 else jnp.concatenate(parts, axis=0)
    return sums, total


def _gla_body(q_ref, k_ref, v_ref, og_ref, dec_ref, wd_ref, bd_ref, g_ref, s0_ref, *rest, t, with_output):
    nr = GLA_RINGS
    if with_output:
        y_ref, sfin_ref, u_ref, dl_ref, qt_ref, o_ref = rest[:6]
        bufs = rest[6:]
        rings = tuple((bufs[r], bufs[nr + r], bufs[2 * nr + r], bufs[3 * nr + r]) for r in range(nr))
    else:
        sfin_ref, u_ref, dl_ref = rest[:3]
        bufs = rest[3:]
        rings = tuple((bufs[r], bufs[nr + r], None, None) for r in range(nr))
    c = GLA_CHUNK
    hp = 2 * GLA_DK
    vp = 2 * GLA_DV
    rows = min(t, GLA_BLOCK)
    cpb = rows // c
    n_blk = t // rows
    scale = GLA_DK ** -0.5

    dk_head0 = lax.broadcasted_iota(jnp.int32, (hp, 2 * c), 0) < GLA_DK
    ar = lax.broadcasted_iota(jnp.int32, (2 * c, 4 * c), 0)
    ac = lax.broadcasted_iota(jnp.int32, (2 * c, 4 * c), 1)
    same_chunk = (ar // c) == ((ac // c) & 1)
    keep_f = same_chunk & ((ar & (c - 1)) >= (ac & (c - 1)))
    keep_b = same_chunk & ((ar & (c - 1)) <= (ac & (c - 1)))
    wd = wd_ref[0]
    bd = bd_ref[0]

    def prepare(blk, ring):
        vb_ring, ke_ring, qt_ring, kt_ring = ring
        r0 = pl.multiple_of(blk * rows, rows)
        code = dec_ref[0, pl.ds(r0, rows), :].astype(BF16)
        logit = jnp.dot(code, wd, preferred_element_type=F32) + bd
        la = (jnp.minimum(logit, 0.0) - jnp.log(1.0 + jnp.exp(-jnp.abs(logit)))) * (1.0 / GLA_TAU)
        k = k_ref[0, pl.ds(r0, rows), :]
        vb_ring[...] = v_ref[0, pl.ds(r0, rows), :].astype(BF16)
        if with_output:
            q = q_ref[0, pl.ds(r0, rows), :] * scale
        for dirn in range(2):
            b, b_all = _chunk_cumsum(la[:, dirn * hp:(dirn + 1) * hp], reverse=dirn == 1)
            ke_ring[dirn] = (k * jnp.exp(b_all - b)).astype(BF16)
            for ci in range(cpb):
                dl_ref[dirn, blk * cpb + ci] = jnp.exp(b_all[ci * c:ci * c + SUBLANES, :])
            if with_output:
                qt = (q * jnp.exp(b)).astype(BF16)
                qt_ref[dirn, pl.ds(r0, rows), :] = qt
                qt_ring[dirn] = qt
                kt_ring[dirn] = (k * jnp.exp(-b)).T.astype(BF16)

    def multiply(blk, ring):
        vb_ring, ke_ring, qt_ring, kt_ring = ring
        r0 = pl.multiple_of(blk * rows, rows)
        vb = vb_ring[...]
        for dirn in range(2):
            ke = ke_ring[dirn]
            for ci in range(cpb):
                sl = slice(ci * c, (ci + 1) * c)
                upd = lax.dot_general(ke[sl], vb[sl], _TN, preferred_element_type=F32)
                u_ref[dirn, blk * cpb + ci, 0] = upd[0:GLA_DK, 0:GLA_DV]
                u_ref[dirn, blk * cpb + ci, 1] = upd[GLA_DK:hp, GLA_DV:vp]
            if not with_output:
                continue
            qt = qt_ring[dirn]
            kt_t = kt_ring[dirn]
            zt = jnp.zeros((hp, 2 * c), BF16)
            zv = jnp.zeros((2 * c, GLA_DV), BF16)
            parts = []
            for cp in range(cpb // 2):
                sl = slice(cp * 2 * c, (cp + 1) * 2 * c)
                kt_cp = kt_t[:, sl]
                kbd = jnp.concatenate([jnp.where(dk_head0, kt_cp, zt), jnp.where(dk_head0, zt, kt_cp)], axis=1)
                sc = jnp.dot(qt[sl], kbd, preferred_element_type=F32)
                p = jnp.where(keep_f if dirn == 0 else keep_b, sc, 0.0).astype(BF16)
                v_cp = vb[sl]
                vbd = jnp.concatenate(
                    [jnp.concatenate([v_cp[:, 0:GLA_DV], zv], axis=1),
                     jnp.concatenate([zv, v_cp[:, GLA_DV:vp]], axis=1)], axis=0)
                parts.append(jnp.dot(p, vbd, preferred_element_type=F32))
            o_intra = parts[0] if len(parts) == 1 else jnp.concatenate(parts, axis=0)
            if dirn == 0:
                o_ref[pl.ds(r0, rows), :] = o_intra
            else:
                o_ref[pl.ds(r0, rows), :] += o_intra

    per = min(n_blk, nr)

    def pre_step(j, carry):
        i = per * j
        prepare(i, rings[0])
        for r in range(per):
            if r + 1 < per:
                prepare(i + r + 1, rings[r + 1])
            multiply(i + r, rings[r])
        return carry

    lax.fori_loop(0, n_blk // per, pre_step, 0)

    zs = jnp.zeros((GLA_DK, GLA_DV), BF16)
    for dirn in range(2):
        def seq_step(i, carry, dirn=dirn):
            s0, s1 = carry
            blk = i if dirn == 0 else n_blk - 1 - i
            r0 = pl.multiple_of(blk * rows, rows)
            for cj in range(cpb):
                ci = cj if dirn == 0 else cpb - 1 - cj
                chunk = blk * cpb + ci
                if with_output:
                    rs = pl.ds(r0 + ci * c, c)
                    sbd = jnp.concatenate(
                        [jnp.concatenate([s0.astype(BF16), zs], axis=1),
                         jnp.concatenate([zs, s1.astype(BF16)], axis=1)], axis=0)
                    o_ref[rs, :] += jnp.dot(qt_ref[dirn, rs, :], sbd, preferred_element_type=F32)
                drow = dl_ref[dirn, chunk][0:1, :]
                dcol = jnp.broadcast_to(drow, (hp, hp)).T
                s0 = s0 * dcol[0:GLA_DK, :] + u_ref[dirn, chunk, 0]
                s1 = s1 * dcol[GLA_DK:hp, :] + u_ref[dirn, chunk, 1]
            return s0, s1

        s0, s1 = lax.fori_loop(0, n_blk, seq_step, (s0_ref[0, dirn, 0], s0_ref[0, dirn, 1]), unroll=min(4, n_blk))
        sfin_ref[0, dirn, 0] = s0
        sfin_ref[0, dirn, 1] = s1

    if with_output:
        g = g_ref[...]

        def gate_step(blk, carry):
            rs = pl.ds(pl.multiple_of(blk * rows, rows), rows)
            o = o_ref[rs, :]
            og = og_ref[0, rs, :]
            o0 = _rms(o[:, 0:GLA_DV], g)
            o1 = _rms(o[:, GLA_DV:vp], g)
            gate = og * jax.nn.sigmoid(og)
            y_ref[0, rs, :] = (jnp.concatenate([o0, o1], axis=1) * gate).astype(y_ref.dtype)
            return carry

        lax.fori_loop(0, n_blk, gate_step, 0)


def _gla(zq, zk, zv, zog, zdec, l, wd_p, bd_p, g_gla, s0, with_output):
    bsz, t, _ = zq.shape
    hp, vp = 2 * GLA_DK, 2 * GLA_DV
    n_chunks = t // GLA_CHUNK
    seq = lambda wd: pl.BlockSpec((1, t, wd), lambda b, p: (b, 0, p))
    st_spec = pl.BlockSpec((1, 2, 2, GLA_DK, GLA_DV), lambda b, p: (b, 0, p, 0, 0))
    out_shape = [jax.ShapeDtypeStruct((bsz, 2, GLA_HEADS, GLA_DK, GLA_DV), F32)]
    out_specs = [st_spec]
    rows = min(t, GLA_BLOCK)
    nr = GLA_RINGS
    scratch = [pltpu.VMEM((2, n_chunks, 2, GLA_DK, GLA_DV), F32), pltpu.VMEM((2, n_chunks, SUBLANES, hp), F32)]
    staging = [pltpu.VMEM((rows, vp), BF16) for _ in range(nr)] + [pltpu.VMEM((2, rows, hp), BF16) for _ in range(nr)]
    if with_output:
        out_shape = [jax.ShapeDtypeStruct((bsz, t, GLA_V), BF16)] + out_shape
        out_specs = [seq(vp)] + out_specs
        scratch = scratch + [pltpu.VMEM((2, t, hp), BF16), pltpu.VMEM((t, vp), F32)]
        staging = staging + [pltpu.VMEM((2, rows, hp), BF16) for _ in range(nr)]
        staging = staging + [pltpu.VMEM((2, hp, rows), BF16) for _ in range(nr)]
    scratch = scratch + staging
    outs = pl.pallas_call(
        functools.partial(_gla_body, t=t, with_output=with_output),
        out_shape=out_shape,
        grid=(bsz, 2),
        in_specs=[
            seq(hp), seq(hp), seq(vp), seq(vp),
            pl.BlockSpec((1, t, DEC_PAD), lambda b, p: (b, 0, 0)),
            pl.BlockSpec((None, 1, DEC_PAD, 2 * hp), lambda b, p: (l, p, 0, 0)),
            pl.BlockSpec((None, 1, 1, 2 * hp), lambda b, p: (l, p, 0, 0)),
            _layer_spec(g_gla, l),
            st_spec,
        ],
        out_specs=out_specs,
        scratch_shapes=scratch,
        compiler_params=pltpu.CompilerParams(vmem_limit_bytes=VMEM_LIMIT),
        name="gla",
    )(zq, zk, zv, zog, zdec, wd_p, bd_p, g_gla, s0)
    if with_output:
        return outs[0], outs[1]
    return None, outs[0]


RG_BLOCK = 256


def _sigmoid(x):
    return 0.5 * jnp.tanh(0.5 * x) + 0.5


def _rg_body(rx_ref, rg_ref, wc_ref, bc_ref, wg_ref, bg_ref, lam_ref, h0_ref, *rest, t, line, with_output):
    if with_output:
        y_ref, hfin_ref, af_ref, bf_ref, ab_ref, bb_ref = rest
    else:
        hfin_ref, af_ref, bf_ref, ab_ref, bb_ref = rest
    w = RG_W
    rows = min(t, RG_BLOCK)
    grp = SUBLANES
    pos = lax.broadcasted_iota(jnp.int32, (rows, w), 0) & (line - 1)
    sub = lax.broadcasted_iota(jnp.int32, (rows // grp, grp, w), 1)
    wc = wc_ref[...]
    bc = bc_ref[...]
    wg = wg_ref[...]
    bg = bg_ref[...]
    sp = [jax.nn.softplus(-lam_ref[d:d + 1, :]) * (-RG_C) for d in range(2)]

    def pre_step(blk, carry):
        rs = pl.ds(pl.multiple_of(blk * rows, rows), rows)
        x = rx_ref[0, rs, :]
        xm1 = jnp.where(pos >= 1, pltpu.roll(x, 1, 0), 0.0)
        xp1 = jnp.where(pos <= line - 2, pltpu.roll(x, rows - 1, 0), 0.0)
        xp2 = jnp.where(pos <= line - 3, pltpu.roll(x, rows - 2, 0), 0.0)
        u = bc + xm1 * wc[0:1, :] + x * wc[1:2, :] + xp1 * wc[2:3, :] + xp2 * wc[3:4, :]
        gates = jnp.dot(u.astype(BF16), wg, preferred_element_type=F32) + bg
        for dirn, (a_ref, b_ref) in enumerate(((af_ref, bf_ref), (ab_ref, bb_ref))):
            r = _sigmoid(gates[:, (2 * dirn) * w:(2 * dirn + 1) * w])
            i = _sigmoid(gates[:, (2 * dirn + 1) * w:(2 * dirn + 2) * w])
            log_a = r * sp[dirn]
            a = jnp.exp(log_a)
            bx = jnp.sqrt(-jnp.tanh(log_a) * (a * a + 1.0)) * (i * u)
            a3 = a.reshape(rows // grp, grp, w)
            b3 = bx.reshape(rows // grp, grp, w)
            for s in (1, 2, 4):
                if dirn == 0:
                    ok = sub >= s
                    sh = s
                else:
                    ok = sub <= grp - 1 - s
                    sh = grp - s
                b3 = jnp.where(ok, a3 * pltpu.roll(b3, sh, 1) + b3, b3)
                a3 = jnp.where(ok, a3 * pltpu.roll(a3, sh, 1), a3)
            a_ref[rs, :] = a3.reshape(rows, w)
            b_ref[rs, :] = b3.reshape(rows, w)
        return carry

    lax.fori_loop(0, t // rows, pre_step, 0)

    n_grp = t // grp

    def grp_step(i, carry):
        hf, hb = carry
        r0 = pl.multiple_of(i * grp, grp)
        hfull = bf_ref[pl.ds(r0, grp), :] + af_ref[pl.ds(r0, grp), :] * hf
        bf_ref[pl.ds(r0, grp), :] = hfull
        r1 = pl.multiple_of((n_grp - 1 - i) * grp, grp)
        hbull = bb_ref[pl.ds(r1, grp), :] + ab_ref[pl.ds(r1, grp), :] * hb
        bb_ref[pl.ds(r1, grp), :] = hbull
        return hfull[grp - 1:grp, :], hbull[0:1, :]

    hf, hb = lax.fori_loop(0, n_grp, grp_step, (h0_ref[0, 0:1, :], h0_ref[0, 1:2, :]), unroll=4)
    hfin_ref[0, 0:1, :] = hf
    hfin_ref[0, 1:2, :] = hb
    hfin_ref[0, 2:grp, :] = jnp.zeros((grp - 2, w), F32)

    if with_output:
        def out_step(blk, carry):
            rs = pl.ds(pl.multiple_of(blk * rows, rows), rows)
            h = bf_ref[rs, :] + bb_ref[rs, :]
            y_ref[0, rs, :] = (h * jax.nn.gelu(rg_ref[0, rs, :])).astype(y_ref.dtype)
            return carry

        lax.fori_loop(0, t // rows, out_step, 0)


def _rg(zrx, zrg, l, w_conv, b_conv, w_gate, b_gate, lam, h0, line, with_output):
    bsz, t, w = zrx.shape
    seq = pl.BlockSpec((1, t, w), lambda b: (b, 0, 0))
    st_spec = pl.BlockSpec((1, SUBLANES, w), lambda b: (b, 0, 0))
    out_shape = [jax.ShapeDtypeStruct((bsz, SUBLANES, w), F32)]
    out_specs = [st_spec]
    if with_output:
        out_shape = [jax.ShapeDtypeStruct((bsz, t, w), BF16)] + out_shape
        out_specs = [seq] + out_specs
    outs = pl.pallas_call(
        functools.partial(_rg_body, t=t, line=line, with_output=with_output),
        out_shape=out_shape,
        grid=(bsz,),
        in_specs=[
            seq, seq,
            _layer_spec(w_conv, l), _layer_spec(b_conv, l),
            _layer_spec(w_gate, l), _layer_spec(b_gate, l),
            _layer_spec(lam, l),
            st_spec,
        ],
        out_specs=out_specs,
        scratch_shapes=[pltpu.VMEM((t, w), F32) for _ in range(4)],
        compiler_params=pltpu.CompilerParams(vmem_limit_bytes=VMEM_LIMIT),
        name="rglru",
    )(zrx, zrg, w_conv, b_conv, w_gate, b_gate, lam, h0)
    if with_output:
        return outs[0], outs[1]
    return None, outs[0]


FFT_PITCH = 72


def _dft_mats(n):
    j = np.arange(n)
    ang = 2.0 * np.pi * ((j[:, None] * j[None, :]) % n) / n
    return np.cos(ang), np.sin(ang)


def _fourier_consts(t):
    n2 = GRID_W
    n1 = t // n2
    cc, sc = _dft_mats(FNET_GD)
    eye2 = np.eye(LANES // FNET_GD)
    m3 = np.concatenate([np.kron(eye2, cc), -np.kron(eye2, sc)], axis=0) / 8.0
    c2, s2 = _dft_mats(n2)
    consts = {"m3": jnp.asarray(m3, F32)}
    if n1 == n2:
        consts["cs"] = jnp.asarray(np.concatenate([c2, s2], axis=0) / 8.0, F32)
        consts["m2"] = jnp.asarray(np.block([[c2, -s2], [s2, c2]]) / 8.0, F32)
        wk = (np.arange(n2)[:, None] * np.arange(n1)[None, :]).reshape(-1)
        ang = 2.0 * np.pi * wk / t
        consts["twc"] = jnp.asarray(np.repeat(np.cos(ang)[:, None], LANES, axis=1), F32)
        consts["tws"] = jnp.asarray(np.repeat(np.sin(ang)[:, None], LANES, axis=1), F32)
    else:
        ct, st = _dft_mats(t)
        consts["cs"] = jnp.asarray(np.concatenate([ct, st], axis=0) / np.sqrt(t), F32)
    return consts


def _fft_body(x_ref, cs_ref, m2_ref, m3_ref, twc_ref, tws_ref, y_ref, sr_ref, si_ref, dr_ref, di_ref, *, t):
    n = GRID_W
    p = FFT_PITCH
    cs = cs_ref[...].astype(BF16)
    m2 = m2_ref[...].astype(BF16)
    m3 = m3_ref[...].astype(BF16)

    def stage1(i, carry):
        w0 = 2 * i
        xw = jnp.concatenate([x_ref[0, pl.ds(w0 + j, n, stride=p), :] for j in range(2)], axis=1)
        a = jnp.dot(cs, xw.astype(BF16), preferred_element_type=F32)
        for j in range(2):
            ar = a[0:n, j * LANES:(j + 1) * LANES]
            ai = a[n:2 * n, j * LANES:(j + 1) * LANES]
            r0 = pl.multiple_of((w0 + j) * n, n)
            tc = twc_ref[pl.ds(r0, n), :]
            ts = tws_ref[pl.ds(r0, n), :]
            sr_ref[pl.ds(w0 + j, n, stride=p), :] = ar * tc - ai * ts
            si_ref[pl.ds(w0 + j, n, stride=p), :] = ar * ts + ai * tc
        return carry

    lax.fori_loop(0, n // 2, stage1, 0, unroll=8)

    def stage2(i, carry):
        k0 = 2 * i
        cols = []
        for j in range(2):
            r0 = pl.multiple_of((k0 + j) * p, SUBLANES)
            cols.append(jnp.concatenate([sr_ref[pl.ds(r0, n), :], si_ref[pl.ds(r0, n), :]], axis=0))
        xb = jnp.concatenate(cols, axis=1).astype(BF16)
        d = jnp.dot(m2, xb, preferred_element_type=F32)
        for j in range(2):
            dr_ref[pl.ds(k0 + j, n, stride=p), :] = d[0:n, j * LANES:(j + 1) * LANES]
            di_ref[pl.ds(k0 + j, n, stride=p), :] = d[n:2 * n, j * LANES:(j + 1) * LANES]
        return carry

    lax.fori_loop(0, n // 2, stage2, 0, unroll=8)

    per = 8

    def stage3(i, carry):
        blocks = []
        for j in range(per):
            r0 = pl.multiple_of((i * per + j) * p, SUBLANES)
            blocks.append(jnp.concatenate([dr_ref[pl.ds(r0, n), :], di_ref[pl.ds(r0, n), :]], axis=1))
        d = jnp.concatenate(blocks, axis=0).astype(BF16)
        rs = pl.ds(pl.multiple_of(i * per * n, per * n), per * n)
        y_ref[0, rs, :] = jnp.dot(d, m3, preferred_element_type=F32).astype(y_ref.dtype)
        return carry

    lax.fori_loop(0, n // per, stage3, 0, unroll=2)


def _dense_dft_body(x_ref, cs_ref, m3_ref, y_ref, *, t):
    a = jnp.dot(cs_ref[...].astype(BF16), x_ref[0].astype(BF16), preferred_element_type=F32)
    d = jnp.concatenate([a[0:t], a[t:2 * t]], axis=1).astype(BF16)
    y_ref[0] = jnp.dot(d, m3_ref[...].astype(BF16), preferred_element_type=F32).astype(y_ref.dtype)


def _fourier(zf, consts, t):
    bsz, t_in, w = zf.shape
    seq = pl.BlockSpec((1, t, LANES), lambda b, j: (b, 0, j))
    seq_in = pl.BlockSpec((1, t_in, LANES), lambda b, j: (b, 0, j))
    common = dict(
        out_shape=jax.ShapeDtypeStruct((bsz, t, w), BF16),
        grid=(bsz, w // LANES),
        out_specs=seq,
        compiler_params=pltpu.CompilerParams(vmem_limit_bytes=VMEM_LIMIT),
    )
    if "m2" in consts:
        return pl.pallas_call(
            functools.partial(_fft_body, t=t),
            in_specs=[seq_in, _const_spec(consts["cs"].shape), _const_spec(consts["m2"].shape),
                      _const_spec(consts["m3"].shape), _const_spec((t, LANES)), _const_spec((t, LANES))],
            scratch_shapes=[pltpu.VMEM((GRID_W * FFT_PITCH, LANES), F32) for _ in range(4)],
            name="fourier_fft", **common,
        )(zf, consts["cs"], consts["m2"], consts["m3"], consts["twc"], consts["tws"])
    return pl.pallas_call(
        functools.partial(_dense_dft_body, t=t),
        in_specs=[seq_in, _const_spec(consts["cs"].shape), _const_spec(consts["m3"].shape)],
        name="fourier_dense", **common,
    )(zf, consts["cs"], consts["m3"])


FF_PIECES = (1536, 1280)


def _outffn_body(yg_ref, yf_ref, yr_ref, x_ref, mod_ref, gpm_ref, gpf_ref, gqf_ref,
                 wo_ref, wg_ref, wu_ref, wd_ref, o_ref):
    tm = x_ref.shape[1]
    sub = tm // ROW_SPLIT
    subs = [slice(r * sub, (r + 1) * sub) for r in range(ROW_SPLIT)]
    mixes = []
    for rs in subs:
        y = jnp.concatenate([yg_ref[0, rs, :], yf_ref[0, rs, :], yr_ref[0, rs, :]], axis=1)
        mixes.append(jnp.dot(y, wo_ref[...], preferred_element_type=F32))
    x1s, hfs = [], []
    for rs, mix in zip(subs, mixes):
        x1 = x_ref[0, rs, :] + mod_ref[0, 2:3, :] * _rms(mix, gpm_ref[...])
        x1s.append(x1)
        hfs.append((_rms(x1, gpf_ref[...]) * (1.0 + mod_ref[0, 4:5, :]) + mod_ref[0, 3:4, :]).astype(BF16))
    for rs, x1, hf in zip(subs, x1s, hfs):
        acc = None
        off = 0
        for fw in FF_PIECES:
            gate = jnp.dot(hf, wg_ref[:, off:off + fw], preferred_element_type=F32)
            up = jnp.dot(hf, wu_ref[:, off:off + fw], preferred_element_type=F32)
            act = (gate * jax.nn.sigmoid(gate) * up).astype(BF16)
            part = jnp.dot(act, wd_ref[off:off + fw, :], preferred_element_type=F32)
            acc = part if acc is None else acc + part
            off += fw
        o_ref[0, rs, :] = x1 + mod_ref[0, 5:6, :] * _rms(acc, gqf_ref[...])


def _outffn(yg, yf, yr, x, mod, mod_row, l, gpm, gpf, gqf, wo, wg, wu, wd, tm):
    bsz, t, d = x.shape
    if mod_row is None:
        mod_map = lambda b, i: (l, b, 0, 0)
    else:
        mod_map = lambda b, i: (l, mod_row, 0, 0)
    tile = lambda wdt: pl.BlockSpec((1, tm, wdt), lambda b, i: (b, i, 0))
    return pl.pallas_call(
        _outffn_body,
        out_shape=jax.ShapeDtypeStruct((bsz, t, d), F32),
        grid=(bsz, t // tm),
        in_specs=[
            tile(GLA_V), tile(FNET_W), tile(RG_W), tile(d),
            pl.BlockSpec((None, 1, 6, d), mod_map),
            _layer_spec(gpm, l), _layer_spec(gpf, l), _layer_spec(gqf, l),
            _layer_spec(wo, l), _layer_spec(wg, l), _layer_spec(wu, l), _layer_spec(wd, l),
        ],
        out_specs=tile(d),
        compiler_params=pltpu.CompilerParams(vmem_limit_bytes=VMEM_LIMIT),
        name="outffn",
    )(yg, yf, yr, x, mod, gpm, gpf, gqf, wo, wg, wu, wd)


def _prep_w_in(w_in):
    off_dec = 2 * GLA_QK + GLA_V
    off_og = off_dec + 2 * GLA_RANK
    pad = jnp.zeros(w_in.shape[:2] + (DEC_PAD - 2 * GLA_RANK,), w_in.dtype)
    return jnp.concatenate([w_in[..., 0:off_dec], w_in[..., off_og:], w_in[..., off_dec:off_og], pad], axis=-1).astype(BF16)


def _prep_dec(w_dec, b_dec):
    depth = w_dec.shape[0]
    hp = 2 * GLA_DK
    wr = w_dec.reshape(depth, 2, GLA_RANK, 2, hp)
    wbd = jnp.einsum("ldrpj,de->lpdrej", wr, jnp.eye(2, dtype=w_dec.dtype))
    wbd = wbd.reshape(depth, 2, 2 * GLA_RANK, 2 * hp)
    wbd = jnp.pad(wbd, ((0, 0), (0, 0), (0, DEC_PAD - 2 * GLA_RANK), (0, 0)))
    bd = b_dec.reshape(depth, 2, 2, hp).transpose(0, 2, 1, 3).reshape(depth, 2, 1, 2 * hp)
    return wbd.astype(BF16), bd


def _block_diag(w):
    h, hd = w.shape[-3], w.shape[-2]
    out = jnp.einsum("...hij,hg->...higj", w, jnp.eye(h, dtype=w.dtype))
    return out.reshape(w.shape[:-3] + (h * hd, h * hd))


def _prep_rg(w_a, b_a, w_x, b_x):
    bda, bdx = _block_diag(w_a), _block_diag(w_x)
    wg = jnp.concatenate([bda[:, 0], bdx[:, 0], bda[:, 1], bdx[:, 1]], axis=-1)
    bg = jnp.concatenate([b_a[:, 0], b_x[:, 0], b_a[:, 1], b_x[:, 1]], axis=-1)[:, None, :]
    return wg.astype(BF16), bg


def _mixer(h, mod, mod_row, l, lw, consts, s0, h0, line, tm, with_output):
    t = h.shape[1]
    f_pitch = FFT_PITCH if "m2" in consts else GRID_W
    zq, zk, zv, zog, zf, zrx, zrg, zdec = _inproj(h, mod, mod_row, l, lw["g_pre_mix"], lw["w_in"], tm, f_pitch)
    yg, s_fin = _gla(zq, zk, zv, zog, zdec, l, lw["wd"], lw["bd"], lw["g_gla"], s0, with_output)
    yr, h_fin = _rg(zrx, zrg, l, lw["w_conv"], lw["b_conv"], lw["w_gate"], lw["b_gate"], lw["lam"], h0, line, with_output)
    yf = _fourier(zf, consts, t) if with_output else None
    return (yg, yf, yr), s_fin, h_fin


def kernel(x, c, ctx, c_ctx, w_ada, b_ada, g_pre_mix, g_post_mix, g_pre_ffn, g_post_ffn, w_in, w_dec, b_dec, g_gla,
           w_conv, b_conv, w_rg_a, b_rg_a, w_rg_x, b_rg_x, rg_lam, w_out, w_ffn_gate, w_ffn_up, w_ffn_down):
    bsz, t, d = x.shape
    t_ctx = ctx.shape[1]
    depth = w_ada.shape[0]

    c_all = jnp.concatenate([c, c_ctx[None, :], jnp.zeros((MOD_ROWS - bsz - 1, d), F32)], axis=0)
    mod = _ada(c_all, w_ada, b_ada).reshape(depth, MOD_ROWS, 6, d)

    consts_lat = _fourier_consts(t)
    consts_ctx = _fourier_consts(t_ctx)
    s_zero = jnp.zeros((bsz, 2, GLA_HEADS, GLA_DK, GLA_DV), F32)
    h_zero = jnp.zeros((bsz, SUBLANES, RG_W), F32)
    rows = lambda v: v[:, None, :]

    wd_p, bd_p = _prep_dec(w_dec, b_dec)
    w_gate, b_gate = _prep_rg(w_rg_a, b_rg_a, w_rg_x, b_rg_x)
    lw = dict(g_pre_mix=rows(g_pre_mix), w_in=_prep_w_in(w_in), wd=wd_p, bd=bd_p, g_gla=rows(g_gla),
              w_conv=w_conv, b_conv=rows(b_conv), w_gate=w_gate, b_gate=b_gate, lam=rg_lam)
    ffn = (rows(g_post_mix), rows(g_pre_ffn), rows(g_post_ffn), w_out.astype(BF16),
           w_ffn_gate.astype(BF16), w_ffn_up.astype(BF16), w_ffn_down.astype(BF16))

    h_ctx = ctx
    for l in range(depth):
        last = l == depth - 1
        ys_c, s_ctx, hs_ctx = _mixer(h_ctx, mod, bsz, l, lw, consts_ctx, s_zero, h_zero, t_ctx, t_ctx, not last)
        ys_l, _, _ = _mixer(x, mod, None, l, lw, consts_lat, s_ctx, hs_ctx, GRID_W, 512, True)
        x = _outffn(*ys_l, x, mod, None, l, *ffn, 512)
        if not last:
            h_ctx = _outffn(*ys_c, h_ctx, mod, bsz, l, *ffn, t_ctx)
    return x
```

```python
import functools

import jax
import jax.numpy as jnp
import numpy as np
from jax import lax
from jax.experimental import pallas as pl
from jax.experimental.pallas import tpu as pltpu

F32 = jnp.float32
BF16 = jnp.bfloat16

D_MODEL = 1024
DEPTH = 4
GRID_W = 64
EPS = 1e-6

GLA_HEADS = 4
GLA_DK = 64
GLA_DV = 128
GLA_QK = GLA_HEADS * GLA_DK
GLA_V = GLA_HEADS * GLA_DV
GLA_RANK = 16
GLA_TAU = 16.0
GLA_CHUNK = 64
FNET_W = 256
FNET_GD = 64
RG_HEADS = 4
RG_HD = 64
RG_W = 256
RG_C = 8.0
CONV_W = 4
D_FF = 2816

LANES = 128
SUBLANES = 8
DEC_PAD = LANES
N_IN_PAD = 2 * GLA_QK + 2 * GLA_V + FNET_W + 2 * RG_W + DEC_PAD
MOD_ROWS = 16
VMEM_LIMIT = 56 * 1024 * 1024

_NT = (((1,), (1,)), ((), ()))
_TN = (((0,), (0,)), ((), ()))


def _rms(x, g):
    return x * lax.rsqrt(jnp.mean(x * x, axis=-1, keepdims=True) + EPS) * g


def _const_spec(shape):
    nd = len(shape)
    return pl.BlockSpec(shape, lambda *_: (0,) * nd, pipeline_mode=pl.Buffered(1))


def _layer_spec(arr, l):
    nd = arr.ndim - 1
    return pl.BlockSpec((None,) + arr.shape[1:], lambda *_: (l,) + (0,) * nd, pipeline_mode=pl.Buffered(1))


def _ada_body(c_ref, w_ref, b_ref, o_ref):
    c = c_ref[...]
    s = (c * jax.nn.sigmoid(c)).astype(BF16)
    o_ref[0] = jnp.dot(s, w_ref[0].astype(BF16), preferred_element_type=F32) + b_ref[0]


def _ada(c_all, w_ada, b_ada):
    depth, d, n = w_ada.shape
    tn = 1536
    return pl.pallas_call(
        _ada_body,
        out_shape=jax.ShapeDtypeStruct((depth, MOD_ROWS, n), F32),
        grid=(depth, n // tn),
        in_specs=[
            pl.BlockSpec((MOD_ROWS, d), lambda l, j: (0, 0)),
            pl.BlockSpec((1, d, tn), lambda l, j: (l, 0, j)),
            pl.BlockSpec((1, 1, tn), lambda l, j: (l, 0, j)),
        ],
        out_specs=pl.BlockSpec((1, MOD_ROWS, tn), lambda l, j: (l, 0, j)),
        compiler_params=pltpu.CompilerParams(vmem_limit_bytes=VMEM_LIMIT),
        name="ada",
    )(c_all, w_ada, b_ada.reshape(depth, 1, n))


ROW_SPLIT = 2

_IN_WIDTHS = (GLA_QK, GLA_QK, GLA_V, GLA_V, FNET_W, RG_W, RG_W, DEC_PAD)
_IN_F = 4
_IN_RX = 5


def _sigmoid(x):
    return 0.5 * jnp.tanh(0.5 * x) + 0.5


def _rg_coeffs(x, wc, bc, wg, bg, sp, line):
    rows, w = x.shape
    grp = SUBLANES
    pos = lax.broadcasted_iota(jnp.int32, (rows, w), 0) & (line - 1)
    sub = lax.broadcasted_iota(jnp.int32, (rows // grp, grp, w), 1)
    xm1 = jnp.where(pos >= 1, pltpu.roll(x, 1, 0), 0.0)
    xp1 = jnp.where(pos <= line - 2, pltpu.roll(x, rows - 1, 0), 0.0)
    xp2 = jnp.where(pos <= line - 3, pltpu.roll(x, rows - 2, 0), 0.0)
    u = bc + xm1 * wc[0:1, :] + x * wc[1:2, :] + xp1 * wc[2:3, :] + xp2 * wc[3:4, :]
    gates = jnp.dot(u.astype(BF16), wg, preferred_element_type=F32) + bg
    out = []
    for dirn in range(2):
        r = _sigmoid(gates[:, (2 * dirn) * w:(2 * dirn + 1) * w])
        i = _sigmoid(gates[:, (2 * dirn + 1) * w:(2 * dirn + 2) * w])
        log_a = r * sp[dirn]
        a = jnp.exp(log_a)
        bx = jnp.sqrt(-jnp.tanh(log_a) * (a * a + 1.0)) * (i * u)
        a3 = a.reshape(rows // grp, grp, w)
        b3 = bx.reshape(rows // grp, grp, w)
        for s in (1, 2, 4):
            if dirn == 0:
                ok = sub >= s
                sh = s
            else:
                ok = sub <= grp - 1 - s
                sh = grp - s
            b3 = jnp.where(ok, a3 * pltpu.roll(b3, sh, 1) + b3, b3)
            a3 = jnp.where(ok, a3 * pltpu.roll(a3, sh, 1), a3)
        out.append((a3.reshape(rows, w), b3.reshape(rows, w)))
    return out


def _inproj_body(x_ref, mod_ref, g_ref, w_ref, wc_ref, bc_ref, wg_ref, bg_ref, lam_ref, *out_refs,
                 f_pitch, line, row_split):
    z_refs = out_refs[:len(_IN_WIDTHS) - 1]
    coef_refs = out_refs[len(_IN_WIDTHS) - 1:]
    tm = x_ref.shape[1]
    sub = tm // row_split
    offs = [sum(_IN_WIDTHS[:i]) for i in range(len(_IN_WIDTHS))]
    sp = [jax.nn.softplus(-lam_ref[d:d + 1, :]) * (-RG_C) for d in range(2)]
    subs = [slice(r * sub, (r + 1) * sub) for r in range(row_split)]
    hs = []
    for rs in subs:
        h = _rms(x_ref[0, rs, :], g_ref[...]) * (1.0 + mod_ref[0, 1:2, :]) + mod_ref[0, 0:1, :]
        hs.append(h.astype(BF16))
    proj = lambda h, idx: jnp.dot(h, w_ref[:, offs[idx]:offs[idx] + _IN_WIDTHS[idx]], preferred_element_type=F32)

    def z_task(r, k, idx):
        def run():
            o_ref = z_refs[k]
            z = proj(hs[r], idx)
            if idx == _IN_F and f_pitch != GRID_W:
                pad = jnp.zeros((f_pitch - GRID_W, _IN_WIDTHS[idx]), F32)
                for ln in range(sub // GRID_W):
                    base = (r * (sub // GRID_W) + ln) * f_pitch
                    o_ref[0, base:base + GRID_W, :] = z[ln * GRID_W:(ln + 1) * GRID_W, :]
                    o_ref[0, base + GRID_W:base + f_pitch, :] = pad
            else:
                o_ref[0, subs[r], :] = z
        return run

    def rg_task(r, rx, c0):
        def run():
            coefs = _rg_coeffs(rx[c0:c0 + line], wc_ref[...], bc_ref[...], wg_ref[...], bg_ref[...], sp, line)
            rows = slice(r * sub + c0, r * sub + c0 + line)
            for dirn in range(2):
                coef_refs[2 * dirn][0, rows, :] = coefs[dirn][0]
                coef_refs[2 * dirn + 1][0, rows, :] = coefs[dirn][1]
        return run

    rxs = [proj(h, _IN_RX) for h in hs]
    others = [i for i in range(len(_IN_WIDTHS)) if i != _IN_RX]
    mxu_tasks = [z_task(r, k, idx) for r in range(row_split) for k, idx in enumerate(others)]
    rg_tasks = [rg_task(r, rxs[r], c0) for r in range(row_split) for c0 in range(0, sub, line)]
    per = -(-len(mxu_tasks) // len(rg_tasks))
    for j, task in enumerate(rg_tasks):
        for m in mxu_tasks[j * per:(j + 1) * per]:
            m()
        task()
    for m in mxu_tasks[len(rg_tasks) * per:]:
        m()


def _inproj(x, mod, mod_row, l, g, w_in_p, rgw, tm, f_pitch, line):
    bsz, t, d = x.shape
    if mod_row is None:
        mod_map = lambda b, i: (l, b, 0, 0)
    else:
        mod_map = lambda b, i: (l, mod_row, 0, 0)
    row_split = ROW_SPLIT if (tm // ROW_SPLIT) % line == 0 else 1
    rows_of = lambda idx, n: n // GRID_W * f_pitch if idx == _IN_F else n
    groups = [i for i in range(len(_IN_WIDTHS)) if i != _IN_RX]
    out_shape = [jax.ShapeDtypeStruct((bsz, rows_of(i, t), _IN_WIDTHS[i]), F32) for i in groups]
    out_specs = [pl.BlockSpec((1, rows_of(i, tm), _IN_WIDTHS[i]), lambda b, i_: (b, i_, 0)) for i in groups]
    out_shape += [jax.ShapeDtypeStruct((bsz, t, RG_W), F32) for _ in range(4)]
    out_specs += [pl.BlockSpec((1, tm, RG_W), lambda b, i_: (b, i_, 0)) for _ in range(4)]
    outs = pl.pallas_call(
        functools.partial(_inproj_body, f_pitch=f_pitch, line=line, row_split=row_split),
        out_shape=out_shape,
        grid=(bsz, t // tm),
        in_specs=[
            pl.BlockSpec((1, tm, d), lambda b, i: (b, i, 0)),
            pl.BlockSpec((None, 1, 6, d), mod_map),
            _layer_spec(g, l),
            _layer_spec(w_in_p, l),
        ] + [_layer_spec(a, l) for a in rgw],
        out_specs=out_specs,
        compiler_params=pltpu.CompilerParams(vmem_limit_bytes=VMEM_LIMIT),
        name="inproj",
    )(x, mod, g, w_in_p, *rgw)
    return outs[:len(groups)], outs[len(groups):]


GLA_BLOCK = 256
GLA_RINGS = 8


def _chunk_cumsum(x, reverse):
    rows, lanes = x.shape
    g = SUBLANES
    c = GLA_CHUNK
    x3 = x.reshape(rows // g, g, lanes)
    sub = lax.broadcasted_iota(jnp.int32, x3.shape, 1)
    for s in (1, 2, 4):
        if reverse:
            x3 = x3 + jnp.where(sub < g - s, pltpu.roll(x3, g - s, 1), 0.0)
        else:
            x3 = x3 + jnp.where(sub >= s, pltpu.roll(x3, s, 1), 0.0)
    edge = x3[:, 0:1, :] if reverse else x3[:, g - 1:g, :]
    tot = jnp.broadcast_to(edge, x3.shape).reshape(rows, lanes)
    pos = lax.broadcasted_iota(jnp.int32, (rows, lanes), 0) & (c - 1)
    acc = tot
    for s in (8, 16, 32):
        if reverse:
            acc = acc + jnp.where(pos < c - s, pltpu.roll(acc, rows - s, 0), 0.0)
        else:
            acc = acc + jnp.where(pos >= s, pltpu.roll(acc, s, 0), 0.0)
    sums = x3.reshape(rows, lanes) + (acc - tot)
    parts = []
    for ci in range(rows // c):
        r = ci * c if reverse else ci * c + c - 1
        parts.append(jnp.broadcast_to(acc[r:r + 1, :], (c, lanes)))
    total = parts[0] if len(parts) == 1 else jnp.concatenate(parts, axis=0)
    return sums, total


def _gla_body(q_ref, k_ref, v_ref, og_ref, dec_ref, wd_ref, bd_ref, g_ref, s0_ref, *rest, t, with_output):
    nr = GLA_RINGS
    if with_output:
        y_ref, sfin_ref, u_ref, dl_ref, qt_ref, o_ref = rest[:6]
        bufs = rest[6:]
        rings = tuple((bufs[r], bufs[nr + r], bufs[2 * nr + r], bufs[3 * nr + r]) for r in range(nr))
    else:
        sfin_ref, u_ref, dl_ref = rest[:3]
        bufs = rest[3:]
        rings = tuple((bufs[r], bufs[nr + r], None, None) for r in range(nr))
    c = GLA_CHUNK
    hp = 2 * GLA_DK
    vp = 2 * GLA_DV
    rows = min(t, GLA_BLOCK)
    cpb = rows // c
    n_blk = t // rows
    scale = GLA_DK ** -0.5

    dk_head0 = lax.broadcasted_iota(jnp.int32, (hp, 2 * c), 0) < GLA_DK
    ar = lax.broadcasted_iota(jnp.int32, (2 * c, 4 * c), 0)
    ac = lax.broadcasted_iota(jnp.int32, (2 * c, 4 * c), 1)
    same_chunk = (ar // c) == ((ac // c) & 1)
    keep_f = same_chunk & ((ar & (c - 1)) >= (ac & (c - 1)))
    keep_b = same_chunk & ((ar & (c - 1)) <= (ac & (c - 1)))
    wd = wd_ref[0]
    bd = bd_ref[0]

    def prepare(blk, ring):
        vb_ring, ke_ring, qt_ring, kt_ring = ring
        r0 = pl.multiple_of(blk * rows, rows)
        code = dec_ref[0, pl.ds(r0, rows), :].astype(BF16)
        logit = jnp.dot(code, wd, preferred_element_type=F32) + bd
        la = (jnp.minimum(logit, 0.0) - jnp.log(1.0 + jnp.exp(-jnp.abs(logit)))) * (1.0 / GLA_TAU)
        k = k_ref[0, pl.ds(r0, rows), :]
        vb_ring[...] = v_ref[0, pl.ds(r0, rows), :].astype(BF16)
        if with_output:
            q = q_ref[0, pl.ds(r0, rows), :] * scale
        for dirn in range(2):
            b, b_all = _chunk_cumsum(la[:, dirn * hp:(dirn + 1) * hp], reverse=dirn == 1)
            ke_ring[dirn] = (k * jnp.exp(b_all - b)).astype(BF16)
            for ci in range(cpb):
                dl_ref[dirn, blk * cpb + ci] = jnp.exp(b_all[ci * c:ci * c + SUBLANES, :])
            if with_output:
                qt = (q * jnp.exp(b)).astype(BF16)
                qt_ref[dirn, pl.ds(r0, rows), :] = qt
                qt_ring[dirn] = qt
                kt_ring[dirn] = (k * jnp.exp(-b)).T.astype(BF16)

    def multiply(blk, ring):
        vb_ring, ke_ring, qt_ring, kt_ring = ring
        r0 = pl.multiple_of(blk * rows, rows)
        vb = vb_ring[...]
        for dirn in range(2):
            ke = ke_ring[dirn]
            for ci in range(cpb):
                sl = slice(ci * c, (ci + 1) * c)
                upd = lax.dot_general(ke[sl], vb[sl], _TN, preferred_element_type=F32)
                u_ref[dirn, blk * cpb + ci, 0] = upd[0:GLA_DK, 0:GLA_DV]
                u_ref[dirn, blk * cpb + ci, 1] = upd[GLA_DK:hp, GLA_DV:vp]
            if not with_output:
                continue
            qt = qt_ring[dirn]
            kt_t = kt_ring[dirn]
            zt = jnp.zeros((hp, 2 * c), BF16)
            zv = jnp.zeros((2 * c, GLA_DV), BF16)
            parts = []
            for cp in range(cpb // 2):
                sl = slice(cp * 2 * c, (cp + 1) * 2 * c)
                kt_cp = kt_t[:, sl]
                kbd = jnp.concatenate([jnp.where(dk_head0, kt_cp, zt), jnp.where(dk_head0, zt, kt_cp)], axis=1)
                sc = jnp.dot(qt[sl], kbd, preferred_element_type=F32)
                p = jnp.where(keep_f if dirn == 0 else keep_b, sc, 0.0).astype(BF16)
                v_cp = vb[sl]
                vbd = jnp.concatenate(
                    [jnp.concatenate([v_cp[:, 0:GLA_DV], zv], axis=1),
                     jnp.concatenate([zv, v_cp[:, GLA_DV:vp]], axis=1)], axis=0)
                parts.append(jnp.dot(p, vbd, preferred_element_type=F32))
            o_intra = parts[0] if len(parts) == 1 else jnp.concatenate(parts, axis=0)
            if dirn == 0:
                o_ref[pl.ds(r0, rows), :] = o_intra
            else:
                o_ref[pl.ds(r0, rows), :] += o_intra

    per = min(n_blk, nr)

    def pre_step(j, carry):
        i = per * j
        prepare(i, rings[0])
        for r in range(per):
            if r + 1 < per:
                prepare(i + r + 1, rings[r + 1])
            multiply(i + r, rings[r])
        return carry

    lax.fori_loop(0, n_blk // per, pre_step, 0)

    zs = jnp.zeros((GLA_DK, GLA_DV), BF16)
    for dirn in range(2):
        def seq_step(i, carry, dirn=dirn):
            s0, s1 = carry
            blk = i if dirn == 0 else n_blk - 1 - i
            r0 = pl.multiple_of(blk * rows, rows)
            for cj in range(cpb):
                ci = cj if dirn == 0 else cpb - 1 - cj
                chunk = blk * cpb + ci
                if with_output:
                    rs = pl.ds(r0 + ci * c, c)
                    sbd = jnp.concatenate(
                        [jnp.concatenate([s0.astype(BF16), zs], axis=1),
                         jnp.concatenate([zs, s1.astype(BF16)], axis=1)], axis=0)
                    o_ref[rs, :] += jnp.dot(qt_ref[dirn, rs, :], sbd, preferred_element_type=F32)
                drow = dl_ref[dirn, chunk][0:1, :]
                dcol = jnp.broadcast_to(drow, (hp, hp)).T
                s0 = s0 * dcol[0:GLA_DK, :] + u_ref[dirn, chunk, 0]
                s1 = s1 * dcol[GLA_DK:hp, :] + u_ref[dirn, chunk, 1]
            return s0, s1

        s0, s1 = lax.fori_loop(0, n_blk, seq_step, (s0_ref[0, dirn, 0], s0_ref[0, dirn, 1]), unroll=min(4, n_blk))
        sfin_ref[0, dirn, 0] = s0
        sfin_ref[0, dirn, 1] = s1

    if with_output:
        g = g_ref[...]

        def gate_step(blk, carry):
            rs = pl.ds(pl.multiple_of(blk * rows, rows), rows)
            o = o_ref[rs, :]
            og = og_ref[0, rs, :]
            o0 = _rms(o[:, 0:GLA_DV], g)
            o1 = _rms(o[:, GLA_DV:vp], g)
            gate = og * _sigmoid(og)
            y_ref[0, rs, :] = (jnp.concatenate([o0, o1], axis=1) * gate).astype(y_ref.dtype)
            return carry

        lax.fori_loop(0, n_blk, gate_step, 0)


def _gla(zq, zk, zv, zog, zdec, l, wd_p, bd_p, g_gla, s0, with_output):
    bsz, t, _ = zq.shape
    hp, vp = 2 * GLA_DK, 2 * GLA_DV
    n_chunks = t // GLA_CHUNK
    seq = lambda wd: pl.BlockSpec((1, t, wd), lambda b, p: (b, 0, p))
    st_spec = pl.BlockSpec((1, 2, 2, GLA_DK, GLA_DV), lambda b, p: (b, 0, p, 0, 0))
    out_shape = [jax.ShapeDtypeStruct((bsz, 2, GLA_HEADS, GLA_DK, GLA_DV), F32)]
    out_specs = [st_spec]
    rows = min(t, GLA_BLOCK)
    nr = GLA_RINGS
    scratch = [pltpu.VMEM((2, n_chunks, 2, GLA_DK, GLA_DV), F32), pltpu.VMEM((2, n_chunks, SUBLANES, hp), F32)]
    staging = [pltpu.VMEM((rows, vp), BF16) for _ in range(nr)] + [pltpu.VMEM((2, rows, hp), BF16) for _ in range(nr)]
    if with_output:
        out_shape = [jax.ShapeDtypeStruct((bsz, t, GLA_V), BF16)] + out_shape
        out_specs = [seq(vp)] + out_specs
        scratch = scratch + [pltpu.VMEM((2, t, hp), BF16), pltpu.VMEM((t, vp), F32)]
        staging = staging + [pltpu.VMEM((2, rows, hp), BF16) for _ in range(nr)]
        staging = staging + [pltpu.VMEM((2, hp, rows), BF16) for _ in range(nr)]
    scratch = scratch + staging
    outs = pl.pallas_call(
        functools.partial(_gla_body, t=t, with_output=with_output),
        out_shape=out_shape,
        grid=(bsz, 2),
        in_specs=[
            seq(hp), seq(hp), seq(vp), seq(vp),
            pl.BlockSpec((1, t, DEC_PAD), lambda b, p: (b, 0, 0)),
            pl.BlockSpec((None, 1, DEC_PAD, 2 * hp), lambda b, p: (l, p, 0, 0)),
            pl.BlockSpec((None, 1, 1, 2 * hp), lambda b, p: (l, p, 0, 0)),
            _layer_spec(g_gla, l),
            st_spec,
        ],
        out_specs=out_specs,
        scratch_shapes=scratch,
        compiler_params=pltpu.CompilerParams(vmem_limit_bytes=VMEM_LIMIT),
        name="gla",
    )(zq, zk, zv, zog, zdec, wd_p, bd_p, g_gla, s0)
    if with_output:
        return outs[0], outs[1]
    return None, outs[0]


def _rg_body(af_ref, bf_ref, ab_ref, bb_ref, h0_ref, *rest, t, with_output):
    if with_output:
        h_ref, hfin_ref, hf_s, hb_s = rest
    else:
        (hfin_ref,) = rest
    grp = SUBLANES
    n_grp = t // grp

    def grp_step(i, carry):
        hf, hb = carry
        r0 = pl.multiple_of(i * grp, grp)
        hfull = bf_ref[0, pl.ds(r0, grp), :] + af_ref[0, pl.ds(r0, grp), :] * hf
        r1 = pl.multiple_of((n_grp - 1 - i) * grp, grp)
        hbull = bb_ref[0, pl.ds(r1, grp), :] + ab_ref[0, pl.ds(r1, grp), :] * hb
        if with_output:
            hf_s[pl.ds(r0, grp), :] = hfull
            hb_s[pl.ds(r1, grp), :] = hbull
        return hfull[grp - 1:grp, :], hbull[0:1, :]

    hf, hb = lax.fori_loop(0, n_grp, grp_step, (h0_ref[0, 0:1, :], h0_ref[0, 1:2, :]), unroll=8)
    hfin_ref[0, 0:1, :] = hf
    hfin_ref[0, 1:2, :] = hb
    hfin_ref[0, 2:grp, :] = jnp.zeros((grp - 2, hf.shape[1]), F32)

    if with_output:
        rows = min(t, 512)

        def out_step(blk, carry):
            rs = pl.ds(pl.multiple_of(blk * rows, rows), rows)
            h_ref[0, rs, :] = hf_s[rs, :] + hb_s[rs, :]
            return carry

        lax.fori_loop(0, t // rows, out_step, 0)


def _rg(coefs, h0, with_output):
    bsz, t, w = coefs[0].shape
    seq = pl.BlockSpec((1, t, LANES), lambda b, j: (b, 0, j))
    st_spec = pl.BlockSpec((1, SUBLANES, LANES), lambda b, j: (b, 0, j))
    out_shape = [jax.ShapeDtypeStruct((bsz, SUBLANES, w), F32)]
    out_specs = [st_spec]
    scratch = []
    if with_output:
        out_shape = [jax.ShapeDtypeStruct((bsz, t, w), F32)] + out_shape
        out_specs = [seq] + out_specs
        scratch = [pltpu.VMEM((t, LANES), F32) for _ in range(2)]
    outs = pl.pallas_call(
        functools.partial(_rg_body, t=t, with_output=with_output),
        out_shape=out_shape,
        grid=(bsz, w // LANES),
        in_specs=[seq, seq, seq, seq, st_spec],
        out_specs=out_specs,
        scratch_shapes=scratch,
        compiler_params=pltpu.CompilerParams(vmem_limit_bytes=VMEM_LIMIT),
        name="rglru",
    )(*coefs, h0)
    if with_output:
        return outs[0], outs[1]
    return None, outs[0]


FFT_PITCH = 72


def _dft_mats(n):
    j = np.arange(n)
    ang = 2.0 * np.pi * ((j[:, None] * j[None, :]) % n) / n
    return np.cos(ang), np.sin(ang)


def _fourier_consts(t):
    n2 = GRID_W
    n1 = t // n2
    cc, sc = _dft_mats(FNET_GD)
    eye2 = np.eye(LANES // FNET_GD)
    m3 = np.concatenate([np.kron(eye2, cc), -np.kron(eye2, sc)], axis=0) / 8.0
    c2, s2 = _dft_mats(n2)
    consts = {"m3": jnp.asarray(m3, F32)}
    if n1 == n2:
        consts["cs"] = jnp.asarray(np.concatenate([c2, s2], axis=0) / 8.0, F32)
        consts["m2"] = jnp.asarray(np.block([[c2, -s2], [s2, c2]]) / 8.0, F32)
        wk = (np.arange(n2)[:, None] * np.arange(n1)[None, :]).reshape(-1)
        ang = 2.0 * np.pi * wk / t
        consts["twc"] = jnp.asarray(np.repeat(np.cos(ang)[:, None], LANES, axis=1), F32)
        consts["tws"] = jnp.asarray(np.repeat(np.sin(ang)[:, None], LANES, axis=1), F32)
    else:
        ct, st = _dft_mats(t)
        consts["cs"] = jnp.asarray(np.concatenate([ct, st], axis=0) / np.sqrt(t), F32)
    return consts


def _fft_body(x_ref, cs_ref, m2_ref, m3_ref, twc_ref, tws_ref, y_ref, sr_ref, si_ref, dr_ref, di_ref, *, t):
    n = GRID_W
    p = FFT_PITCH
    cs = cs_ref[...].astype(BF16)
    m2 = m2_ref[...].astype(BF16)
    m3 = m3_ref[...].astype(BF16)

    def stage1(i, carry):
        w0 = 2 * i
        xw = jnp.concatenate([x_ref[0, pl.ds(w0 + j, n, stride=p), :] for j in range(2)], axis=1)
        a = jnp.dot(cs, xw.astype(BF16), preferred_element_type=F32)
        for j in range(2):
            ar = a[0:n, j * LANES:(j + 1) * LANES]
            ai = a[n:2 * n, j * LANES:(j + 1) * LANES]
            r0 = pl.multiple_of((w0 + j) * n, n)
            tc = twc_ref[pl.ds(r0, n), :]
            ts = tws_ref[pl.ds(r0, n), :]
            sr_ref[pl.ds(w0 + j, n, stride=p), :] = ar * tc - ai * ts
            si_ref[pl.ds(w0 + j, n, stride=p), :] = ar * ts + ai * tc
        return carry

    lax.fori_loop(0, n // 2, stage1, 0, unroll=8)

    def stage2(i, carry):
        k0 = 2 * i
        cols = []
        for j in range(2):
            r0 = pl.multiple_of((k0 + j) * p, SUBLANES)
            cols.append(jnp.concatenate([sr_ref[pl.ds(r0, n), :], si_ref[pl.ds(r0, n), :]], axis=0))
        xb = jnp.concatenate(cols, axis=1).astype(BF16)
        d = jnp.dot(m2, xb, preferred_element_type=F32)
        for j in range(2):
            dr_ref[pl.ds(k0 + j, n, stride=p), :] = d[0:n, j * LANES:(j + 1) * LANES]
            di_ref[pl.ds(k0 + j, n, stride=p), :] = d[n:2 * n, j * LANES:(j + 1) * LANES]
        return carry

    lax.fori_loop(0, n // 2, stage2, 0, unroll=8)

    per = 8

    def stage3(i, carry):
        blocks = []
        for j in range(per):
            r0 = pl.multiple_of((i * per + j) * p, SUBLANES)
            blocks.append(jnp.concatenate([dr_ref[pl.ds(r0, n), :], di_ref[pl.ds(r0, n), :]], axis=1))
        d = jnp.concatenate(blocks, axis=0).astype(BF16)
        rs = pl.ds(pl.multiple_of(i * per * n, per * n), per * n)
        y_ref[0, rs, :] = jnp.dot(d, m3, preferred_element_type=F32).astype(y_ref.dtype)
        return carry

    lax.fori_loop(0, n // per, stage3, 0, unroll=2)


def _dense_dft_body(x_ref, cs_ref, m3_ref, y_ref, *, t):
    a = jnp.dot(cs_ref[...].astype(BF16), x_ref[0].astype(BF16), preferred_element_type=F32)
    d = jnp.concatenate([a[0:t], a[t:2 * t]], axis=1).astype(BF16)
    y_ref[0] = jnp.dot(d, m3_ref[...].astype(BF16), preferred_element_type=F32).astype(y_ref.dtype)


def _fourier(zf, consts, t):
    bsz, t_in, w = zf.shape
    seq = pl.BlockSpec((1, t, LANES), lambda b, j: (b, 0, j))
    seq_in = pl.BlockSpec((1, t_in, LANES), lambda b, j: (b, 0, j))
    common = dict(
        out_shape=jax.ShapeDtypeStruct((bsz, t, w), BF16),
        grid=(bsz, w // LANES),
        out_specs=seq,
        compiler_params=pltpu.CompilerParams(vmem_limit_bytes=VMEM_LIMIT),
    )
    if "m2" in consts:
        return pl.pallas_call(
            functools.partial(_fft_body, t=t),
            in_specs=[seq_in, _const_spec(consts["cs"].shape), _const_spec(consts["m2"].shape),
                      _const_spec(consts["m3"].shape), _const_spec((t, LANES)), _const_spec((t, LANES))],
            scratch_shapes=[pltpu.VMEM((GRID_W * FFT_PITCH, LANES), F32) for _ in range(4)],
            name="fourier_fft", **common,
        )(zf, consts["cs"], consts["m2"], consts["m3"], consts["twc"], consts["tws"])
    return pl.pallas_call(
        functools.partial(_dense_dft_body, t=t),
        in_specs=[seq_in, _const_spec(consts["cs"].shape), _const_spec(consts["m3"].shape)],
        name="fourier_dense", **common,
    )(zf, consts["cs"], consts["m3"])


FF_PIECES = (1536, 1280)


def _outffn_body(yg_ref, yf_ref, hr_ref, rg_ref, x_ref, mod_ref, gpm_ref, gpf_ref, gqf_ref,
                 wo_ref, wg_ref, wu_ref, wd_ref, o_ref):
    tm = x_ref.shape[1]
    sub = tm // ROW_SPLIT
    subs = [slice(r * sub, (r + 1) * sub) for r in range(ROW_SPLIT)]
    mixes = []
    for rs in subs:
        yr = (hr_ref[0, rs, :] * jax.nn.gelu(rg_ref[0, rs, :])).astype(BF16)
        y = jnp.concatenate([yg_ref[0, rs, :], yf_ref[0, rs, :], yr], axis=1)
        mixes.append(jnp.dot(y, wo_ref[...], preferred_element_type=F32))
    x1s, hfs = [], []
    for rs, mix in zip(subs, mixes):
        x1 = x_ref[0, rs, :] + mod_ref[0, 2:3, :] * _rms(mix, gpm_ref[...])
        x1s.append(x1)
        hfs.append((_rms(x1, gpf_ref[...]) * (1.0 + mod_ref[0, 4:5, :]) + mod_ref[0, 3:4, :]).astype(BF16))
    for rs, x1, hf in zip(subs, x1s, hfs):
        acc = None
        off = 0
        for fw in FF_PIECES:
            gate = jnp.dot(hf, wg_ref[:, off:off + fw], preferred_element_type=F32)
            up = jnp.dot(hf, wu_ref[:, off:off + fw], preferred_element_type=F32)
            act = (gate * jax.nn.sigmoid(gate) * up).astype(BF16)
            part = jnp.dot(act, wd_ref[off:off + fw, :], preferred_element_type=F32)
            acc = part if acc is None else acc + part
            off += fw
        o_ref[0, rs, :] = x1 + mod_ref[0, 5:6, :] * _rms(acc, gqf_ref[...])


def _outffn(yg, yf, hr, rg, x, mod, mod_row, l, gpm, gpf, gqf, wo, wg, wu, wd, tm):
    bsz, t, d = x.shape
    if mod_row is None:
        mod_map = lambda b, i: (l, b, 0, 0)
    else:
        mod_map = lambda b, i: (l, mod_row, 0, 0)
    tile = lambda wdt: pl.BlockSpec((1, tm, wdt), lambda b, i: (b, i, 0))
    return pl.pallas_call(
        _outffn_body,
        out_shape=jax.ShapeDtypeStruct((bsz, t, d), F32),
        grid=(bsz, t // tm),
        in_specs=[
            tile(GLA_V), tile(FNET_W), tile(RG_W), tile(RG_W), tile(d),
            pl.BlockSpec((None, 1, 6, d), mod_map),
            _layer_spec(gpm, l), _layer_spec(gpf, l), _layer_spec(gqf, l),
            _layer_spec(wo, l), _layer_spec(wg, l), _layer_spec(wu, l), _layer_spec(wd, l),
        ],
        out_specs=tile(d),
        compiler_params=pltpu.CompilerParams(vmem_limit_bytes=VMEM_LIMIT),
        name="outffn",
    )(yg, yf, hr, rg, x, mod, gpm, gpf, gqf, wo, wg, wu, wd)


def _prep_w_in(w_in):
    off_dec = 2 * GLA_QK + GLA_V
    off_og = off_dec + 2 * GLA_RANK
    pad = jnp.zeros(w_in.shape[:2] + (DEC_PAD - 2 * GLA_RANK,), w_in.dtype)
    return jnp.concatenate([w_in[..., 0:off_dec], w_in[..., off_og:], w_in[..., off_dec:off_og], pad], axis=-1).astype(BF16)


def _prep_dec(w_dec, b_dec):
    depth = w_dec.shape[0]
    hp = 2 * GLA_DK
    wr = w_dec.reshape(depth, 2, GLA_RANK, 2, hp)
    wbd = jnp.einsum("ldrpj,de->lpdrej", wr, jnp.eye(2, dtype=w_dec.dtype))
    wbd = wbd.reshape(depth, 2, 2 * GLA_RANK, 2 * hp)
    wbd = jnp.pad(wbd, ((0, 0), (0, 0), (0, DEC_PAD - 2 * GLA_RANK), (0, 0)))
    bd = b_dec.reshape(depth, 2, 2, hp).transpose(0, 2, 1, 3).reshape(depth, 2, 1, 2 * hp)
    return wbd.astype(BF16), bd


def _block_diag(w):
    h, hd = w.shape[-3], w.shape[-2]
    out = jnp.einsum("...hij,hg->...higj", w, jnp.eye(h, dtype=w.dtype))
    return out.reshape(w.shape[:-3] + (h * hd, h * hd))


def _prep_rg(w_a, b_a, w_x, b_x):
    bda, bdx = _block_diag(w_a), _block_diag(w_x)
    wg = jnp.concatenate([bda[:, 0], bdx[:, 0], bda[:, 1], bdx[:, 1]], axis=-1)
    bg = jnp.concatenate([b_a[:, 0], b_x[:, 0], b_a[:, 1], b_x[:, 1]], axis=-1)[:, None, :]
    return wg.astype(BF16), bg


def _mixer(h, mod, mod_row, l, lw, consts, s0, h0, line, tm, with_output):
    t = h.shape[1]
    f_pitch = FFT_PITCH if "m2" in consts else GRID_W
    rgw = (lw["w_conv"], lw["b_conv"], lw["w_gate"], lw["b_gate"], lw["lam"])
    (zq, zk, zv, zog, zf, zrg, zdec), coefs = _inproj(h, mod, mod_row, l, lw["g_pre_mix"], lw["w_in"], rgw, tm, f_pitch, line)
    yg, s_fin = _gla(zq, zk, zv, zog, zdec, l, lw["wd"], lw["bd"], lw["g_gla"], s0, with_output)
    hr, h_fin = _rg(coefs, h0, with_output)
    yf = _fourier(zf, consts, t) if with_output else None
    return (yg, yf, hr, zrg), s_fin, h_fin


def kernel(x, c, ctx, c_ctx, w_ada, b_ada, g_pre_mix, g_post_mix, g_pre_ffn, g_post_ffn, w_in, w_dec, b_dec, g_gla,
           w_conv, b_conv, w_rg_a, b_rg_a, w_rg_x, b_rg_x, rg_lam, w_out, w_ffn_gate, w_ffn_up, w_ffn_down):
    bsz, t, d = x.shape
    t_ctx = ctx.shape[1]
    depth = w_ada.shape[0]

    c_all = jnp.concatenate([c, c_ctx[None, :], jnp.zeros((MOD_ROWS - bsz - 1, d), F32)], axis=0)
    mod = _ada(c_all, w_ada, b_ada).reshape(depth, MOD_ROWS, 6, d)

    consts_lat = _fourier_consts(t)
    consts_ctx = _fourier_consts(t_ctx)
    s_zero = jnp.zeros((bsz, 2, GLA_HEADS, GLA_DK, GLA_DV), F32)
    h_zero = jnp.zeros((bsz, SUBLANES, RG_W), F32)
    rows = lambda v: v[:, None, :]

    wd_p, bd_p = _prep_dec(w_dec, b_dec)
    w_gate, b_gate = _prep_rg(w_rg_a, b_rg_a, w_rg_x, b_rg_x)
    lw = dict(g_pre_mix=rows(g_pre_mix), w_in=_prep_w_in(w_in), wd=wd_p, bd=bd_p, g_gla=rows(g_gla),
              w_conv=w_conv, b_conv=rows(b_conv), w_gate=w_gate, b_gate=b_gate, lam=rg_lam)
    ffn = (rows(g_post_mix), rows(g_pre_ffn), rows(g_post_ffn), w_out.astype(BF16),
           w_ffn_gate.astype(BF16), w_ffn_up.astype(BF16), w_ffn_down.astype(BF16))

    h_ctx = ctx
    for l in range(depth):
        last = l == depth - 1
        ys_c, s_ctx, hs_ctx = _mixer(h_ctx, mod, bsz, l, lw, consts_ctx, s_zero, h_zero, t_ctx, t_ctx, not last)
        ys_l, _, _ = _mixer(x, mod, None, l, lw, consts_lat, s_ctx, hs_ctx, GRID_W, 512, True)
        x = _outffn(*ys_l, x, mod, None, l, *ffn, 512)
        if not last:
            h_ctx = _outffn(*ys_c, h_ctx, mod, bsz, l, *ffn, t_ctx)
    return x
```

```python
import functools

import jax
import jax.numpy as jnp
import numpy as np
from jax import lax
from jax.experimental import pallas as pl
from jax.experimental.pallas import tpu as pltpu

F32 = jnp.float32
BF16 = jnp.bfloat16

D_MODEL = 1024
DEPTH = 4
GRID_W = 64
EPS = 1e-6

GLA_HEADS = 4
GLA_DK = 64
GLA_DV = 128
GLA_QK = GLA_HEADS * GLA_DK
GLA_V = GLA_HEADS * GLA_DV
GLA_RANK = 16
GLA_TAU = 16.0
GLA_CHUNK = 64
FNET_W = 256
FNET_GD = 64
RG_HEADS = 4
RG_HD = 64
RG_W = 256
RG_C = 8.0
CONV_W = 4
D_FF = 2816

LANES = 128
SUBLANES = 8
DEC_PAD = LANES
N_IN_PAD = 2 * GLA_QK + 2 * GLA_V + FNET_W + 2 * RG_W + DEC_PAD
MOD_ROWS = 16
VMEM_LIMIT = 56 * 1024 * 1024

_NT = (((1,), (1,)), ((), ()))
_TN = (((0,), (0,)), ((), ()))


def _rms(x, g):
    return x * lax.rsqrt(jnp.mean(x * x, axis=-1, keepdims=True) + EPS) * g


def _const_spec(shape):
    nd = len(shape)
    return pl.BlockSpec(shape, lambda *_: (0,) * nd, pipeline_mode=pl.Buffered(1))


def _layer_spec(arr, l):
    nd = arr.ndim - 1
    return pl.BlockSpec((None,) + arr.shape[1:], lambda *_: (l,) + (0,) * nd, pipeline_mode=pl.Buffered(1))


def _ada_body(c_ref, w_ref, b_ref, o_ref):
    c = c_ref[...]
    s = (c * jax.nn.sigmoid(c)).astype(BF16)
    o_ref[0] = jnp.dot(s, w_ref[0].astype(BF16), preferred_element_type=F32) + b_ref[0]


def _ada(c_all, w_ada, b_ada):
    depth, d, n = w_ada.shape
    tn = 1536
    return pl.pallas_call(
        _ada_body,
        out_shape=jax.ShapeDtypeStruct((depth, MOD_ROWS, n), F32),
        grid=(depth, n // tn),
        in_specs=[
            pl.BlockSpec((MOD_ROWS, d), lambda l, j: (0, 0)),
            pl.BlockSpec((1, d, tn), lambda l, j: (l, 0, j)),
            pl.BlockSpec((1, 1, tn), lambda l, j: (l, 0, j)),
        ],
        out_specs=pl.BlockSpec((1, MOD_ROWS, tn), lambda l, j: (l, 0, j)),
        compiler_params=pltpu.CompilerParams(vmem_limit_bytes=VMEM_LIMIT),
        name="ada",
    )(c_all, w_ada, b_ada.reshape(depth, 1, n))


ROW_SPLIT = 2

_IN_WIDTHS = (GLA_QK, GLA_QK, GLA_V, GLA_V, FNET_W, RG_W, RG_W, DEC_PAD)
_IN_F = 4


def _inproj_body(x_ref, mod_ref, g_ref, w_ref, *out_refs, f_pitch):
    tm = x_ref.shape[1]
    sub = tm // ROW_SPLIT
    for r in range(ROW_SPLIT):
        rs = slice(r * sub, (r + 1) * sub)
        h = _rms(x_ref[0, rs, :], g_ref[...]) * (1.0 + mod_ref[0, 1:2, :]) + mod_ref[0, 0:1, :]
        h = h.astype(BF16)
        off = 0
        for idx, (o_ref, wd) in enumerate(zip(out_refs, _IN_WIDTHS)):
            z = jnp.dot(h, w_ref[:, off:off + wd], preferred_element_type=F32)
            off += wd
            if idx == _IN_F and f_pitch != GRID_W:
                pad = jnp.zeros((f_pitch - GRID_W, wd), F32)
                for ln in range(sub // GRID_W):
                    base = (r * (sub // GRID_W) + ln) * f_pitch
                    o_ref[0, base:base + GRID_W, :] = z[ln * GRID_W:(ln + 1) * GRID_W, :]
                    o_ref[0, base + GRID_W:base + f_pitch, :] = pad
            else:
                o_ref[0, rs, :] = z


def _inproj(x, mod, mod_row, l, g, w_in_p, tm, f_pitch):
    bsz, t, d = x.shape
    if mod_row is None:
        mod_map = lambda b, i: (l, b, 0, 0)
    else:
        mod_map = lambda b, i: (l, mod_row, 0, 0)
    rows_of = lambda idx, n: n // GRID_W * f_pitch if idx == _IN_F else n
    return pl.pallas_call(
        functools.partial(_inproj_body, f_pitch=f_pitch),
        out_shape=[jax.ShapeDtypeStruct((bsz, rows_of(i, t), wd), F32) for i, wd in enumerate(_IN_WIDTHS)],
        grid=(bsz, t // tm),
        in_specs=[
            pl.BlockSpec((1, tm, d), lambda b, i: (b, i, 0)),
            pl.BlockSpec((None, 1, 6, d), mod_map),
            _layer_spec(g, l),
            _layer_spec(w_in_p, l),
        ],
        out_specs=[pl.BlockSpec((1, rows_of(i, tm), wd), lambda b, i: (b, i, 0)) for i, wd in enumerate(_IN_WIDTHS)],
        compiler_params=pltpu.CompilerParams(vmem_limit_bytes=VMEM_LIMIT),
        name="inproj",
    )(x, mod, g, w_in_p)


GLA_BLOCK = 256
GLA_RINGS = 8


def _chunk_cumsum(x, reverse):
    rows, lanes = x.shape
    g = SUBLANES
    c = GLA_CHUNK
    x3 = x.reshape(rows // g, g, lanes)
    sub = lax.broadcasted_iota(jnp.int32, x3.shape, 1)
    for s in (1, 2, 4):
        if reverse:
            x3 = x3 + jnp.where(sub < g - s, pltpu.roll(x3, g - s, 1), 0.0)
        else:
            x3 = x3 + jnp.where(sub >= s, pltpu.roll(x3, s, 1), 0.0)
    edge = x3[:, 0:1, :] if reverse else x3[:, g - 1:g, :]
    tot = jnp.broadcast_to(edge, x3.shape).reshape(rows, lanes)
    pos = lax.broadcasted_iota(jnp.int32, (rows, lanes), 0) & (c - 1)
    acc = tot
    for s in (8, 16, 32):
        if reverse:
            acc = acc + jnp.where(pos < c - s, pltpu.roll(acc, rows - s, 0), 0.0)
        else:
            acc = acc + jnp.where(pos >= s, pltpu.roll(acc, s, 0), 0.0)
    sums = x3.reshape(rows, lanes) + (acc - tot)
    parts = []
    for ci in range(rows // c):
        r = ci * c if reverse else ci * c + c - 1
        parts.append(jnp.broadcast_to(acc[r:r + 1, :], (c, lanes)))
    total = parts[0] if len(parts) == 1 else jnp.concatenate(parts, axis=0)
    return sums, total


def _gla_body(q_ref, k_ref, v_ref, og_ref, dec_ref, wd_ref, bd_ref, g_ref, s0_ref, *rest, t, with_output):
    nr = GLA_RINGS
    if with_output:
        y_ref, sfin_ref, u_ref, dl_ref, qt_ref, o_ref = rest[:6]
        bufs = rest[6:]
        rings = tuple((bufs[r], bufs[nr + r], bufs[2 * nr + r], bufs[3 * nr + r]) for r in range(nr))
    else:
        sfin_ref, u_ref, dl_ref = rest[:3]
        bufs = rest[3:]
        rings = tuple((bufs[r], bufs[nr + r], None, None) for r in range(nr))
    c = GLA_CHUNK
    hp = 2 * GLA_DK
    vp = 2 * GLA_DV
    rows = min(t, GLA_BLOCK)
    cpb = rows // c
    n_blk = t // rows
    scale = GLA_DK ** -0.5

    dk_head0 = lax.broadcasted_iota(jnp.int32, (hp, 2 * c), 0) < GLA_DK
    ar = lax.broadcasted_iota(jnp.int32, (2 * c, 4 * c), 0)
    ac = lax.broadcasted_iota(jnp.int32, (2 * c, 4 * c), 1)
    same_chunk = (ar // c) == ((ac // c) & 1)
    keep_f = same_chunk & ((ar & (c - 1)) >= (ac & (c - 1)))
    keep_b = same_chunk & ((ar & (c - 1)) <= (ac & (c - 1)))
    wd = wd_ref[0]
    bd = bd_ref[0]

    def prepare(blk, ring):
        vb_ring, ke_ring, qt_ring, kt_ring = ring
        r0 = pl.multiple_of(blk * rows, rows)
        code = dec_ref[0, pl.ds(r0, rows), :].astype(BF16)
        logit = jnp.dot(code, wd, preferred_element_type=F32) + bd
        la = (jnp.minimum(logit, 0.0) - jnp.log(1.0 + jnp.exp(-jnp.abs(logit)))) * (1.0 / GLA_TAU)
        k = k_ref[0, pl.ds(r0, rows), :]
        vb_ring[...] = v_ref[0, pl.ds(r0, rows), :].astype(BF16)
        if with_output:
            q = q_ref[0, pl.ds(r0, rows), :] * scale
        for dirn in range(2):
            b, b_all = _chunk_cumsum(la[:, dirn * hp:(dirn + 1) * hp], reverse=dirn == 1)
            ke_ring[dirn] = (k * jnp.exp(b_all - b)).astype(BF16)
            for ci in range(cpb):
                dl_ref[dirn, blk * cpb + ci] = jnp.exp(b_all[ci * c:ci * c + SUBLANES, :])
            if with_output:
                qt = (q * jnp.exp(b)).astype(BF16)
                qt_ref[dirn, pl.ds(r0, rows), :] = qt
                qt_ring[dirn] = qt
                kt_ring[dirn] = (k * jnp.exp(-b)).T.astype(BF16)

    def multiply(blk, ring):
        vb_ring, ke_ring, qt_ring, kt_ring = ring
        r0 = pl.multiple_of(blk * rows, rows)
        vb = vb_ring[...]
        for dirn in range(2):
            ke = ke_ring[dirn]
            for ci in range(cpb):
                sl = slice(ci * c, (ci + 1) * c)
                upd = lax.dot_general(ke[sl], vb[sl], _TN, preferred_element_type=F32)
                u_ref[dirn, blk * cpb + ci, 0] = upd[0:GLA_DK, 0:GLA_DV]
                u_ref[dirn, blk * cpb + ci, 1] = upd[GLA_DK:hp, GLA_DV:vp]
            if not with_output:
                continue
            qt = qt_ring[dirn]
            kt_t = kt_ring[dirn]
            zt = jnp.zeros((hp, 2 * c), BF16)
            zv = jnp.zeros((2 * c, GLA_DV), BF16)
            parts = []
            for cp in range(cpb // 2):
                sl = slice(cp * 2 * c, (cp + 1) * 2 * c)
                kt_cp = kt_t[:, sl]
                kbd = jnp.concatenate([jnp.where(dk_head0, kt_cp, zt), jnp.where(dk_head0, zt, kt_cp)], axis=1)
                sc = jnp.dot(qt[sl], kbd, preferred_element_type=F32)
                p = jnp.where(keep_f if dirn == 0 else keep_b, sc, 0.0).astype(BF16)
                v_cp = vb[sl]
                vbd = jnp.concatenate(
                    [jnp.concatenate([v_cp[:, 0:GLA_DV], zv], axis=1),
                     jnp.concatenate([zv, v_cp[:, GLA_DV:vp]], axis=1)], axis=0)
                parts.append(jnp.dot(p, vbd, preferred_element_type=F32))
            o_intra = parts[0] if len(parts) == 1 else jnp.concatenate(parts, axis=0)
            if dirn == 0:
                o_ref[pl.ds(r0, rows), :] = o_intra
            else:
                o_ref[pl.ds(r0, rows), :] += o_intra

    per = min(n_blk, nr)

    def pre_step(j, carry):
        i = per * j
        prepare(i, rings[0])
        for r in range(per):
            if r + 1 < per:
                prepare(i + r + 1, rings[r + 1])
            multiply(i + r, rings[r])
        return carry

    lax.fori_loop(0, n_blk // per, pre_step, 0)

    zs = jnp.zeros((GLA_DK, GLA_DV), BF16)

    def gate_block(r0):
        rs = pl.ds(r0, rows)
        o = o_ref[rs, :]
        og = og_ref[0, rs, :]
        g = g_ref[...]
        o0 = _rms(o[:, 0:GLA_DV], g)
        o1 = _rms(o[:, GLA_DV:vp], g)
        gate = og * jax.nn.sigmoid(og)
        y_ref[0, rs, :] = (jnp.concatenate([o0, o1], axis=1) * gate).astype(y_ref.dtype)

    for dirn in range(2):
        def seq_step(i, carry, dirn=dirn):
            s0, s1 = carry
            blk = i if dirn == 0 else n_blk - 1 - i
            r0 = pl.multiple_of(blk * rows, rows)
            for cj in range(cpb):
                ci = cj if dirn == 0 else cpb - 1 - cj
                chunk = blk * cpb + ci
                if with_output:
                    rs = pl.ds(r0 + ci * c, c)
                    sbd = jnp.concatenate(
                        [jnp.concatenate([s0.astype(BF16), zs], axis=1),
                         jnp.concatenate([zs, s1.astype(BF16)], axis=1)], axis=0)
                    o_ref[rs, :] += jnp.dot(qt_ref[dirn, rs, :], sbd, preferred_element_type=F32)
                drow = dl_ref[dirn, chunk][0:1, :]
                dcol = jnp.broadcast_to(drow, (hp, hp)).T
                s0 = s0 * dcol[0:GLA_DK, :] + u_ref[dirn, chunk, 0]
                s1 = s1 * dcol[GLA_DK:hp, :] + u_ref[dirn, chunk, 1]
            if with_output and dirn == 1:
                gate_block(r0)
            return s0, s1

        s0, s1 = lax.fori_loop(0, n_blk, seq_step, (s0_ref[0, dirn, 0], s0_ref[0, dirn, 1]), unroll=min(8 if dirn == 0 else 4, n_blk))
        sfin_ref[0, dirn, 0] = s0
        sfin_ref[0, dirn, 1] = s1


def _gla(zq, zk, zv, zog, zdec, l, wd_p, bd_p, g_gla, s0, with_output):
    bsz, t, _ = zq.shape
    hp, vp = 2 * GLA_DK, 2 * GLA_DV
    n_chunks = t // GLA_CHUNK
    seq = lambda wd: pl.BlockSpec((1, t, wd), lambda b, p: (b, 0, p))
    st_spec = pl.BlockSpec((1, 2, 2, GLA_DK, GLA_DV), lambda b, p: (b, 0, p, 0, 0))
    out_shape = [jax.ShapeDtypeStruct((bsz, 2, GLA_HEADS, GLA_DK, GLA_DV), F32)]
    out_specs = [st_spec]
    rows = min(t, GLA_BLOCK)
    nr = GLA_RINGS
    scratch = [pltpu.VMEM((2, n_chunks, 2, GLA_DK, GLA_DV), F32), pltpu.VMEM((2, n_chunks, SUBLANES, hp), F32)]
    staging = [pltpu.VMEM((rows, vp), BF16) for _ in range(nr)] + [pltpu.VMEM((2, rows, hp), BF16) for _ in range(nr)]
    if with_output:
        out_shape = [jax.ShapeDtypeStruct((bsz, t, GLA_V), BF16)] + out_shape
        out_specs = [seq(vp)] + out_specs
        scratch = scratch + [pltpu.VMEM((2, t, hp), BF16), pltpu.VMEM((t, vp), F32)]
        staging = staging + [pltpu.VMEM((2, rows, hp), BF16) for _ in range(nr)]
        staging = staging + [pltpu.VMEM((2, hp, rows), BF16) for _ in range(nr)]
    scratch = scratch + staging
    outs = pl.pallas_call(
        functools.partial(_gla_body, t=t, with_output=with_output),
        out_shape=out_shape,
        grid=(bsz, 2),
        in_specs=[
            seq(hp), seq(hp), seq(vp), seq(vp),
            pl.BlockSpec((1, t, DEC_PAD), lambda b, p: (b, 0, 0)),
            pl.BlockSpec((None, 1, DEC_PAD, 2 * hp), lambda b, p: (l, p, 0, 0)),
            pl.BlockSpec((None, 1, 1, 2 * hp), lambda b, p: (l, p, 0, 0)),
            _layer_spec(g_gla, l),
            st_spec,
        ],
        out_specs=out_specs,
        scratch_shapes=scratch,
        compiler_params=pltpu.CompilerParams(vmem_limit_bytes=VMEM_LIMIT),
        name="gla",
    )(zq, zk, zv, zog, zdec, wd_p, bd_p, g_gla, s0)
    if with_output:
        return outs[0], outs[1]
    return None, outs[0]


RG_BLOCK = 256


def _sigmoid(x):
    return 0.5 * jnp.tanh(0.5 * x) + 0.5


def _rg_body(rx_ref, rg_ref, wc_ref, bc_ref, wg_ref, bg_ref, lam_ref, h0_ref, *rest, t, line, with_output):
    if with_output:
        y_ref, hfin_ref, af_ref, bf_ref, ab_ref, bb_ref = rest
    else:
        hfin_ref, af_ref, bf_ref, ab_ref, bb_ref = rest
    w = RG_W
    rows = min(t, RG_BLOCK)
    grp = SUBLANES
    pos = lax.broadcasted_iota(jnp.int32, (rows, w), 0) & (line - 1)
    sub = lax.broadcasted_iota(jnp.int32, (rows // grp, grp, w), 1)
    wc = wc_ref[...]
    bc = bc_ref[...]
    wg = wg_ref[...]
    bg = bg_ref[...]
    sp = [jax.nn.softplus(-lam_ref[d:d + 1, :]) * (-RG_C) for d in range(2)]

    def pre_step(blk, carry):
        rs = pl.ds(pl.multiple_of(blk * rows, rows), rows)
        x = rx_ref[0, rs, :]
        xm1 = jnp.where(pos >= 1, pltpu.roll(x, 1, 0), 0.0)
        xp1 = jnp.where(pos <= line - 2, pltpu.roll(x, rows - 1, 0), 0.0)
        xp2 = jnp.where(pos <= line - 3, pltpu.roll(x, rows - 2, 0), 0.0)
        u = bc + xm1 * wc[0:1, :] + x * wc[1:2, :] + xp1 * wc[2:3, :] + xp2 * wc[3:4, :]
        gates = jnp.dot(u.astype(BF16), wg, preferred_element_type=F32) + bg
        for dirn, (a_ref, b_ref) in enumerate(((af_ref, bf_ref), (ab_ref, bb_ref))):
            r = _sigmoid(gates[:, (2 * dirn) * w:(2 * dirn + 1) * w])
            i = _sigmoid(gates[:, (2 * dirn + 1) * w:(2 * dirn + 2) * w])
            log_a = r * sp[dirn]
            a = jnp.exp(log_a)
            bx = jnp.sqrt(-jnp.tanh(log_a) * (a * a + 1.0)) * (i * u)
            a3 = a.reshape(rows // grp, grp, w)
            b3 = bx.reshape(rows // grp, grp, w)
            for s in (1, 2, 4):
                if dirn == 0:
                    ok = sub >= s
                    sh = s
                else:
                    ok = sub <= grp - 1 - s
                    sh = grp - s
                b3 = jnp.where(ok, a3 * pltpu.roll(b3, sh, 1) + b3, b3)
                a3 = jnp.where(ok, a3 * pltpu.roll(a3, sh, 1), a3)
            a_ref[rs, :] = a3.reshape(rows, w)
            b_ref[rs, :] = b3.reshape(rows, w)
        return carry

    lax.fori_loop(0, t // rows, pre_step, 0)

    n_grp = t // grp

    def grp_step(i, carry):
        hf, hb = carry
        r0 = pl.multiple_of(i * grp, grp)
        hfull = bf_ref[pl.ds(r0, grp), :] + af_ref[pl.ds(r0, grp), :] * hf
        bf_ref[pl.ds(r0, grp), :] = hfull
        r1 = pl.multiple_of((n_grp - 1 - i) * grp, grp)
        hbull = bb_ref[pl.ds(r1, grp), :] + ab_ref[pl.ds(r1, grp), :] * hb
        bb_ref[pl.ds(r1, grp), :] = hbull
        return hfull[grp - 1:grp, :], hbull[0:1, :]

    hf, hb = lax.fori_loop(0, n_grp, grp_step, (h0_ref[0, 0:1, :], h0_ref[0, 1:2, :]), unroll=4)
    hfin_ref[0, 0:1, :] = hf
    hfin_ref[0, 1:2, :] = hb
    hfin_ref[0, 2:grp, :] = jnp.zeros((grp - 2, w), F32)

    if with_output:
        def out_step(blk, carry):
            rs = pl.ds(pl.multiple_of(blk * rows, rows), rows)
            h = bf_ref[rs, :] + bb_ref[rs, :]
            y_ref[0, rs, :] = (h * jax.nn.gelu(rg_ref[0, rs, :])).astype(y_ref.dtype)
            return carry

        lax.fori_loop(0, t // rows, out_step, 0)


def _rg(zrx, zrg, l, w_conv, b_conv, w_gate, b_gate, lam, h0, line, with_output):
    bsz, t, w = zrx.shape
    seq = pl.BlockSpec((1, t, w), lambda b: (b, 0, 0))
    st_spec = pl.BlockSpec((1, SUBLANES, w), lambda b: (b, 0, 0))
    out_shape = [jax.ShapeDtypeStruct((bsz, SUBLANES, w), F32)]
    out_specs = [st_spec]
    if with_output:
        out_shape = [jax.ShapeDtypeStruct((bsz, t, w), BF16)] + out_shape
        out_specs = [seq] + out_specs
    outs = pl.pallas_call(
        functools.partial(_rg_body, t=t, line=line, with_output=with_output),
        out_shape=out_shape,
        grid=(bsz,),
        in_specs=[
            seq, seq,
            _layer_spec(w_conv, l), _layer_spec(b_conv, l),
            _layer_spec(w_gate, l), _layer_spec(b_gate, l),
            _layer_spec(lam, l),
            st_spec,
        ],
        out_specs=out_specs,
        scratch_shapes=[pltpu.VMEM((t, w), F32) for _ in range(4)],
        compiler_params=pltpu.CompilerParams(vmem_limit_bytes=VMEM_LIMIT),
        name="rglru",
    )(zrx, zrg, w_conv, b_conv, w_gate, b_gate, lam, h0)
    if with_output:
        return outs[0], outs[1]
    return None, outs[0]


FFT_PITCH = 72


def _dft_mats(n):
    j = np.arange(n)
    ang = 2.0 * np.pi * ((j[:, None] * j[None, :]) % n) / n
    return np.cos(ang), np.sin(ang)


def _fourier_consts(t):
    n2 = GRID_W
    n1 = t // n2
    cc, sc = _dft_mats(FNET_GD)
    eye2 = np.eye(LANES // FNET_GD)
    m3 = np.concatenate([np.kron(eye2, cc), -np.kron(eye2, sc)], axis=0) / 8.0
    c2, s2 = _dft_mats(n2)
    consts = {"m3": jnp.asarray(m3, F32)}
    if n1 == n2:
        consts["cs"] = jnp.asarray(np.concatenate([c2, s2], axis=0) / 8.0, F32)
        consts["m2"] = jnp.asarray(np.block([[c2, -s2], [s2, c2]]) / 8.0, F32)
        wk = (np.arange(n2)[:, None] * np.arange(n1)[None, :]).reshape(-1)
        ang = 2.0 * np.pi * wk / t
        consts["twc"] = jnp.asarray(np.repeat(np.cos(ang)[:, None], LANES, axis=1), F32)
        consts["tws"] = jnp.asarray(np.repeat(np.sin(ang)[:, None], LANES, axis=1), F32)
    else:
        ct, st = _dft_mats(t)
        consts["cs"] = jnp.asarray(np.concatenate([ct, st], axis=0) / np.sqrt(t), F32)
    return consts


def _fft_body(x_ref, cs_ref, m2_ref, m3_ref, twc_ref, tws_ref, y_ref, sr_ref, si_ref, dr_ref, di_ref, *, t):
    n = GRID_W
    p = FFT_PITCH
    cs = cs_ref[...].astype(BF16)
    m2 = m2_ref[...].astype(BF16)
    m3 = m3_ref[...].astype(BF16)

    def stage1(i, carry):
        w0 = 2 * i
        xw = jnp.concatenate([x_ref[0, pl.ds(w0 + j, n, stride=p), :] for j in range(2)], axis=1)
        a = jnp.dot(cs, xw.astype(BF16), preferred_element_type=F32)
        for j in range(2):
            ar = a[0:n, j * LANES:(j + 1) * LANES]
            ai = a[n:2 * n, j * LANES:(j + 1) * LANES]
            r0 = pl.multiple_of((w0 + j) * n, n)
            tc = twc_ref[pl.ds(r0, n), :]
            ts = tws_ref[pl.ds(r0, n), :]
            sr_ref[pl.ds(w0 + j, n, stride=p), :] = ar * tc - ai * ts
            si_ref[pl.ds(w0 + j, n, stride=p), :] = ar * ts + ai * tc
        return carry

    lax.fori_loop(0, n // 2, stage1, 0, unroll=8)

    def stage2(i, carry):
        k0 = 2 * i
        cols = []
        for j in range(2):
            r0 = pl.multiple_of((k0 + j) * p, SUBLANES)
            cols.append(jnp.concatenate([sr_ref[pl.ds(r0, n), :], si_ref[pl.ds(r0, n), :]], axis=0))
        xb = jnp.concatenate(cols, axis=1).astype(BF16)
        d = jnp.dot(m2, xb, preferred_element_type=F32)
        for j in range(2):
            dr_ref[pl.ds(k0 + j, n, stride=p), :] = d[0:n, j * LANES:(j + 1) * LANES]
            di_ref[pl.ds(k0 + j, n, stride=p), :] = d[n:2 * n, j * LANES:(j + 1) * LANES]
        return carry

    lax.fori_loop(0, n // 2, stage2, 0, unroll=8)

    per = 8

    def stage3(i, carry):
        blocks = []
        for j in range(per):
            r0 = pl.multiple_of((i * per + j) * p, SUBLANES)
            blocks.append(jnp.concatenate([dr_ref[pl.ds(r0, n), :], di_ref[pl.ds(r0, n), :]], axis=1))
        d = jnp.concatenate(blocks, axis=0).astype(BF16)
        rs = pl.ds(pl.multiple_of(i * per * n, per * n), per * n)
        y_ref[0, rs, :] = jnp.dot(d, m3, preferred_element_type=F32).astype(y_ref.dtype)
        return carry

    lax.fori_loop(0, n // per, stage3, 0, unroll=2)


def _dense_dft_body(x_ref, cs_ref, m3_ref, y_ref, *, t):
    a = jnp.dot(cs_ref[...].astype(BF16), x_ref[0].astype(BF16), preferred_element_type=F32)
    d = jnp.concatenate([a[0:t], a[t:2 * t]], axis=1).astype(BF16)
    y_ref[0] = jnp.dot(d, m3_ref[...].astype(BF16), preferred_element_type=F32).astype(y_ref.dtype)


def _fourier(zf, consts, t):
    bsz, t_in, w = zf.shape
    seq = pl.BlockSpec((1, t, LANES), lambda b, j: (b, 0, j))
    seq_in = pl.BlockSpec((1, t_in, LANES), lambda b, j: (b, 0, j))
    common = dict(
        out_shape=jax.ShapeDtypeStruct((bsz, t, w), BF16),
        grid=(bsz, w // LANES),
        out_specs=seq,
        compiler_params=pltpu.CompilerParams(vmem_limit_bytes=VMEM_LIMIT),
    )
    if "m2" in consts:
        return pl.pallas_call(
            functools.partial(_fft_body, t=t),
            in_specs=[seq_in, _const_spec(consts["cs"].shape), _const_spec(consts["m2"].shape),
                      _const_spec(consts["m3"].shape), _const_spec((t, LANES)), _const_spec((t, LANES))],
            scratch_shapes=[pltpu.VMEM((GRID_W * FFT_PITCH, LANES), F32) for _ in range(4)],
            name="fourier_fft", **common,
        )(zf, consts["cs"], consts["m2"], consts["m3"], consts["twc"], consts["tws"])
    return pl.pallas_call(
        functools.partial(_dense_dft_body, t=t),
        in_specs=[seq_in, _const_spec(consts["cs"].shape), _const_spec(consts["m3"].shape)],
        name="fourier_dense", **common,
    )(zf, consts["cs"], consts["m3"])


FF_PIECES = (1536, 1280)


def _outffn_body(yg_ref, yf_ref, yr_ref, x_ref, mod_ref, gpm_ref, gpf_ref, gqf_ref,
                 wo_ref, wg_ref, wu_ref, wd_ref, o_ref):
    tm = x_ref.shape[1]
    sub = tm // ROW_SPLIT
    subs = [slice(r * sub, (r + 1) * sub) for r in range(ROW_SPLIT)]
    mixes = []
    for rs in subs:
        y = jnp.concatenate([yg_ref[0, rs, :], yf_ref[0, rs, :], yr_ref[0, rs, :]], axis=1)
        mixes.append(jnp.dot(y, wo_ref[...], preferred_element_type=F32))
    x1s, hfs = [], []
    for rs, mix in zip(subs, mixes):
        x1 = x_ref[0, rs, :] + mod_ref[0, 2:3, :] * _rms(mix, gpm_ref[...])
        x1s.append(x1)
        hfs.append((_rms(x1, gpf_ref[...]) * (1.0 + mod_ref[0, 4:5, :]) + mod_ref[0, 3:4, :]).astype(BF16))
    for rs, x1, hf in zip(subs, x1s, hfs):
        acc = None
        off = 0
        for fw in FF_PIECES:
            gate = jnp.dot(hf, wg_ref[:, off:off + fw], preferred_element_type=F32)
            up = jnp.dot(hf, wu_ref[:, off:off + fw], preferred_element_type=F32)
            act = (gate * jax.nn.sigmoid(gate) * up).astype(BF16)
            part = jnp.dot(act, wd_ref[off:off + fw, :], preferred_element_type=F32)
            acc = part if acc is None else acc + part
            off += fw
        o_ref[0, rs, :] = x1 + mod_ref[0, 5:6, :] * _rms(acc, gqf_ref[...])


def _outffn(yg, yf, yr, x, mod, mod_row, l, gpm, gpf, gqf, wo, wg, wu, wd, tm):
    bsz, t, d = x.shape
    if mod_row is None:
        mod_map = lambda b, i: (l, b, 0, 0)
    else:
        mod_map = lambda b, i: (l, mod_row, 0, 0)
    tile = lambda wdt: pl.BlockSpec((1, tm, wdt), lambda b, i: (b, i, 0))
    return pl.pallas_call(
        _outffn_body,
        out_shape=jax.ShapeDtypeStruct((bsz, t, d), F32),
        grid=(bsz, t // tm),
        in_specs=[
            tile(GLA_V), tile(FNET_W), tile(RG_W), tile(d),
            pl.BlockSpec((None, 1, 6, d), mod_map),
            _layer_spec(gpm, l), _layer_spec(gpf, l), _layer_spec(gqf, l),
            _layer_spec(wo, l), _layer_spec(wg, l), _layer_spec(wu, l), _layer_spec(wd, l),
        ],
        out_specs=tile(d),
        compiler_params=pltpu.CompilerParams(vmem_limit_bytes=VMEM_LIMIT),
        name="outffn",
    )(yg, yf, yr, x, mod, gpm, gpf, gqf, wo, wg, wu, wd)


def _prep_w_in(w_in):
    off_dec = 2 * GLA_QK + GLA_V
    off_og = off_dec + 2 * GLA_RANK
    pad = jnp.zeros(w_in.shape[:2] + (DEC_PAD - 2 * GLA_RANK,), w_in.dtype)
    return jnp.concatenate([w_in[..., 0:off_dec], w_in[..., off_og:], w_in[..., off_dec:off_og], pad], axis=-1).astype(BF16)


def _prep_dec(w_dec, b_dec):
    depth = w_dec.shape[0]
    hp = 2 * GLA_DK
    wr = w_dec.reshape(depth, 2, GLA_RANK, 2, hp)
    wbd = jnp.einsum("ldrpj,de->lpdrej", wr, jnp.eye(2, dtype=w_dec.dtype))
    wbd = wbd.reshape(depth, 2, 2 * GLA_RANK, 2 * hp)
    wbd = jnp.pad(wbd, ((0, 0), (0, 0), (0, DEC_PAD - 2 * GLA_RANK), (0, 0)))
    bd = b_dec.reshape(depth, 2, 2, hp).transpose(0, 2, 1, 3).reshape(depth, 2, 1, 2 * hp)
    return wbd.astype(BF16), bd


def _block_diag(w):
    h, hd = w.shape[-3], w.shape[-2]
    out = jnp.einsum("...hij,hg->...higj", w, jnp.eye(h, dtype=w.dtype))
    return out.reshape(w.shape[:-3] + (h * hd, h * hd))


def _prep_rg(w_a, b_a, w_x, b_x):
    bda, bdx = _block_diag(w_a), _block_diag(w_x)
    wg = jnp.concatenate([bda[:, 0], bdx[:, 0], bda[:, 1], bdx[:, 1]], axis=-1)
    bg = jnp.concatenate([b_a[:, 0], b_x[:, 0], b_a[:, 1], b_x[:, 1]], axis=-1)[:, None, :]
    return wg.astype(BF16), bg


def _mixer(h, mod, mod_row, l, lw, consts, s0, h0, line, tm, with_output):
    t = h.shape[1]
    f_pitch = FFT_PITCH if "m2" in consts else GRID_W
    zq, zk, zv, zog, zf, zrx, zrg, zdec = _inproj(h, mod, mod_row, l, lw["g_pre_mix"], lw["w_in"], tm, f_pitch)
    yg, s_fin = _gla(zq, zk, zv, zog, zdec, l, lw["wd"], lw["bd"], lw["g_gla"], s0, with_output)
    yr, h_fin = _rg(zrx, zrg, l, lw["w_conv"], lw["b_conv"], lw["w_gate"], lw["b_gate"], lw["lam"], h0, line, with_output)
    yf = _fourier(zf, consts, t) if with_output else None
    return (yg, yf, yr), s_fin, h_fin


def kernel(x, c, ctx, c_ctx, w_ada, b_ada, g_pre_mix, g_post_mix, g_pre_ffn, g_post_ffn, w_in, w_dec, b_dec, g_gla,
           w_conv, b_conv, w_rg_a, b_rg_a, w_rg_x, b_rg_x, rg_lam, w_out, w_ffn_gate, w_ffn_up, w_ffn_down):
    bsz, t, d = x.shape
    t_ctx = ctx.shape[1]
    depth = w_ada.shape[0]

    c_all = jnp.concatenate([c, c_ctx[None, :], jnp.zeros((MOD_ROWS - bsz - 1, d), F32)], axis=0)
    mod = _ada(c_all, w_ada, b_ada).reshape(depth, MOD_ROWS, 6, d)

    consts_lat = _fourier_consts(t)
    consts_ctx = _fourier_consts(t_ctx)
    s_zero = jnp.zeros((bsz, 2, GLA_HEADS, GLA_DK, GLA_DV), F32)
    h_zero = jnp.zeros((bsz, SUBLANES, RG_W), F32)
    rows = lambda v: v[:, None, :]

    wd_p, bd_p = _prep_dec(w_dec, b_dec)
    w_gate, b_gate = _prep_rg(w_rg_a, b_rg_a, w_rg_x, b_rg_x)
    lw = dict(g_pre_mix=rows(g_pre_mix), w_in=_prep_w_in(w_in), wd=wd_p, bd=bd_p, g_gla=rows(g_gla),
              w_conv=w_conv, b_conv=rows(b_conv), w_gate=w_gate, b_gate=b_gate, lam=rg_lam)
    ffn = (rows(g_post_mix), rows(g_pre_ffn), rows(g_post_ffn), w_out.astype(BF16),
           w_ffn_gate.astype(BF16), w_ffn_up.astype(BF16), w_ffn_down.astype(BF16))

    h_ctx = ctx
    for l in range(depth):
        last = l == depth - 1
        ys_c, s_ctx, hs_ctx = _mixer(h_ctx, mod, bsz, l, lw, consts_ctx, s_zero, h_zero, t_ctx, t_ctx, not last)
        ys_l, _, _ = _mixer(x, mod, None, l, lw, consts_lat, s_ctx, hs_ctx, GRID_W, 512, True)
        x = _outffn(*ys_l, x, mod, None, l, *ffn, 512)
        if not last:
            h_ctx = _outffn(*ys_c, h_ctx, mod, bsz, l, *ffn, t_ctx)
    return x
```

```python
import functools

import jax
import jax.numpy as jnp
import numpy as np
from jax import lax
from jax.experimental import pallas as pl
from jax.experimental.pallas import tpu as pltpu

F32 = jnp.float32
BF16 = jnp.bfloat16

D_MODEL = 1024
DEPTH = 4
GRID_W = 64
EPS = 1e-6

GLA_HEADS = 4
GLA_DK = 64
GLA_DV = 128
GLA_QK = GLA_HEADS * GLA_DK
GLA_V = GLA_HEADS * GLA_DV
GLA_RANK = 16
GLA_TAU = 16.0
GLA_CHUNK = 64
FNET_W = 256
FNET_GD = 64
RG_HEADS = 4
RG_HD = 64
RG_W = 256
RG_C = 8.0
CONV_W = 4
D_FF = 2816

LANES = 128
SUBLANES = 8
DEC_PAD = LANES
N_IN_PAD = 2 * GLA_QK + 2 * GLA_V + FNET_W + 2 * RG_W + DEC_PAD
MOD_ROWS = 16
VMEM_LIMIT = 56 * 1024 * 1024

_NT = (((1,), (1,)), ((), ()))
_TN = (((0,), (0,)), ((), ()))


def _rms(x, g):
    return x * lax.rsqrt(jnp.mean(x * x, axis=-1, keepdims=True) + EPS) * g


def _const_spec(shape):
    nd = len(shape)
    return pl.BlockSpec(shape, lambda *_: (0,) * nd, pipeline_mode=pl.Buffered(1))


def _layer_spec(arr, l):
    nd = arr.ndim - 1
    return pl.BlockSpec((None,) + arr.shape[1:], lambda *_: (l,) + (0,) * nd, pipeline_mode=pl.Buffered(1))


def _ada_body(c_ref, w_ref, b_ref, o_ref):
    c = c_ref[...]
    s = (c * jax.nn.sigmoid(c)).astype(BF16)
    o_ref[0] = jnp.dot(s, w_ref[0].astype(BF16), preferred_element_type=F32) + b_ref[0]


def _ada(c_all, w_ada, b_ada):
    depth, d, n = w_ada.shape
    tn = 1536
    return pl.pallas_call(
        _ada_body,
        out_shape=jax.ShapeDtypeStruct((depth, MOD_ROWS, n), F32),
        grid=(depth, n // tn),
        in_specs=[
            pl.BlockSpec((MOD_ROWS, d), lambda l, j: (0, 0)),
            pl.BlockSpec((1, d, tn), lambda l, j: (l, 0, j)),
            pl.BlockSpec((1, 1, tn), lambda l, j: (l, 0, j)),
        ],
        out_specs=pl.BlockSpec((1, MOD_ROWS, tn), lambda l, j: (l, 0, j)),
        compiler_params=pltpu.CompilerParams(vmem_limit_bytes=VMEM_LIMIT),
        name="ada",
    )(c_all, w_ada, b_ada.reshape(depth, 1, n))


ROW_SPLIT = 2

_IN_WIDTHS = (GLA_QK, GLA_QK, GLA_V, GLA_V, FNET_W, RG_W, RG_W, DEC_PAD)
_IN_F = 4


def _inproj_body(x_ref, mod_ref, g_ref, w_ref, *out_refs, f_pitch):
    tm = x_ref.shape[1]
    sub = tm // ROW_SPLIT
    for r in range(ROW_SPLIT):
        rs = slice(r * sub, (r + 1) * sub)
        h = _rms(x_ref[0, rs, :], g_ref[...]) * (1.0 + mod_ref[0, 1:2, :]) + mod_ref[0, 0:1, :]
        h = h.astype(BF16)
        off = 0
        for idx, (o_ref, wd) in enumerate(zip(out_refs, _IN_WIDTHS)):
            z = jnp.dot(h, w_ref[:, off:off + wd], preferred_element_type=F32)
            off += wd
            if idx == _IN_F and f_pitch != GRID_W:
                pad = jnp.zeros((f_pitch - GRID_W, wd), F32)
                for ln in range(sub // GRID_W):
                    base = (r * (sub // GRID_W) + ln) * f_pitch
                    o_ref[0, base:base + GRID_W, :] = z[ln * GRID_W:(ln + 1) * GRID_W, :]
                    o_ref[0, base + GRID_W:base + f_pitch, :] = pad
            else:
                o_ref[0, rs, :] = z


def _inproj(x, mod, mod_row, l, g, w_in_p, tm, f_pitch):
    bsz, t, d = x.shape
    if mod_row is None:
        mod_map = lambda b, i: (l, b, 0, 0)
    else:
        mod_map = lambda b, i: (l, mod_row, 0, 0)
    rows_of = lambda idx, n: n // GRID_W * f_pitch if idx == _IN_F else n
    return pl.pallas_call(
        functools.partial(_inproj_body, f_pitch=f_pitch),
        out_shape=[jax.ShapeDtypeStruct((bsz, rows_of(i, t), wd), F32) for i, wd in enumerate(_IN_WIDTHS)],
        grid=(bsz, t // tm),
        in_specs=[
            pl.BlockSpec((1, tm, d), lambda b, i: (b, i, 0)),
            pl.BlockSpec((None, 1, 6, d), mod_map),
            _layer_spec(g, l),
            _layer_spec(w_in_p, l),
        ],
        out_specs=[pl.BlockSpec((1, rows_of(i, tm), wd), lambda b, i: (b, i, 0)) for i, wd in enumerate(_IN_WIDTHS)],
        compiler_params=pltpu.CompilerParams(vmem_limit_bytes=VMEM_LIMIT),
        name="inproj",
    )(x, mod, g, w_in_p)


GLA_BLOCK = 256
GLA_RINGS = 8


def _chunk_cumsum(x, reverse):
    rows, lanes = x.shape
    g = SUBLANES
    c = GLA_CHUNK
    x3 = x.reshape(rows // g, g, lanes)
    sub = lax.broadcasted_iota(jnp.int32, x3.shape, 1)
    for s in (1, 2, 4):
        if reverse:
            x3 = x3 + jnp.where(sub < g - s, pltpu.roll(x3, g - s, 1), 0.0)
        else:
            x3 = x3 + jnp.where(sub >= s, pltpu.roll(x3, s, 1), 0.0)
    edge = x3[:, 0:1, :] if reverse else x3[:, g - 1:g, :]
    tot = jnp.broadcast_to(edge, x3.shape).reshape(rows, lanes)
    pos = lax.broadcasted_iota(jnp.int32, (rows, lanes), 0) & (c - 1)
    acc = tot
    for s in (8, 16, 32):
        if reverse:
            acc = acc + jnp.where(pos < c - s, pltpu.roll(acc, rows - s, 0), 0.0)
        else:
            acc = acc + jnp.where(pos >= s, pltpu.roll(acc, s, 0), 0.0)
    sums = x3.reshape(rows, lanes) + (acc - tot)
    parts = []
    for ci in range(rows // c):
        r = ci * c if reverse else ci * c + c - 1
        parts.append(jnp.broadcast_to(acc[r:r + 1, :], (c, lanes)))
    total = parts[0] if len(parts) == 1 else jnp.concatenate(parts, axis=0)
    return sums, total


def _gla_body(q_ref, k_ref, v_ref, og_ref, dec_ref, wd_ref, bd_ref, g_ref, s0_ref, *rest, t, with_output):
    nr = GLA_RINGS
    if with_output:
        y_ref, sfin_ref, u_ref, dl_ref, qt_ref, o_ref = rest[:6]
        bufs = rest[6:]
        rings = tuple((bufs[r], bufs[nr + r], bufs[2 * nr + r], bufs[3 * nr + r]) for r in range(nr))
    else:
        sfin_ref, u_ref, dl_ref = rest[:3]
        bufs = rest[3:]
        rings = tuple((bufs[r], bufs[nr + r], None, None) for r in range(nr))
    c = GLA_CHUNK
    hp = 2 * GLA_DK
    vp = 2 * GLA_DV
    rows = min(t, GLA_BLOCK)
    cpb = rows // c
    n_blk = t // rows
    scale = GLA_DK ** -0.5

    dk_head0 = lax.broadcasted_iota(jnp.int32, (hp, 2 * c), 0) < GLA_DK
    ar = lax.broadcasted_iota(jnp.int32, (2 * c, 4 * c), 0)
    ac = lax.broadcasted_iota(jnp.int32, (2 * c, 4 * c), 1)
    same_chunk = (ar // c) == ((ac // c) & 1)
    keep_f = same_chunk & ((ar & (c - 1)) >= (ac & (c - 1)))
    keep_b = same_chunk & ((ar & (c - 1)) <= (ac & (c - 1)))
    wd = wd_ref[0]
    bd = bd_ref[0]

    def prepare(blk, ring):
        vb_ring, ke_ring, qt_ring, kt_ring = ring
        r0 = pl.multiple_of(blk * rows, rows)
        code = dec_ref[0, pl.ds(r0, rows), :].astype(BF16)
        logit = jnp.dot(code, wd, preferred_element_type=F32) + bd
        la = (jnp.minimum(logit, 0.0) - jnp.log(1.0 + jnp.exp(-jnp.abs(logit)))) * (1.0 / GLA_TAU)
        k = k_ref[0, pl.ds(r0, rows), :]
        vb_ring[...] = v_ref[0, pl.ds(r0, rows), :].astype(BF16)
        if with_output:
            q = q_ref[0, pl.ds(r0, rows), :] * scale
        for dirn in range(2):
            b, b_all = _chunk_cumsum(la[:, dirn * hp:(dirn + 1) * hp], reverse=dirn == 1)
            ke_ring[dirn] = (k * jnp.exp(b_all - b)).astype(BF16)
            for ci in range(cpb):
                dl_ref[dirn, blk * cpb + ci] = jnp.exp(b_all[ci * c:ci * c + SUBLANES, :])
            if with_output:
                qt = (q * jnp.exp(b)).astype(BF16)
                qt_ref[dirn, pl.ds(r0, rows), :] = qt
                qt_ring[dirn] = qt
                kt_ring[dirn] = (k * jnp.exp(-b)).T.astype(BF16)

    def multiply(blk, ring):
        vb_ring, ke_ring, qt_ring, kt_ring = ring
        r0 = pl.multiple_of(blk * rows, rows)
        vb = vb_ring[...]
        for dirn in range(2):
            ke = ke_ring[dirn]
            for ci in range(cpb):
                sl = slice(ci * c, (ci + 1) * c)
                upd = lax.dot_general(ke[sl], vb[sl], _TN, preferred_element_type=F32)
                u_ref[dirn, blk * cpb + ci, 0] = upd[0:GLA_DK, 0:GLA_DV]
                u_ref[dirn, blk * cpb + ci, 1] = upd[GLA_DK:hp, GLA_DV:vp]
            if not with_output:
                continue
            qt = qt_ring[dirn]
            kt_t = kt_ring[dirn]
            zt = jnp.zeros((hp, 2 * c), BF16)
            zv = jnp.zeros((2 * c, GLA_DV), BF16)
            parts = []
            for cp in range(cpb // 2):
                sl = slice(cp * 2 * c, (cp + 1) * 2 * c)
                kt_cp = kt_t[:, sl]
                kbd = jnp.concatenate([jnp.where(dk_head0, kt_cp, zt), jnp.where(dk_head0, zt, kt_cp)], axis=1)
                sc = jnp.dot(qt[sl], kbd, preferred_element_type=F32)
                p = jnp.where(keep_f if dirn == 0 else keep_b, sc, 0.0).astype(BF16)
                v_cp = vb[sl]
                vbd = jnp.concatenate(
                    [jnp.concatenate([v_cp[:, 0:GLA_DV], zv], axis=1),
                     jnp.concatenate([zv, v_cp[:, GLA_DV:vp]], axis=1)], axis=0)
                parts.append(jnp.dot(p, vbd, preferred_element_type=F32))
            o_intra = parts[0] if len(parts) == 1 else jnp.concatenate(parts, axis=0)
            if dirn == 0:
                o_ref[pl.ds(r0, rows), :] = o_intra
            else:
                o_ref[pl.ds(r0, rows), :] += o_intra

    per = min(n_blk, nr)

    def pre_step(j, carry):
        i = per * j
        prepare(i, rings[0])
        for r in range(per):
            if r + 1 < per:
                prepare(i + r + 1, rings[r + 1])
            multiply(i + r, rings[r])
        return carry

    lax.fori_loop(0, n_blk // per, pre_step, 0)

    zs = jnp.zeros((GLA_DK, GLA_DV), BF16)

    def gate_block(r0):
        rs = pl.ds(r0, rows)
        o = o_ref[rs, :]
        og = og_ref[0, rs, :]
        g = g_ref[...]
        o0 = _rms(o[:, 0:GLA_DV], g)
        o1 = _rms(o[:, GLA_DV:vp], g)
        gate = og * jax.nn.sigmoid(og)
        y_ref[0, rs, :] = (jnp.concatenate([o0, o1], axis=1) * gate).astype(y_ref.dtype)

    for dirn in range(2):
        def seq_step(i, carry, dirn=dirn):
            s0, s1 = carry
            blk = i if dirn == 0 else n_blk - 1 - i
            r0 = pl.multiple_of(blk * rows, rows)
            for cj in range(cpb):
                ci = cj if dirn == 0 else cpb - 1 - cj
                chunk = blk * cpb + ci
                if with_output:
                    rs = pl.ds(r0 + ci * c, c)
                    sbd = jnp.concatenate(
                        [jnp.concatenate([s0.astype(BF16), zs], axis=1),
                         jnp.concatenate([zs, s1.astype(BF16)], axis=1)], axis=0)
                    o_ref[rs, :] += jnp.dot(qt_ref[dirn, rs, :], sbd, preferred_element_type=F32)
                drow = dl_ref[dirn, chunk][0:1, :]
                dcol = jnp.broadcast_to(drow, (hp, hp)).T
                s0 = s0 * dcol[0:GLA_DK, :] + u_ref[dirn, chunk, 0]
                s1 = s1 * dcol[GLA_DK:hp, :] + u_ref[dirn, chunk, 1]
            if with_output and dirn == 1:
                gate_block(r0)
            return s0, s1

        s0, s1 = lax.fori_loop(0, n_blk, seq_step, (s0_ref[0, dirn, 0], s0_ref[0, dirn, 1]), unroll=min(8 if dirn == 0 else 4, n_blk))
        sfin_ref[0, dirn, 0] = s0
        sfin_ref[0, dirn, 1] = s1


def _gla(zq, zk, zv, zog, zdec, l, wd_p, bd_p, g_gla, s0, with_output):
    bsz, t, _ = zq.shape
    hp, vp = 2 * GLA_DK, 2 * GLA_DV
    n_chunks = t // GLA_CHUNK
    seq = lambda wd: pl.BlockSpec((1, t, wd), lambda b, p: (b, 0, p))
    st_spec = pl.BlockSpec((1, 2, 2, GLA_DK, GLA_DV), lambda b, p: (b, 0, p, 0, 0))
    out_shape = [jax.ShapeDtypeStruct((bsz, 2, GLA_HEADS, GLA_DK, GLA_DV), F32)]
    out_specs = [st_spec]
    rows = min(t, GLA_BLOCK)
    nr = GLA_RINGS
    scratch = [pltpu.VMEM((2, n_chunks, 2, GLA_DK, GLA_DV), F32), pltpu.VMEM((2, n_chunks, SUBLANES, hp), F32)]
    staging = [pltpu.VMEM((rows, vp), BF16) for _ in range(nr)] + [pltpu.VMEM((2, rows, hp), BF16) for _ in range(nr)]
    if with_output:
        out_shape = [jax.ShapeDtypeStruct((bsz, t, GLA_V), BF16)] + out_shape
        out_specs = [seq(vp)] + out_specs
        scratch = scratch + [pltpu.VMEM((2, t, hp), BF16), pltpu.VMEM((t, vp), F32)]
        staging = staging + [pltpu.VMEM((2, rows, hp), BF16) for _ in range(nr)]
        staging = staging + [pltpu.VMEM((2, hp, rows), BF16) for _ in range(nr)]
    scratch = scratch + staging
    outs = pl.pallas_call(
        functools.partial(_gla_body, t=t, with_output=with_output),
        out_shape=out_shape,
        grid=(bsz, 2),
        in_specs=[
            seq(hp), seq(hp), seq(vp), seq(vp),
            pl.BlockSpec((1, t, DEC_PAD), lambda b, p: (b, 0, 0)),
            pl.BlockSpec((None, 1, DEC_PAD, 2 * hp), lambda b, p: (l, p, 0, 0)),
            pl.BlockSpec((None, 1, 1, 2 * hp), lambda b, p: (l, p, 0, 0)),
            _layer_spec(g_gla, l),
            st_spec,
        ],
        out_specs=out_specs,
        scratch_shapes=scratch,
        compiler_params=pltpu.CompilerParams(vmem_limit_bytes=VMEM_LIMIT),
        name="gla",
    )(zq, zk, zv, zog, zdec, wd_p, bd_p, g_gla, s0)
    if with_output:
        return outs[0], outs[1]
    return None, outs[0]


RG_BLOCK = 256


def _sigmoid(x):
    return 0.5 * jnp.tanh(0.5 * x) + 0.5


def _rg_body(rx_ref, rg_ref, wc_ref, bc_ref, wg_ref, bg_ref, lam_ref, h0_ref, *rest, t, line, with_output):
    if with_output:
        y_ref, hfin_ref, hf_ref, ab_ref, bb_ref = rest
    else:
        hfin_ref, hf_ref, ab_ref, bb_ref = rest
    w = RG_W
    rows = min(t, RG_BLOCK)
    n_blk = t // rows
    grp = SUBLANES
    gpb = rows // grp
    pos = lax.broadcasted_iota(jnp.int32, (rows, w), 0) & (line - 1)
    sub = lax.broadcasted_iota(jnp.int32, (rows // grp, grp, w), 1)
    wc = wc_ref[...]
    bc = bc_ref[...]
    wg = wg_ref[...]
    bg = bg_ref[...]
    sp = [jax.nn.softplus(-lam_ref[d:d + 1, :]) * (-RG_C) for d in range(2)]

    def pre_step(blk, hf):
        r0 = pl.multiple_of(blk * rows, rows)
        rs = pl.ds(r0, rows)
        x = rx_ref[0, rs, :]
        xm1 = jnp.where(pos >= 1, pltpu.roll(x, 1, 0), 0.0)
        xp1 = jnp.where(pos <= line - 2, pltpu.roll(x, rows - 1, 0), 0.0)
        xp2 = jnp.where(pos <= line - 3, pltpu.roll(x, rows - 2, 0), 0.0)
        u = bc + xm1 * wc[0:1, :] + x * wc[1:2, :] + xp1 * wc[2:3, :] + xp2 * wc[3:4, :]
        gates = jnp.dot(u.astype(BF16), wg, preferred_element_type=F32) + bg
        coef = []
        for dirn in range(2):
            r = _sigmoid(gates[:, (2 * dirn) * w:(2 * dirn + 1) * w])
            i = _sigmoid(gates[:, (2 * dirn + 1) * w:(2 * dirn + 2) * w])
            log_a = r * sp[dirn]
            a = jnp.exp(log_a)
            bx = jnp.sqrt(-jnp.tanh(log_a) * (a * a + 1.0)) * (i * u)
            a3 = a.reshape(rows // grp, grp, w)
            b3 = bx.reshape(rows // grp, grp, w)
            for s in (1, 2, 4):
                if dirn == 0:
                    ok = sub >= s
                    sh = s
                else:
                    ok = sub <= grp - 1 - s
                    sh = grp - s
                b3 = jnp.where(ok, a3 * pltpu.roll(b3, sh, 1) + b3, b3)
                a3 = jnp.where(ok, a3 * pltpu.roll(a3, sh, 1), a3)
            coef.append((a3.reshape(rows, w), b3.reshape(rows, w)))
        ab_ref[rs, :] = coef[1][0]
        bb_ref[rs, :] = coef[1][1]
        af, bf = coef[0]
        for gi in range(gpb):
            g0 = gi * grp
            hfull = bf[g0:g0 + grp, :] + af[g0:g0 + grp, :] * hf
            hf_ref[pl.ds(r0 + g0, grp), :] = hfull
            hf = hfull[grp - 1:grp, :]
        return hf

    hf = lax.fori_loop(0, n_blk, pre_step, h0_ref[0, 0:1, :])

    def bwd_step(j, hb):
        blk = n_blk - 1 - j
        r0 = pl.multiple_of(blk * rows, rows)
        parts = []
        for gi in range(gpb - 1, -1, -1):
            g0 = r0 + gi * grp
            hbull = bb_ref[pl.ds(g0, grp), :] + ab_ref[pl.ds(g0, grp), :] * hb
            hb = hbull[0:1, :]
            parts.append(hbull)
        if with_output:
            hbwd = jnp.concatenate(parts[::-1], axis=0)
            rs = pl.ds(r0, rows)
            h = hf_ref[rs, :] + hbwd
            y_ref[0, rs, :] = (h * jax.nn.gelu(rg_ref[0, rs, :])).astype(y_ref.dtype)
        return hb

    hb = lax.fori_loop(0, n_blk, bwd_step, h0_ref[0, 1:2, :])
    hfin_ref[0, 0:1, :] = hf
    hfin_ref[0, 1:2, :] = hb
    hfin_ref[0, 2:grp, :] = jnp.zeros((grp - 2, w), F32)


def _rg(zrx, zrg, l, w_conv, b_conv, w_gate, b_gate, lam, h0, line, with_output):
    bsz, t, w = zrx.shape
    seq = pl.BlockSpec((1, t, w), lambda b: (b, 0, 0))
    st_spec = pl.BlockSpec((1, SUBLANES, w), lambda b: (b, 0, 0))
    out_shape = [jax.ShapeDtypeStruct((bsz, SUBLANES, w), F32)]
    out_specs = [st_spec]
    if with_output:
        out_shape = [jax.ShapeDtypeStruct((bsz, t, w), BF16)] + out_shape
        out_specs = [seq] + out_specs
    outs = pl.pallas_call(
        functools.partial(_rg_body, t=t, line=line, with_output=with_output),
        out_shape=out_shape,
        grid=(bsz,),
        in_specs=[
            seq, seq,
            _layer_spec(w_conv, l), _layer_spec(b_conv, l),
            _layer_spec(w_gate, l), _layer_spec(b_gate, l),
            _layer_spec(lam, l),
            st_spec,
        ],
        out_specs=out_specs,
        scratch_shapes=[pltpu.VMEM((t, w), F32) for _ in range(3)],
        compiler_params=pltpu.CompilerParams(vmem_limit_bytes=VMEM_LIMIT),
        name="rglru",
    )(zrx, zrg, w_conv, b_conv, w_gate, b_gate, lam, h0)
    if with_output:
        return outs[0], outs[1]
    return None, outs[0]


FFT_PITCH = 72


def _dft_mats(n):
    j = np.arange(n)
    ang = 2.0 * np.pi * ((j[:, None] * j[None, :]) % n) / n
    return np.cos(ang), np.sin(ang)


def _fourier_consts(t):
    n2 = GRID_W
    n1 = t // n2
    cc, sc = _dft_mats(FNET_GD)
    eye2 = np.eye(LANES // FNET_GD)
    m3 = np.concatenate([np.kron(eye2, cc), -np.kron(eye2, sc)], axis=0) / 8.0
    c2, s2 = _dft_mats(n2)
    consts = {"m3": jnp.asarray(m3, F32)}
    if n1 == n2:
        consts["cs"] = jnp.asarray(np.concatenate([c2, s2], axis=0) / 8.0, F32)
        consts["m2"] = jnp.asarray(np.block([[c2, -s2], [s2, c2]]) / 8.0, F32)
        wk = (np.arange(n2)[:, None] * np.arange(n1)[None, :]).reshape(-1)
        ang = 2.0 * np.pi * wk / t
        consts["twc"] = jnp.asarray(np.repeat(np.cos(ang)[:, None], LANES, axis=1), F32)
        consts["tws"] = jnp.asarray(np.repeat(np.sin(ang)[:, None], LANES, axis=1), F32)
    else:
        ct, st = _dft_mats(t)
        consts["cs"] = jnp.asarray(np.concatenate([ct, st], axis=0) / np.sqrt(t), F32)
    return consts


def _fft_body(xa_ref, xb_ref, cs_ref, m2_ref, m3_ref, twc_ref, tws_ref, y_ref, *scr, t):
    n = GRID_W
    p = FFT_PITCH
    cs = cs_ref[...].astype(BF16)
    m2 = m2_ref[...].astype(BF16)
    m3 = m3_ref[...].astype(BF16)
    halves = ((xa_ref,) + tuple(scr[0:4]), (xb_ref,) + tuple(scr[4:8]))

    def stage1(h, i):
        x_ref, sr_ref, si_ref, _, _ = halves[h]
        w0 = 2 * i
        xw = jnp.concatenate([x_ref[0, pl.ds(w0 + j, n, stride=p), :] for j in range(2)], axis=1)
        a = jnp.dot(cs, xw.astype(BF16), preferred_element_type=F32)
        for j in range(2):
            ar = a[0:n, j * LANES:(j + 1) * LANES]
            ai = a[n:2 * n, j * LANES:(j + 1) * LANES]
            r0 = pl.multiple_of((w0 + j) * n, n)
            tc = twc_ref[pl.ds(r0, n), :]
            ts = tws_ref[pl.ds(r0, n), :]
            sr_ref[pl.ds(w0 + j, n, stride=p), :] = ar * tc - ai * ts
            si_ref[pl.ds(w0 + j, n, stride=p), :] = ar * ts + ai * tc

    def stage2(h, i):
        _, sr_ref, si_ref, dr_ref, di_ref = halves[h]
        k0 = 2 * i
        cols = []
        for j in range(2):
            r0 = pl.multiple_of((k0 + j) * p, SUBLANES)
            cols.append(jnp.concatenate([sr_ref[pl.ds(r0, n), :], si_ref[pl.ds(r0, n), :]], axis=0))
        xb = jnp.concatenate(cols, axis=1).astype(BF16)
        d = jnp.dot(m2, xb, preferred_element_type=F32)
        for j in range(2):
            dr_ref[pl.ds(k0 + j, n, stride=p), :] = d[0:n, j * LANES:(j + 1) * LANES]
            di_ref[pl.ds(k0 + j, n, stride=p), :] = d[n:2 * n, j * LANES:(j + 1) * LANES]

    per = 8

    def stage3(h, i):
        _, _, _, dr_ref, di_ref = halves[h]
        blocks = []
        for j in range(per):
            r0 = pl.multiple_of((i * per + j) * p, SUBLANES)
            blocks.append(jnp.concatenate([dr_ref[pl.ds(r0, n), :], di_ref[pl.ds(r0, n), :]], axis=1))
        d = jnp.concatenate(blocks, axis=0).astype(BF16)
        rs = pl.ds(pl.multiple_of(i * per * n, per * n), per * n)
        y_ref[0, rs, h * LANES:(h + 1) * LANES] = jnp.dot(d, m3, preferred_element_type=F32).astype(y_ref.dtype)

    def loop(trips, unroll, *stages):
        def body(i, carry):
            for fn in stages:
                fn(i)
            return carry
        lax.fori_loop(0, trips, body, 0, unroll=unroll)

    ratio = (n // 2) // (n // per)
    loop(n // 2, 8, lambda i: stage1(0, i))
    loop(n // 2, 8, lambda i: stage2(0, i), lambda i: stage1(1, i))
    loop(n // per, 2, lambda i: stage3(0, i), lambda i: [stage2(1, ratio * i + k) for k in range(ratio)])
    loop(n // per, 2, lambda i: stage3(1, i))


def _dense_dft_body(x_ref, cs_ref, m3_ref, y_ref, *, t):
    a = jnp.dot(cs_ref[...].astype(BF16), x_ref[0].astype(BF16), preferred_element_type=F32)
    d = jnp.concatenate([a[0:t], a[t:2 * t]], axis=1).astype(BF16)
    y_ref[0] = jnp.dot(d, m3_ref[...].astype(BF16), preferred_element_type=F32).astype(y_ref.dtype)


def _fourier(zf, consts, t):
    bsz, t_in, w = zf.shape
    seq = pl.BlockSpec((1, t, LANES), lambda b, j: (b, 0, j))
    seq_in = pl.BlockSpec((1, t_in, LANES), lambda b, j: (b, 0, j))
    common = dict(
        out_shape=jax.ShapeDtypeStruct((bsz, t, w), BF16),
        grid=(bsz, w // LANES),
        out_specs=seq,
        compiler_params=pltpu.CompilerParams(vmem_limit_bytes=VMEM_LIMIT),
    )
    if "m2" in consts:
        full = pl.BlockSpec((1, t, w), lambda b: (b, 0, 0))
        half = lambda j: pl.BlockSpec((1, t_in, LANES), lambda b: (b, 0, j))
        return pl.pallas_call(
            functools.partial(_fft_body, t=t),
            out_shape=jax.ShapeDtypeStruct((bsz, t, w), BF16),
            grid=(bsz,),
            in_specs=[half(0), half(1), _const_spec(consts["cs"].shape), _const_spec(consts["m2"].shape),
                      _const_spec(consts["m3"].shape), _const_spec((t, LANES)), _const_spec((t, LANES))],
            out_specs=full,
            scratch_shapes=[pltpu.VMEM((GRID_W * FFT_PITCH, LANES), F32) for _ in range(8)],
            compiler_params=pltpu.CompilerParams(vmem_limit_bytes=VMEM_LIMIT),
            name="fourier_fft",
        )(zf, zf, consts["cs"], consts["m2"], consts["m3"], consts["twc"], consts["tws"])
    return pl.pallas_call(
        functools.partial(_dense_dft_body, t=t),
        in_specs=[seq_in, _const_spec(consts["cs"].shape), _const_spec(consts["m3"].shape)],
        name="fourier_dense", **common,
    )(zf, consts["cs"], consts["m3"])


FF_PIECES = (1536, 1280)


def _outffn_body(yg_ref, yf_ref, yr_ref, x_ref, mod_ref, gpm_ref, gpf_ref, gqf_ref,
                 wo_ref, wg_ref, wu_ref, wd_ref, o_ref):
    tm = x_ref.shape[1]
    sub = tm // ROW_SPLIT
    subs = [slice(r * sub, (r + 1) * sub) for r in range(ROW_SPLIT)]
    mixes = []
    for rs in subs:
        y = jnp.concatenate([yg_ref[0, rs, :], yf_ref[0, rs, :], yr_ref[0, rs, :]], axis=1)
        mixes.append(jnp.dot(y, wo_ref[...], preferred_element_type=F32))
    x1s, hfs = [], []
    for rs, mix in zip(subs, mixes):
        x1 = x_ref[0, rs, :] + mod_ref[0, 2:3, :] * _rms(mix, gpm_ref[...])
        x1s.append(x1)
        hfs.append((_rms(x1, gpf_ref[...]) * (1.0 + mod_ref[0, 4:5, :]) + mod_ref[0, 3:4, :]).astype(BF16))
    for rs, x1, hf in zip(subs, x1s, hfs):
        acc = None
        off = 0
        for fw in FF_PIECES:
            gate = jnp.dot(hf, wg_ref[:, off:off + fw], preferred_element_type=F32)
            up = jnp.dot(hf, wu_ref[:, off:off + fw], preferred_element_type=F32)
            act = (gate * jax.nn.sigmoid(gate) * up).astype(BF16)
            part = jnp.dot(act, wd_ref[off:off + fw, :], preferred_element_type=F32)
            acc = part if acc is None else acc + part
            off += fw
        o_ref[0, rs, :] = x1 + mod_ref[0, 5:6, :] * _rms(acc, gqf_ref[...])


def _outffn(yg, yf, yr, x, mod, mod_row, l, gpm, gpf, gqf, wo, wg, wu, wd, tm):
    bsz, t, d = x.shape
    if mod_row is None:
        mod_map = lambda b, i: (l, b, 0, 0)
    else:
        mod_map = lambda b, i: (l, mod_row, 0, 0)
    tile = lambda wdt: pl.BlockSpec((1, tm, wdt), lambda b, i: (b, i, 0))
    return pl.pallas_call(
        _outffn_body,
        out_shape=jax.ShapeDtypeStruct((bsz, t, d), F32),
        grid=(bsz, t // tm),
        in_specs=[
            tile(GLA_V), tile(FNET_W), tile(RG_W), tile(d),
            pl.BlockSpec((None, 1, 6, d), mod_map),
            _layer_spec(gpm, l), _layer_spec(gpf, l), _layer_spec(gqf, l),
            _layer_spec(wo, l), _layer_spec(wg, l), _layer_spec(wu, l), _layer_spec(wd, l),
        ],
        out_specs=tile(d),
        compiler_params=pltpu.CompilerParams(vmem_limit_bytes=VMEM_LIMIT),
        name="outffn",
    )(yg, yf, yr, x, mod, gpm, gpf, gqf, wo, wg, wu, wd)


def _prep_w_in(w_in):
    off_dec = 2 * GLA_QK + GLA_V
    off_og = off_dec + 2 * GLA_RANK
    pad = jnp.zeros(w_in.shape[:2] + (DEC_PAD - 2 * GLA_RANK,), w_in.dtype)
    return jnp.concatenate([w_in[..., 0:off_dec], w_in[..., off_og:], w_in[..., off_dec:off_og], pad], axis=-1).astype(BF16)


def _prep_dec(w_dec, b_dec):
    depth = w_dec.shape[0]
    hp = 2 * GLA_DK
    wr = w_dec.reshape(depth, 2, GLA_RANK, 2, hp)
    wbd = jnp.einsum("ldrpj,de->lpdrej", wr, jnp.eye(2, dtype=w_dec.dtype))
    wbd = wbd.reshape(depth, 2, 2 * GLA_RANK, 2 * hp)
    wbd = jnp.pad(wbd, ((0, 0), (0, 0), (0, DEC_PAD - 2 * GLA_RANK), (0, 0)))
    bd = b_dec.reshape(depth, 2, 2, hp).transpose(0, 2, 1, 3).reshape(depth, 2, 1, 2 * hp)
    return wbd.astype(BF16), bd


def _block_diag(w):
    h, hd = w.shape[-3], w.shape[-2]
    out = jnp.einsum("...hij,hg->...higj", w, jnp.eye(h, dtype=w.dtype))
    return out.reshape(w.shape[:-3] + (h * hd, h * hd))


def _prep_rg(w_a, b_a, w_x, b_x):
    bda, bdx = _block_diag(w_a), _block_diag(w_x)
    wg = jnp.concatenate([bda[:, 0], bdx[:, 0], bda[:, 1], bdx[:, 1]], axis=-1)
    bg = jnp.concatenate([b_a[:, 0], b_x[:, 0], b_a[:, 1], b_x[:, 1]], axis=-1)[:, None, :]
    return wg.astype(BF16), bg


def _mixer(h, mod, mod_row, l, lw, consts, s0, h0, line, tm, with_output):
    t = h.shape[1]
    f_pitch = FFT_PITCH if "m2" in consts else GRID_W
    zq, zk, zv, zog, zf, zrx, zrg, zdec = _inproj(h, mod, mod_row, l, lw["g_pre_mix"], lw["w_in"], tm, f_pitch)
    yg, s_fin = _gla(zq, zk, zv, zog, zdec, l, lw["wd"], lw["bd"], lw["g_gla"], s0, with_output)
    yr, h_fin = _rg(zrx, zrg, l, lw["w_conv"], lw["b_conv"], lw["w_gate"], lw["b_gate"], lw["lam"], h0, line, with_output)
    yf = _fourier(zf, consts, t) if with_output else None
    return (yg, yf, yr), s_fin, h_fin


def kernel(x, c, ctx, c_ctx, w_ada, b_ada, g_pre_mix, g_post_mix, g_pre_ffn, g_post_ffn, w_in, w_dec, b_dec, g_gla,
           w_conv, b_conv, w_rg_a, b_rg_a, w_rg_x, b_rg_x, rg_lam, w_out, w_ffn_gate, w_ffn_up, w_ffn_down):
    bsz, t, d = x.shape
    t_ctx = ctx.shape[1]
    depth = w_ada.shape[0]

    c_all = jnp.concatenate([c, c_ctx[None, :], jnp.zeros((MOD_ROWS - bsz - 1, d), F32)], axis=0)
    mod = _ada(c_all, w_ada, b_ada).reshape(depth, MOD_ROWS, 6, d)

    consts_lat = _fourier_consts(t)
    consts_ctx = _fourier_consts(t_ctx)
    s_zero = jnp.zeros((bsz, 2, GLA_HEADS, GLA_DK, GLA_DV), F32)
    h_zero = jnp.zeros((bsz, SUBLANES, RG_W), F32)
    rows = lambda v: v[:, None, :]

    wd_p, bd_p = _prep_dec(w_dec, b_dec)
    w_gate, b_gate = _prep_rg(w_rg_a, b_rg_a, w_rg_x, b_rg_x)
    lw = dict(g_pre_mix=rows(g_pre_mix), w_in=_prep_w_in(w_in), wd=wd_p, bd=bd_p, g_gla=rows(g_gla),
              w_conv=w_conv, b_conv=rows(b_conv), w_gate=w_gate, b_gate=b_gate, lam=rg_lam)
    ffn = (rows(g_post_mix), rows(g_pre_ffn), rows(g_post_ffn), w_out.astype(BF16),
           w_ffn_gate.astype(BF16), w_ffn_up.astype(BF16), w_ffn_down.astype(BF16))

    h_ctx = ctx
    for l in range(depth):
        last = l == depth - 1
        ys_c, s_ctx, hs_ctx = _mixer(h_ctx, mod, bsz, l, lw, consts_ctx, s_zero, h_zero, t_ctx, t_ctx, not last)
        ys_l, _, _ = _mixer(x, mod, None, l, lw, consts_lat, s_ctx, hs_ctx, GRID_W, 512, True)
        x = _outffn(*ys_l, x, mod, None, l, *ffn, 512)
        if not last:
            h_ctx = _outffn(*ys_c, h_ctx, mod, bsz, l, *ffn, t_ctx)
    return x
```

```python
import functools

import jax
import jax.numpy as jnp
import numpy as np
from jax import lax
from jax.experimental import pallas as pl
from jax.experimental.pallas import tpu as pltpu

F32 = jnp.float32
BF16 = jnp.bfloat16

D_MODEL = 1024
DEPTH = 4
GRID_W = 64
EPS = 1e-6

GLA_HEADS = 4
GLA_DK = 64
GLA_DV = 128
GLA_QK = GLA_HEADS * GLA_DK
GLA_V = GLA_HEADS * GLA_DV
GLA_RANK = 16
GLA_TAU = 16.0
GLA_CHUNK = 64
FNET_W = 256
FNET_GD = 64
RG_HEADS = 4
RG_HD = 64
RG_W = 256
RG_C = 8.0
CONV_W = 4
D_FF = 2816

LANES = 128
SUBLANES = 8
DEC_PAD = LANES
N_IN_PAD = 2 * GLA_QK + 2 * GLA_V + FNET_W + 2 * RG_W + DEC_PAD
MOD_ROWS = 16
VMEM_LIMIT = 56 * 1024 * 1024

_NT = (((1,), (1,)), ((), ()))
_TN = (((0,), (0,)), ((), ()))


def _rms(x, g):
    return x * lax.rsqrt(jnp.mean(x * x, axis=-1, keepdims=True) + EPS) * g


def _const_spec(shape):
    nd = len(shape)
    return pl.BlockSpec(shape, lambda *_: (0,) * nd, pipeline_mode=pl.Buffered(1))


def _layer_spec(arr, l):
    nd = arr.ndim - 1
    return pl.BlockSpec((None,) + arr.shape[1:], lambda *_: (l,) + (0,) * nd, pipeline_mode=pl.Buffered(1))


def _ada_body(c_ref, w_ref, b_ref, o_ref):
    c = c_ref[...]
    s = (c * jax.nn.sigmoid(c)).astype(BF16)
    o_ref[0] = jnp.dot(s, w_ref[0].astype(BF16), preferred_element_type=F32) + b_ref[0]


def _ada(c_all, w_ada, b_ada):
    depth, d, n = w_ada.shape
    tn = 1536
    return pl.pallas_call(
        _ada_body,
        out_shape=jax.ShapeDtypeStruct((depth, MOD_ROWS, n), F32),
        grid=(depth, n // tn),
        in_specs=[
            pl.BlockSpec((MOD_ROWS, d), lambda l, j: (0, 0)),
            pl.BlockSpec((1, d, tn), lambda l, j: (l, 0, j)),
            pl.BlockSpec((1, 1, tn), lambda l, j: (l, 0, j)),
        ],
        out_specs=pl.BlockSpec((1, MOD_ROWS, tn), lambda l, j: (l, 0, j)),
        compiler_params=pltpu.CompilerParams(vmem_limit_bytes=VMEM_LIMIT),
        name="ada",
    )(c_all, w_ada, b_ada.reshape(depth, 1, n))


ROW_SPLIT = 2

_IN_WIDTHS = (GLA_QK, GLA_QK, GLA_V, GLA_V, FNET_W, RG_W, RG_W, DEC_PAD)
_IN_F = 4
_IN_BLOCKS = (2, 2, 2, 2, 2, 1, 1, 1)


def _inproj_body(x_ref, mod_ref, g_ref, w_ref, *out_refs, f_pitch):
    tm = x_ref.shape[1]
    sub = tm // ROW_SPLIT
    for r in range(ROW_SPLIT):
        rs = slice(r * sub, (r + 1) * sub)
        h = _rms(x_ref[0, rs, :], g_ref[...]) * (1.0 + mod_ref[0, 1:2, :]) + mod_ref[0, 0:1, :]
        h = h.astype(BF16)
        off = 0
        for idx, (o_ref, wd, nb) in enumerate(zip(out_refs, _IN_WIDTHS, _IN_BLOCKS)):
            z = jnp.dot(h, w_ref[:, off:off + wd], preferred_element_type=F32)
            off += wd
            wb = wd // nb
            for p in range(nb):
                zp = z[:, p * wb:(p + 1) * wb]
                dst = o_ref.at[0, p] if nb > 1 else o_ref.at[0]
                if idx == _IN_F and f_pitch != GRID_W:
                    pad = jnp.zeros((f_pitch - GRID_W, wb), F32)
                    for ln in range(sub // GRID_W):
                        base = (r * (sub // GRID_W) + ln) * f_pitch
                        dst[base:base + GRID_W, :] = zp[ln * GRID_W:(ln + 1) * GRID_W, :]
                        dst[base + GRID_W:base + f_pitch, :] = pad
                else:
                    dst[rs, :] = zp


def _inproj(x, mod, mod_row, l, g, w_in_p, tm, f_pitch):
    bsz, t, d = x.shape
    if mod_row is None:
        mod_map = lambda b, i: (l, b, 0, 0)
    else:
        mod_map = lambda b, i: (l, mod_row, 0, 0)
    rows_of = lambda idx, n: n // GRID_W * f_pitch if idx == _IN_F else n

    def shape_of(idx, n):
        wd, nb = _IN_WIDTHS[idx], _IN_BLOCKS[idx]
        return (nb, rows_of(idx, n), wd // nb) if nb > 1 else (rows_of(idx, n), wd)

    n_out = len(_IN_WIDTHS)
    return pl.pallas_call(
        functools.partial(_inproj_body, f_pitch=f_pitch),
        out_shape=[jax.ShapeDtypeStruct((bsz,) + shape_of(i, t), F32) for i in range(n_out)],
        grid=(bsz, t // tm),
        in_specs=[
            pl.BlockSpec((1, tm, d), lambda b, i: (b, i, 0)),
            pl.BlockSpec((None, 1, 6, d), mod_map),
            _layer_spec(g, l),
            _layer_spec(w_in_p, l),
        ],
        out_specs=[
            pl.BlockSpec((1,) + shape_of(i, tm), (lambda b, j: (b, 0, j, 0)) if _IN_BLOCKS[i] > 1 else (lambda b, j: (b, j, 0)))
            for i in range(n_out)],
        compiler_params=pltpu.CompilerParams(vmem_limit_bytes=VMEM_LIMIT),
        name="inproj",
    )(x, mod, g, w_in_p)


GLA_BLOCK = 256
GLA_RINGS = 8


def _chunk_cumsum(x, reverse):
    rows, lanes = x.shape
    g = SUBLANES
    c = GLA_CHUNK
    x3 = x.reshape(rows // g, g, lanes)
    sub = lax.broadcasted_iota(jnp.int32, x3.shape, 1)
    for s in (1, 2, 4):
        if reverse:
            x3 = x3 + jnp.where(sub < g - s, pltpu.roll(x3, g - s, 1), 0.0)
        else:
            x3 = x3 + jnp.where(sub >= s, pltpu.roll(x3, s, 1), 0.0)
    edge = x3[:, 0:1, :] if reverse else x3[:, g - 1:g, :]
    tot = jnp.broadcast_to(edge, x3.shape).reshape(rows, lanes)
    pos = lax.broadcasted_iota(jnp.int32, (rows, lanes), 0) & (c - 1)
    acc = tot
    for s in (8, 16, 32):
        if reverse:
            acc = acc + jnp.where(pos < c - s, pltpu.roll(acc, rows - s, 0), 0.0)
        else:
            acc = acc + jnp.where(pos >= s, pltpu.roll(acc, s, 0), 0.0)
    sums = x3.reshape(rows, lanes) + (acc - tot)
    parts = []
    for ci in range(rows // c):
        r = ci * c if reverse else ci * c + c - 1
        parts.append(jnp.broadcast_to(acc[r:r + 1, :], (c, lanes)))
    total = parts[0] if len(parts) == 1 else jnp.concatenate(parts, axis=0)
    return sums, total


def _gla_body(q_ref, k_ref, v_ref, og_ref, dec_ref, wd_ref, bd_ref, g_ref, s0_ref, *rest, t, with_output):
    nr = GLA_RINGS
    if with_output:
        y_ref, sfin_ref, u_ref, dl_ref, qt_ref, o_ref = rest[:6]
        bufs = rest[6:]
        rings = tuple((bufs[r], bufs[nr + r], bufs[2 * nr + r], bufs[3 * nr + r]) for r in range(nr))
    else:
        sfin_ref, u_ref, dl_ref = rest[:3]
        bufs = rest[3:]
        rings = tuple((bufs[r], bufs[nr + r], None, None) for r in range(nr))
    c = GLA_CHUNK
    hp = 2 * GLA_DK
    vp = 2 * GLA_DV
    rows = min(t, GLA_BLOCK)
    cpb = rows // c
    n_blk = t // rows

    dk_head0 = lax.broadcasted_iota(jnp.int32, (hp, 2 * c), 0) < GLA_DK
    ar = lax.broadcasted_iota(jnp.int32, (2 * c, 4 * c), 0)
    ac = lax.broadcasted_iota(jnp.int32, (2 * c, 4 * c), 1)
    same_chunk = (ar // c) == ((ac // c) & 1)
    keep_f = same_chunk & ((ar & (c - 1)) >= (ac & (c - 1)))
    keep_b = same_chunk & ((ar & (c - 1)) <= (ac & (c - 1)))
    wd = wd_ref[0]
    bd = bd_ref[0]

    def prepare(blk, ring):
        vb_ring, ke_ring, qt_ring, kt_ring = ring
        r0 = pl.multiple_of(blk * rows, rows)
        code = dec_ref[0, pl.ds(r0, rows), :].astype(BF16)
        logit = jnp.dot(code, wd, preferred_element_type=F32) + bd
        la = (jnp.minimum(logit, 0.0) - jnp.log(1.0 + jnp.exp(-jnp.abs(logit)))) * (1.0 / GLA_TAU)
        k = k_ref[0, pl.ds(r0, rows), :]
        vb_ring[...] = v_ref[0, pl.ds(r0, rows), :].astype(BF16)
        if with_output:
            q = q_ref[0, pl.ds(r0, rows), :]
        for dirn in range(2):
            b, b_all = _chunk_cumsum(la[:, dirn * hp:(dirn + 1) * hp], reverse=dirn == 1)
            ke_ring[dirn] = (k * jnp.exp(b_all - b)).astype(BF16)
            for ci in range(cpb):
                dl_ref[dirn, blk * cpb + ci] = jnp.exp(b_all[ci * c:ci * c + SUBLANES, :])
            if with_output:
                qt = (q * jnp.exp(b)).astype(BF16)
                qt_ref[dirn, pl.ds(r0, rows), :] = qt
                qt_ring[dirn] = qt
                kt_ring[dirn] = (k * jnp.exp(-b)).T.astype(BF16)

    def multiply(blk, ring):
        vb_ring, ke_ring, qt_ring, kt_ring = ring
        r0 = pl.multiple_of(blk * rows, rows)
        vb = vb_ring[...]
        for dirn in range(2):
            ke = ke_ring[dirn]
            for ci in range(cpb):
                sl = slice(ci * c, (ci + 1) * c)
                upd = lax.dot_general(ke[sl], vb[sl], _TN, preferred_element_type=F32)
                u_ref[dirn, blk * cpb + ci, 0] = upd[0:GLA_DK, 0:GLA_DV]
                u_ref[dirn, blk * cpb + ci, 1] = upd[GLA_DK:hp, GLA_DV:vp]
            if not with_output:
                continue
            qt = qt_ring[dirn]
            kt_t = kt_ring[dirn]
            zt = jnp.zeros((hp, 2 * c), BF16)
            zv = jnp.zeros((2 * c, GLA_DV), BF16)
            parts = []
            for cp in range(cpb // 2):
                sl = slice(cp * 2 * c, (cp + 1) * 2 * c)
                kt_cp = kt_t[:, sl]
                kbd = jnp.concatenate([jnp.where(dk_head0, kt_cp, zt), jnp.where(dk_head0, zt, kt_cp)], axis=1)
                sc = jnp.dot(qt[sl], kbd, preferred_element_type=F32)
                p = jnp.where(keep_f if dirn == 0 else keep_b, sc, 0.0).astype(BF16)
                v_cp = vb[sl]
                vbd = jnp.concatenate(
                    [jnp.concatenate([v_cp[:, 0:GLA_DV], zv], axis=1),
                     jnp.concatenate([zv, v_cp[:, GLA_DV:vp]], axis=1)], axis=0)
                parts.append(jnp.dot(p, vbd, preferred_element_type=F32))
            o_intra = parts[0] if len(parts) == 1 else jnp.concatenate(parts, axis=0)
            if dirn == 0:
                o_ref[pl.ds(r0, rows), :] = o_intra
            else:
                o_ref[pl.ds(r0, rows), :] += o_intra

    per = min(n_blk, nr)

    def pre_step(j, carry):
        i = per * j
        prepare(i, rings[0])
        for r in range(per):
            if r + 1 < per:
                prepare(i + r + 1, rings[r + 1])
            multiply(i + r, rings[r])
        return carry

    lax.fori_loop(0, n_blk // per, pre_step, 0)

    zs = jnp.zeros((GLA_DK, GLA_DV), BF16)

    def gate_block(r0):
        rs = pl.ds(r0, rows)
        o = o_ref[rs, :]
        og = og_ref[0, rs, :]
        g = g_ref[...]
        o0 = _rms(o[:, 0:GLA_DV], g)
        o1 = _rms(o[:, GLA_DV:vp], g)
        gate = og * _sigmoid(og)
        y_ref[0, rs, :] = (jnp.concatenate([o0, o1], axis=1) * gate).astype(y_ref.dtype)

    for dirn in range(2):
        def seq_step(i, carry, dirn=dirn):
            s0, s1 = carry
            blk = i if dirn == 0 else n_blk - 1 - i
            r0 = pl.multiple_of(blk * rows, rows)
            for cj in range(cpb):
                ci = cj if dirn == 0 else cpb - 1 - cj
                chunk = blk * cpb + ci
                if with_output:
                    rs = pl.ds(r0 + ci * c, c)
                    sbd = jnp.concatenate(
                        [jnp.concatenate([s0.astype(BF16), zs], axis=1),
                         jnp.concatenate([zs, s1.astype(BF16)], axis=1)], axis=0)
                    o_ref[rs, :] += jnp.dot(qt_ref[dirn, rs, :], sbd, preferred_element_type=F32)
                drow = dl_ref[dirn, chunk][0:1, :]
                dcol = jnp.broadcast_to(drow, (hp, hp)).T
                s0 = s0 * dcol[0:GLA_DK, :] + u_ref[dirn, chunk, 0]
                s1 = s1 * dcol[GLA_DK:hp, :] + u_ref[dirn, chunk, 1]
            if with_output and dirn == 1:
                gate_block(r0)
            return s0, s1

        s0, s1 = lax.fori_loop(0, n_blk, seq_step, (s0_ref[0, dirn, 0], s0_ref[0, dirn, 1]), unroll=min(8 if dirn == 0 else 4, n_blk))
        sfin_ref[0, dirn, 0] = s0
        sfin_ref[0, dirn, 1] = s1


def _gla(zq, zk, zv, zog, zdec, l, wd_p, bd_p, g_gla, s0, with_output):
    bsz, _, t, _ = zq.shape
    hp, vp = 2 * GLA_DK, 2 * GLA_DV
    n_chunks = t // GLA_CHUNK
    seq = lambda wd: pl.BlockSpec((1, None, t, wd), lambda b, p: (b, p, 0, 0))
    st_spec = pl.BlockSpec((1, 2, 2, GLA_DK, GLA_DV), lambda b, p: (b, 0, p, 0, 0))
    out_shape = [jax.ShapeDtypeStruct((bsz, 2, GLA_HEADS, GLA_DK, GLA_DV), F32)]
    out_specs = [st_spec]
    rows = min(t, GLA_BLOCK)
    nr = GLA_RINGS
    scratch = [pltpu.VMEM((2, n_chunks, 2, GLA_DK, GLA_DV), F32), pltpu.VMEM((2, n_chunks, SUBLANES, hp), F32)]
    staging = [pltpu.VMEM((rows, vp), BF16) for _ in range(nr)] + [pltpu.VMEM((2, rows, hp), BF16) for _ in range(nr)]
    if with_output:
        out_shape = [jax.ShapeDtypeStruct((bsz, 2, t, vp), BF16)] + out_shape
        out_specs = [seq(vp)] + out_specs
        scratch = scratch + [pltpu.VMEM((2, t, hp), BF16), pltpu.VMEM((t, vp), F32)]
        staging = staging + [pltpu.VMEM((2, rows, hp), BF16) for _ in range(nr)]
        staging = staging + [pltpu.VMEM((2, hp, rows), BF16) for _ in range(nr)]
    scratch = scratch + staging
    outs = pl.pallas_call(
        functools.partial(_gla_body, t=t, with_output=with_output),
        out_shape=out_shape,
        grid=(bsz, 2),
        in_specs=[
            seq(hp), seq(hp), seq(vp), seq(vp),
            pl.BlockSpec((1, t, DEC_PAD), lambda b, p: (b, 0, 0)),
            pl.BlockSpec((None, 1, DEC_PAD, 2 * hp), lambda b, p: (l, p, 0, 0)),
            pl.BlockSpec((None, 1, 1, 2 * hp), lambda b, p: (l, p, 0, 0)),
            _layer_spec(g_gla, l),
            st_spec,
        ],
        out_specs=out_specs,
        scratch_shapes=scratch,
        compiler_params=pltpu.CompilerParams(vmem_limit_bytes=VMEM_LIMIT),
        name="gla",
    )(zq, zk, zv, zog, zdec, wd_p, bd_p, g_gla, s0)
    if with_output:
        return outs[0], outs[1]
    return None, outs[0]


RG_BLOCK = 256


def _sigmoid(x):
    return 0.5 * jnp.tanh(0.5 * x) + 0.5


def _sigmoid_of_twice(x):
    return 0.5 * jnp.tanh(x) + 0.5


def _rg_body(rx_ref, rg_ref, wc_ref, bc_ref, wg_ref, bg_ref, lam_ref, h0_ref, *rest, t, line, with_output):
    if with_output:
        y_ref, hfin_ref, hf_ref, ab_ref, bb_ref = rest
    else:
        hfin_ref, hf_ref, ab_ref, bb_ref = rest
    w = RG_W
    rows = min(t, RG_BLOCK)
    n_blk = t // rows
    grp = SUBLANES
    gpb = rows // grp
    pos = lax.broadcasted_iota(jnp.int32, (rows, w), 0) & (line - 1)
    sub = lax.broadcasted_iota(jnp.int32, (rows // grp, grp, w), 1)
    wc = wc_ref[...]
    bc = bc_ref[...]
    wg = wg_ref[...]
    bg = bg_ref[...]
    sp = [jax.nn.softplus(-lam_ref[d:d + 1, :]) * (-RG_C) for d in range(2)]

    def pre_step(blk, hf):
        r0 = pl.multiple_of(blk * rows, rows)
        rs = pl.ds(r0, rows)
        x = rx_ref[0, rs, :]
        xm1 = jnp.where(pos >= 1, pltpu.roll(x, 1, 0), 0.0)
        xp1 = jnp.where(pos <= line - 2, pltpu.roll(x, rows - 1, 0), 0.0)
        xp2 = jnp.where(pos <= line - 3, pltpu.roll(x, rows - 2, 0), 0.0)
        u = bc + xm1 * wc[0:1, :] + x * wc[1:2, :] + xp1 * wc[2:3, :] + xp2 * wc[3:4, :]
        gates = jnp.dot(u.astype(BF16), wg, preferred_element_type=F32) + bg
        coef = []
        for dirn in range(2):
            r = _sigmoid_of_twice(gates[:, (2 * dirn) * w:(2 * dirn + 1) * w])
            i = _sigmoid_of_twice(gates[:, (2 * dirn + 1) * w:(2 * dirn + 2) * w])
            log_a = r * sp[dirn]
            a = jnp.exp(log_a)
            bx = jnp.sqrt(-jnp.tanh(log_a) * (a * a + 1.0)) * (i * u)
            a3 = a.reshape(rows // grp, grp, w)
            b3 = bx.reshape(rows // grp, grp, w)
            for s in (1, 2, 4):
                if dirn == 0:
                    ok = sub >= s
                    sh = s
                else:
                    ok = sub <= grp - 1 - s
                    sh = grp - s
                b3 = jnp.where(ok, a3 * pltpu.roll(b3, sh, 1) + b3, b3)
                a3 = jnp.where(ok, a3 * pltpu.roll(a3, sh, 1), a3)
            coef.append((a3.reshape(rows, w), b3.reshape(rows, w)))
        ab_ref[rs, :] = coef[1][0]
        bb_ref[rs, :] = coef[1][1]
        af, bf = coef[0]
        for gi in range(gpb):
            g0 = gi * grp
            hfull = bf[g0:g0 + grp, :] + af[g0:g0 + grp, :] * hf
            hf_ref[pl.ds(r0 + g0, grp), :] = hfull
            hf = hfull[grp - 1:grp, :]
        return hf

    hf = lax.fori_loop(0, n_blk, pre_step, h0_ref[0, 0:1, :])

    def bwd_step(j, hb):
        blk = n_blk - 1 - j
        r0 = pl.multiple_of(blk * rows, rows)
        parts = []
        for gi in range(gpb - 1, -1, -1):
            g0 = r0 + gi * grp
            hbull = bb_ref[pl.ds(g0, grp), :] + ab_ref[pl.ds(g0, grp), :] * hb
            hb = hbull[0:1, :]
            parts.append(hbull)
        if with_output:
            hbwd = jnp.concatenate(parts[::-1], axis=0)
            rs = pl.ds(r0, rows)
            h = hf_ref[rs, :] + hbwd
            y_ref[0, rs, :] = (h * jax.nn.gelu(rg_ref[0, rs, :])).astype(y_ref.dtype)
        return hb

    hb = lax.fori_loop(0, n_blk, bwd_step, h0_ref[0, 1:2, :])
    hfin_ref[0, 0:1, :] = hf
    hfin_ref[0, 1:2, :] = hb
    hfin_ref[0, 2:grp, :] = jnp.zeros((grp - 2, w), F32)


def _rg(zrx, zrg, l, w_conv, b_conv, w_gate, b_gate, lam, h0, line, with_output):
    bsz, t, w = zrx.shape
    seq = pl.BlockSpec((1, t, w), lambda b: (b, 0, 0))
    st_spec = pl.BlockSpec((1, SUBLANES, w), lambda b: (b, 0, 0))
    out_shape = [jax.ShapeDtypeStruct((bsz, SUBLANES, w), F32)]
    out_specs = [st_spec]
    if with_output:
        out_shape = [jax.ShapeDtypeStruct((bsz, t, w), BF16)] + out_shape
        out_specs = [seq] + out_specs
    outs = pl.pallas_call(
        functools.partial(_rg_body, t=t, line=line, with_output=with_output),
        out_shape=out_shape,
        grid=(bsz,),
        in_specs=[
            seq, seq,
            _layer_spec(w_conv, l), _layer_spec(b_conv, l),
            _layer_spec(w_gate, l), _layer_spec(b_gate, l),
            _layer_spec(lam, l),
            st_spec,
        ],
        out_specs=out_specs,
        scratch_shapes=[pltpu.VMEM((t, w), F32) for _ in range(3)],
        compiler_params=pltpu.CompilerParams(vmem_limit_bytes=VMEM_LIMIT),
        name="rglru",
    )(zrx, zrg, w_conv, b_conv, w_gate, b_gate, lam, h0)
    if with_output:
        return outs[0], outs[1]
    return None, outs[0]


FFT_PITCH = 72


def _dft_mats(n):
    j = np.arange(n)
    ang = 2.0 * np.pi * ((j[:, None] * j[None, :]) % n) / n
    return np.cos(ang), np.sin(ang)


def _fourier_consts(t):
    n2 = GRID_W
    n1 = t // n2
    cc, sc = _dft_mats(FNET_GD)
    eye2 = np.eye(LANES // FNET_GD)
    m3 = np.concatenate([np.kron(eye2, cc), -np.kron(eye2, sc)], axis=0) / 8.0
    c2, s2 = _dft_mats(n2)
    consts = {"m3": jnp.asarray(m3, F32)}
    if n1 == n2:
        consts["cs"] = jnp.asarray(np.concatenate([c2, s2], axis=0) / 8.0, F32)
        consts["m2"] = jnp.asarray(np.block([[c2, -s2], [s2, c2]]) / 8.0, F32)
        wk = (np.arange(n2)[:, None] * np.arange(n1)[None, :]).reshape(-1)
        ang = 2.0 * np.pi * wk / t
        consts["twc"] = jnp.asarray(np.repeat(np.cos(ang)[:, None], LANES, axis=1), F32)
        consts["tws"] = jnp.asarray(np.repeat(np.sin(ang)[:, None], LANES, axis=1), F32)
    else:
        ct, st = _dft_mats(t)
        consts["cs"] = jnp.asarray(np.concatenate([ct, st], axis=0) / np.sqrt(t), F32)
    return consts


def _fft_body(xa_ref, xb_ref, cs_ref, m2_ref, m3_ref, twc_ref, tws_ref, y_ref, *scr, t):
    n = GRID_W
    p = FFT_PITCH
    cs = cs_ref[...].astype(BF16)
    m2 = m2_ref[...].astype(BF16)
    m3 = m3_ref[...].astype(BF16)
    halves = ((xa_ref,) + tuple(scr[0:4]), (xb_ref,) + tuple(scr[4:8]))

    def stage1(h, i):
        x_ref, sr_ref, si_ref, _, _ = halves[h]
        w0 = 2 * i
        xw = jnp.concatenate([x_ref[0, pl.ds(w0 + j, n, stride=p), :] for j in range(2)], axis=1)
        a = jnp.dot(cs, xw.astype(BF16), preferred_element_type=F32)
        for j in range(2):
            ar = a[0:n, j * LANES:(j + 1) * LANES]
            ai = a[n:2 * n, j * LANES:(j + 1) * LANES]
            r0 = pl.multiple_of((w0 + j) * n, n)
            tc = twc_ref[pl.ds(r0, n), :]
            ts = tws_ref[pl.ds(r0, n), :]
            sr_ref[pl.ds(w0 + j, n, stride=p), :] = ar * tc - ai * ts
            si_ref[pl.ds(w0 + j, n, stride=p), :] = ar * ts + ai * tc

    def stage2(h, i):
        _, sr_ref, si_ref, dr_ref, di_ref = halves[h]
        k0 = 2 * i
        cols = []
        for j in range(2):
            r0 = pl.multiple_of((k0 + j) * p, SUBLANES)
            cols.append(jnp.concatenate([sr_ref[pl.ds(r0, n), :], si_ref[pl.ds(r0, n), :]], axis=0))
        xb = jnp.concatenate(cols, axis=1).astype(BF16)
        d = jnp.dot(m2, xb, preferred_element_type=F32)
        for j in range(2):
            dr_ref[pl.ds(k0 + j, n, stride=p), :] = d[0:n, j * LANES:(j + 1) * LANES]
            di_ref[pl.ds(k0 + j, n, stride=p), :] = d[n:2 * n, j * LANES:(j + 1) * LANES]

    per = 8

    def stage3(h, i):
        _, _, _, dr_ref, di_ref = halves[h]
        blocks = []
        for j in range(per):
            r0 = pl.multiple_of((i * per + j) * p, SUBLANES)
            blocks.append(jnp.concatenate([dr_ref[pl.ds(r0, n), :], di_ref[pl.ds(r0, n), :]], axis=1))
        d = jnp.concatenate(blocks, axis=0).astype(BF16)
        rs = pl.ds(pl.multiple_of(i * per * n, per * n), per * n)
        y_ref[0, rs, h * LANES:(h + 1) * LANES] = jnp.dot(d, m3, preferred_element_type=F32).astype(y_ref.dtype)

    def loop(trips, unroll, *stages):
        def body(i, carry):
            for fn in stages:
                fn(i)
            return carry
        lax.fori_loop(0, trips, body, 0, unroll=unroll)

    ratio = (n // 2) // (n // per)
    loop(n // 2, 8, lambda i: stage1(0, i))
    loop(n // 2, 8, lambda i: stage2(0, i), lambda i: stage1(1, i))
    loop(n // per, 2, lambda i: stage3(0, i), lambda i: [stage2(1, ratio * i + k) for k in range(ratio)])
    loop(n // per, 2, lambda i: stage3(1, i))


def _dense_dft_body(x_ref, cs_ref, m3_ref, y_ref, *, t):
    a = jnp.dot(cs_ref[...].astype(BF16), x_ref[0].astype(BF16), preferred_element_type=F32)
    d = jnp.concatenate([a[0:t], a[t:2 * t]], axis=1).astype(BF16)
    y_ref[0] = jnp.dot(d, m3_ref[...].astype(BF16), preferred_element_type=F32).astype(y_ref.dtype)


def _fourier(zf, consts, t):
    bsz, nh, t_in, _ = zf.shape
    w = nh * LANES
    seq = pl.BlockSpec((1, t, LANES), lambda b, j: (b, 0, j))
    seq_in = pl.BlockSpec((1, None, t_in, LANES), lambda b, j: (b, j, 0, 0))
    common = dict(
        out_shape=jax.ShapeDtypeStruct((bsz, t, w), BF16),
        grid=(bsz, w // LANES),
        out_specs=seq,
        compiler_params=pltpu.CompilerParams(vmem_limit_bytes=VMEM_LIMIT),
    )
    if "m2" in consts:
        full = pl.BlockSpec((1, t, w), lambda b: (b, 0, 0))
        half = lambda j: pl.BlockSpec((1, None, t_in, LANES), lambda b: (b, j, 0, 0))
        return pl.pallas_call(
            functools.partial(_fft_body, t=t),
            out_shape=jax.ShapeDtypeStruct((bsz, t, w), BF16),
            grid=(bsz,),
            in_specs=[half(0), half(1), _const_spec(consts["cs"].shape), _const_spec(consts["m2"].shape),
                      _const_spec(consts["m3"].shape), _const_spec((t, LANES)), _const_spec((t, LANES))],
            out_specs=full,
            scratch_shapes=[pltpu.VMEM((GRID_W * FFT_PITCH, LANES), F32) for _ in range(8)],
            compiler_params=pltpu.CompilerParams(vmem_limit_bytes=VMEM_LIMIT),
            name="fourier_fft",
        )(zf, zf, consts["cs"], consts["m2"], consts["m3"], consts["twc"], consts["tws"])
    return pl.pallas_call(
        functools.partial(_dense_dft_body, t=t),
        in_specs=[seq_in, _const_spec(consts["cs"].shape), _const_spec(consts["m3"].shape)],
        name="fourier_dense", **common,
    )(zf, consts["cs"], consts["m3"])


FF_PIECES = (1536, 1280)


def _outffn_body(yg_ref, yf_ref, yr_ref, x_ref, mod_ref, gpm_ref, gpf_ref, gqf_ref,
                 wo_ref, wg_ref, wu_ref, wd_ref, o_ref):
    tm = x_ref.shape[1]
    sub = tm // ROW_SPLIT
    subs = [slice(r * sub, (r + 1) * sub) for r in range(ROW_SPLIT)]
    mixes = []
    for rs in subs:
        y = jnp.concatenate([yg_ref[0, 0, rs, :], yg_ref[0, 1, rs, :], yf_ref[0, rs, :], yr_ref[0, rs, :]], axis=1)
        mixes.append(jnp.dot(y, wo_ref[...], preferred_element_type=F32))
    x1s, hfs = [], []
    for rs, mix in zip(subs, mixes):
        x1 = x_ref[0, rs, :] + mod_ref[0, 2:3, :] * _rms(mix, gpm_ref[...])
        x1s.append(x1)
        hfs.append((_rms(x1, gpf_ref[...]) * (1.0 + mod_ref[0, 4:5, :]) + mod_ref[0, 3:4, :]).astype(BF16))
    for rs, x1, hf in zip(subs, x1s, hfs):
        acc = None
        off = 0
        for fw in FF_PIECES:
            gate = jnp.dot(hf, wg_ref[:, off:off + fw], preferred_element_type=F32)
            up = jnp.dot(hf, wu_ref[:, off:off + fw], preferred_element_type=F32)
            act = (gate * jax.nn.sigmoid(gate) * up).astype(BF16)
            part = jnp.dot(act, wd_ref[off:off + fw, :], preferred_element_type=F32)
            acc = part if acc is None else acc + part
            off += fw
        o_ref[0, rs, :] = x1 + mod_ref[0, 5:6, :] * _rms(acc, gqf_ref[...])


def _outffn(yg, yf, yr, x, mod, mod_row, l, gpm, gpf, gqf, wo, wg, wu, wd, tm):
    bsz, t, d = x.shape
    if mod_row is None:
        mod_map = lambda b, i: (l, b, 0, 0)
    else:
        mod_map = lambda b, i: (l, mod_row, 0, 0)
    tile = lambda wdt: pl.BlockSpec((1, tm, wdt), lambda b, i: (b, i, 0))
    return pl.pallas_call(
        _outffn_body,
        out_shape=jax.ShapeDtypeStruct((bsz, t, d), F32),
        grid=(bsz, t // tm),
        in_specs=[
            pl.BlockSpec((1, 2, tm, GLA_V // 2), lambda b, i: (b, 0, i, 0)), tile(FNET_W), tile(RG_W), tile(d),
            pl.BlockSpec((None, 1, 6, d), mod_map),
            _layer_spec(gpm, l), _layer_spec(gpf, l), _layer_spec(gqf, l),
            _layer_spec(wo, l), _layer_spec(wg, l), _layer_spec(wu, l), _layer_spec(wd, l),
        ],
        out_specs=tile(d),
        compiler_params=pltpu.CompilerParams(vmem_limit_bytes=VMEM_LIMIT),
        name="outffn",
    )(yg, yf, yr, x, mod, gpm, gpf, gqf, wo, wg, wu, wd)


def _prep_w_in(w_in):
    off_dec = 2 * GLA_QK + GLA_V
    off_og = off_dec + 2 * GLA_RANK
    pad = jnp.zeros(w_in.shape[:2] + (DEC_PAD - 2 * GLA_RANK,), w_in.dtype)
    q_scale = GLA_DK ** -0.5
    return jnp.concatenate([w_in[..., 0:GLA_QK] * q_scale, w_in[..., GLA_QK:off_dec], w_in[..., off_og:],
                            w_in[..., off_dec:off_og], pad], axis=-1).astype(BF16)


def _prep_dec(w_dec, b_dec):
    depth = w_dec.shape[0]
    hp = 2 * GLA_DK
    wr = w_dec.reshape(depth, 2, GLA_RANK, 2, hp)
    wbd = jnp.einsum("ldrpj,de->lpdrej", wr, jnp.eye(2, dtype=w_dec.dtype))
    wbd = wbd.reshape(depth, 2, 2 * GLA_RANK, 2 * hp)
    wbd = jnp.pad(wbd, ((0, 0), (0, 0), (0, DEC_PAD - 2 * GLA_RANK), (0, 0)))
    bd = b_dec.reshape(depth, 2, 2, hp).transpose(0, 2, 1, 3).reshape(depth, 2, 1, 2 * hp)
    return wbd.astype(BF16), bd


def _block_diag(w):
    h, hd = w.shape[-3], w.shape[-2]
    out = jnp.einsum("...hij,hg->...higj", w, jnp.eye(h, dtype=w.dtype))
    return out.reshape(w.shape[:-3] + (h * hd, h * hd))


def _prep_rg(w_a, b_a, w_x, b_x):
    bda, bdx = _block_diag(w_a), _block_diag(w_x)
    wg = jnp.concatenate([bda[:, 0], bdx[:, 0], bda[:, 1], bdx[:, 1]], axis=-1)
    bg = jnp.concatenate([b_a[:, 0], b_x[:, 0], b_a[:, 1], b_x[:, 1]], axis=-1)[:, None, :]
    return (0.5 * wg).astype(BF16), 0.5 * bg


def _mixer(h, mod, mod_row, l, lw, consts, s0, h0, line, tm, with_output):
    t = h.shape[1]
    f_pitch = FFT_PITCH if "m2" in consts else GRID_W
    zq, zk, zv, zog, zf, zrx, zrg, zdec = _inproj(h, mod, mod_row, l, lw["g_pre_mix"], lw["w_in"], tm, f_pitch)
    yg, s_fin = _gla(zq, zk, zv, zog, zdec, l, lw["wd"], lw["bd"], lw["g_gla"], s0, with_output)
    yr, h_fin = _rg(zrx, zrg, l, lw["w_conv"], lw["b_conv"], lw["w_gate"], lw["b_gate"], lw["lam"], h0, line, with_output)
    yf = _fourier(zf, consts, t) if with_output else None
    return (yg, yf, yr), s_fin, h_fin


def kernel(x, c, ctx, c_ctx, w_ada, b_ada, g_pre_mix, g_post_mix, g_pre_ffn, g_post_ffn, w_in, w_dec, b_dec, g_gla,
           w_conv, b_conv, w_rg_a, b_rg_a, w_rg_x, b_rg_x, rg_lam, w_out, w_ffn_gate, w_ffn_up, w_ffn_down):
    bsz, t, d = x.shape
    t_ctx = ctx.shape[1]
    depth = w_ada.shape[0]

    c_all = jnp.concatenate([c, c_ctx[None, :], jnp.zeros((MOD_ROWS - bsz - 1, d), F32)], axis=0)
    mod = _ada(c_all, w_ada, b_ada).reshape(depth, MOD_ROWS, 6, d)

    consts_lat = _fourier_consts(t)
    consts_ctx = _fourier_consts(t_ctx)
    s_zero = jnp.zeros((bsz, 2, GLA_HEADS, GLA_DK, GLA_DV), F32)
    h_zero = jnp.zeros((bsz, SUBLANES, RG_W), F32)
    rows = lambda v: v[:, None, :]

    wd_p, bd_p = _prep_dec(w_dec, b_dec)
    w_gate, b_gate = _prep_rg(w_rg_a, b_rg_a, w_rg_x, b_rg_x)
    lw = dict(g_pre_mix=rows(g_pre_mix), w_in=_prep_w_in(w_in), wd=wd_p, bd=bd_p, g_gla=rows(g_gla),
              w_conv=w_conv, b_conv=rows(b_conv), w_gate=w_gate, b_gate=b_gate, lam=rg_lam)
    ffn = (rows(g_post_mix), rows(g_pre_ffn), rows(g_post_ffn), w_out.astype(BF16),
           w_ffn_gate.astype(BF16), w_ffn_up.astype(BF16), w_ffn_down.astype(BF16))

    h_ctx = ctx
    for l in range(depth):
        last = l == depth - 1
        ys_c, s_ctx, hs_ctx = _mixer(h_ctx, mod, bsz, l, lw, consts_ctx, s_zero, h_zero, t_ctx, t_ctx, not last)
        ys_l, _, _ = _mixer(x, mod, None, l, lw, consts_lat, s_ctx, hs_ctx, GRID_W, 512, True)
        x = _outffn(*ys_l, x, mod, None, l, *ffn, 512)
        if not last:
            h_ctx = _outffn(*ys_c, h_ctx, mod, bsz, l, *ffn, t_ctx)
    return x
```

```python
import functools

import jax
import jax.numpy as jnp
import numpy as np
from jax import lax
from jax.experimental import pallas as pl
from jax.experimental.pallas import tpu as pltpu

F32 = jnp.float32
BF16 = jnp.bfloat16

D_MODEL = 1024
DEPTH = 4
GRID_W = 64
EPS = 1e-6

GLA_HEADS = 4
GLA_DK = 64
GLA_DV = 128
GLA_QK = GLA_HEADS * GLA_DK
GLA_V = GLA_HEADS * GLA_DV
GLA_RANK = 16
GLA_TAU = 16.0
GLA_CHUNK = 64
FNET_W = 256
FNET_GD = 64
RG_HEADS = 4
RG_HD = 64
RG_W = 256
RG_C = 8.0
CONV_W = 4
D_FF = 2816

LANES = 128
SUBLANES = 8
DEC_PAD = LANES
N_IN_PAD = 2 * GLA_QK + 2 * GLA_V + FNET_W + 2 * RG_W + DEC_PAD
MOD_ROWS = 16
VMEM_LIMIT = 56 * 1024 * 1024

_NT = (((1,), (1,)), ((), ()))
_TN = (((0,), (0,)), ((), ()))


def _rms(x, g):
    return x * lax.rsqrt(jnp.mean(x * x, axis=-1, keepdims=True) + EPS) * g


def _const_spec(shape):
    nd = len(shape)
    return pl.BlockSpec(shape, lambda *_: (0,) * nd, pipeline_mode=pl.Buffered(1))


def _layer_spec(arr, l):
    nd = arr.ndim - 1
    return pl.BlockSpec((None,) + arr.shape[1:], lambda *_: (l,) + (0,) * nd, pipeline_mode=pl.Buffered(1))


def _ada_body(c_ref, w_ref, b_ref, o_ref):
    c = c_ref[...]
    s = (c * jax.nn.sigmoid(c)).astype(BF16)
    o_ref[0] = jnp.dot(s, w_ref[0].astype(BF16), preferred_element_type=F32) + b_ref[0]


def _ada(c_all, w_ada, b_ada):
    depth, d, n = w_ada.shape
    tn = 1536
    return pl.pallas_call(
        _ada_body,
        out_shape=jax.ShapeDtypeStruct((depth, MOD_ROWS, n), F32),
        grid=(depth, n // tn),
        in_specs=[
            pl.BlockSpec((MOD_ROWS, d), lambda l, j: (0, 0)),
            pl.BlockSpec((1, d, tn), lambda l, j: (l, 0, j)),
            pl.BlockSpec((1, 1, tn), lambda l, j: (l, 0, j)),
        ],
        out_specs=pl.BlockSpec((1, MOD_ROWS, tn), lambda l, j: (l, 0, j)),
        compiler_params=pltpu.CompilerParams(vmem_limit_bytes=VMEM_LIMIT),
        name="ada",
    )(c_all, w_ada, b_ada.reshape(depth, 1, n))


ROW_SPLIT = 2

_IN_WIDTHS = (GLA_QK, GLA_QK, GLA_V, GLA_V, FNET_W, RG_W, RG_W, DEC_PAD)
_IN_F = 4


def _inproj_body(x_ref, mod_ref, g_ref, w_ref, *out_refs, f_pitch):
    tm = x_ref.shape[1]
    sub = tm // ROW_SPLIT
    for r in range(ROW_SPLIT):
        rs = slice(r * sub, (r + 1) * sub)
        h = _rms(x_ref[0, rs, :], g_ref[...]) * (1.0 + mod_ref[0, 1:2, :]) + mod_ref[0, 0:1, :]
        h = h.astype(BF16)
        off = 0
        for idx, (o_ref, wd) in enumerate(zip(out_refs, _IN_WIDTHS)):
            z = jnp.dot(h, w_ref[:, off:off + wd], preferred_element_type=F32)
            off += wd
            if idx == _IN_F and f_pitch != GRID_W:
                pad = jnp.zeros((f_pitch - GRID_W, wd), F32)
                for ln in range(sub // GRID_W):
                    base = (r * (sub // GRID_W) + ln) * f_pitch
                    o_ref[0, base:base + GRID_W, :] = z[ln * GRID_W:(ln + 1) * GRID_W, :]
                    o_ref[0, base + GRID_W:base + f_pitch, :] = pad
            else:
                o_ref[0, rs, :] = z


def _inproj(x, mod, mod_row, l, g, w_in_p, tm, f_pitch):
    bsz, t, d = x.shape
    if mod_row is None:
        mod_map = lambda b, i: (l, b, 0, 0)
    else:
        mod_map = lambda b, i: (l, mod_row, 0, 0)
    rows_of = lambda idx, n: n // GRID_W * f_pitch if idx == _IN_F else n
    return pl.pallas_call(
        functools.partial(_inproj_body, f_pitch=f_pitch),
        out_shape=[jax.ShapeDtypeStruct((bsz, rows_of(i, t), wd), F32) for i, wd in enumerate(_IN_WIDTHS)],
        grid=(bsz, t // tm),
        in_specs=[
            pl.BlockSpec((1, tm, d), lambda b, i: (b, i, 0)),
            pl.BlockSpec((None, 1, 6, d), mod_map),
            _layer_spec(g, l),
            _layer_spec(w_in_p, l),
        ],
        out_specs=[pl.BlockSpec((1, rows_of(i, tm), wd), lambda b, i: (b, i, 0)) for i, wd in enumerate(_IN_WIDTHS)],
        compiler_params=pltpu.CompilerParams(vmem_limit_bytes=VMEM_LIMIT),
        name="inproj",
    )(x, mod, g, w_in_p)


GLA_BLOCK = 256
GLA_RINGS = 8


def _chunk_cumsum(x, reverse):
    rows, lanes = x.shape
    g = SUBLANES
    c = GLA_CHUNK
    x3 = x.reshape(rows // g, g, lanes)
    sub = lax.broadcasted_iota(jnp.int32, x3.shape, 1)
    for s in (1, 2, 4):
        if reverse:
            x3 = x3 + jnp.where(sub < g - s, pltpu.roll(x3, g - s, 1), 0.0)
        else:
            x3 = x3 + jnp.where(sub >= s, pltpu.roll(x3, s, 1), 0.0)
    edge = x3[:, 0:1, :] if reverse else x3[:, g - 1:g, :]
    tot = jnp.broadcast_to(edge, x3.shape).reshape(rows, lanes)
    pos = lax.broadcasted_iota(jnp.int32, (rows, lanes), 0) & (c - 1)
    acc = tot
    for s in (8, 16, 32):
        if reverse:
            acc = acc + jnp.where(pos < c - s, pltpu.roll(acc, rows - s, 0), 0.0)
        else:
            acc = acc + jnp.where(pos >= s, pltpu.roll(acc, s, 0), 0.0)
    sums = x3.reshape(rows, lanes) + (acc - tot)
    parts = []
    for ci in range(rows // c):
        r = ci * c if reverse else ci * c + c - 1
        parts.append(jnp.broadcast_to(acc[r:r + 1, :], (c, lanes)))
    total = parts[0] if len(parts) == 1 else jnp.concatenate(parts, axis=0)
    return sums, total


def _gla_body(q_ref, k_ref, v_ref, og_ref, dec_ref, wd_ref, bd_ref, g_ref, s0_ref, *rest, t, with_output):
    nr = GLA_RINGS
    if with_output:
        y_ref, sfin_ref, u_ref, dl_ref, qt_ref, o_ref = rest[:6]
        bufs = rest[6:]
        rings = tuple((bufs[r], bufs[nr + r], bufs[2 * nr + r], bufs[3 * nr + r]) for r in range(nr))
    else:
        sfin_ref, u_ref, dl_ref = rest[:3]
        bufs = rest[3:]
        rings = tuple((bufs[r], bufs[nr + r], None, None) for r in range(nr))
    c = GLA_CHUNK
    hp = 2 * GLA_DK
    vp = 2 * GLA_DV
    rows = min(t, GLA_BLOCK)
    cpb = rows // c
    n_blk = t // rows

    dk_head0 = lax.broadcasted_iota(jnp.int32, (hp, 2 * c), 0) < GLA_DK
    ar = lax.broadcasted_iota(jnp.int32, (2 * c, 4 * c), 0)
    ac = lax.broadcasted_iota(jnp.int32, (2 * c, 4 * c), 1)
    same_chunk = (ar // c) == ((ac // c) & 1)
    keep_f = same_chunk & ((ar & (c - 1)) >= (ac & (c - 1)))
    keep_b = same_chunk & ((ar & (c - 1)) <= (ac & (c - 1)))
    wd = wd_ref[0]
    bd = bd_ref[0]

    def prepare(blk, ring):
        vb_ring, ke_ring, qt_ring, kt_ring = ring
        r0 = pl.multiple_of(blk * rows, rows)
        code = dec_ref[0, pl.ds(r0, rows), :].astype(BF16)
        logit = jnp.dot(code, wd, preferred_element_type=F32) + bd
        la = (jnp.minimum(logit, 0.0) - jnp.log(1.0 + jnp.exp(-jnp.abs(logit)))) * (1.0 / GLA_TAU)
        k = k_ref[0, pl.ds(r0, rows), :]
        vb_ring[...] = v_ref[0, pl.ds(r0, rows), :].astype(BF16)
        if with_output:
            q = q_ref[0, pl.ds(r0, rows), :]
        for dirn in range(2):
            b, b_all = _chunk_cumsum(la[:, dirn * hp:(dirn + 1) * hp], reverse=dirn == 1)
            ke_ring[dirn] = (k * jnp.exp(b_all - b)).astype(BF16)
            for ci in range(cpb):
                dl_ref[dirn, blk * cpb + ci] = jnp.exp(b_all[ci * c:ci * c + SUBLANES, :])
            if with_output:
                qt = (q * jnp.exp(b)).astype(BF16)
                qt_ref[dirn, pl.ds(r0, rows), :] = qt
                qt_ring[dirn] = qt
                kt_ring[dirn] = (k * jnp.exp(-b)).T.astype(BF16)

    def multiply(blk, ring):
        vb_ring, ke_ring, qt_ring, kt_ring = ring
        r0 = pl.multiple_of(blk * rows, rows)
        vb = vb_ring[...]
        for dirn in range(2):
            ke = ke_ring[dirn]
            for ci in range(cpb):
                sl = slice(ci * c, (ci + 1) * c)
                upd = lax.dot_general(ke[sl], vb[sl], _TN, preferred_element_type=F32)
                u_ref[dirn, blk * cpb + ci, 0] = upd[0:GLA_DK, 0:GLA_DV]
                u_ref[dirn, blk * cpb + ci, 1] = upd[GLA_DK:hp, GLA_DV:vp]
            if not with_output:
                continue
            qt = qt_ring[dirn]
            kt_t = kt_ring[dirn]
            zt = jnp.zeros((hp, 2 * c), BF16)
            zv = jnp.zeros((2 * c, GLA_DV), BF16)
            parts = []
            for cp in range(cpb // 2):
                sl = slice(cp * 2 * c, (cp + 1) * 2 * c)
                kt_cp = kt_t[:, sl]
                kbd = jnp.concatenate([jnp.where(dk_head0, kt_cp, zt), jnp.where(dk_head0, zt, kt_cp)], axis=1)
                sc = jnp.dot(qt[sl], kbd, preferred_element_type=F32)
                p = jnp.where(keep_f if dirn == 0 else keep_b, sc, 0.0).astype(BF16)
                v_cp = vb[sl]
                vbd = jnp.concatenate(
                    [jnp.concatenate([v_cp[:, 0:GLA_DV], zv], axis=1),
                     jnp.concatenate([zv, v_cp[:, GLA_DV:vp]], axis=1)], axis=0)
                parts.append(jnp.dot(p, vbd, preferred_element_type=F32))
            o_intra = parts[0] if len(parts) == 1 else jnp.concatenate(parts, axis=0)
            if dirn == 0:
                o_ref[pl.ds(r0, rows), :] = o_intra
            else:
                o_ref[pl.ds(r0, rows), :] += o_intra

    per = min(n_blk, nr)

    def pre_step(j, carry):
        i = per * j
        prepare(i, rings[0])
        for r in range(per):
            if r + 1 < per:
                prepare(i + r + 1, rings[r + 1])
            multiply(i + r, rings[r])
        return carry

    lax.fori_loop(0, n_blk // per, pre_step, 0)

    zs = jnp.zeros((GLA_DK, GLA_DV), BF16)

    def gate_block(r0):
        rs = pl.ds(r0, rows)
        o = o_ref[rs, :]
        og = og_ref[0, rs, :]
        g = g_ref[...]
        o0 = _rms(o[:, 0:GLA_DV], g)
        o1 = _rms(o[:, GLA_DV:vp], g)
        gate = og * _sigmoid(og)
        y_ref[0, rs, :] = (jnp.concatenate([o0, o1], axis=1) * gate).astype(y_ref.dtype)

    for dirn in range(2):
        def seq_step(i, carry, dirn=dirn):
            s0, s1 = carry
            blk = i if dirn == 0 else n_blk - 1 - i
            r0 = pl.multiple_of(blk * rows, rows)
            for cj in range(cpb):
                ci = cj if dirn == 0 else cpb - 1 - cj
                chunk = blk * cpb + ci
                if with_output:
                    rs = pl.ds(r0 + ci * c, c)
                    sbd = jnp.concatenate(
                        [jnp.concatenate([s0.astype(BF16), zs], axis=1),
                         jnp.concatenate([zs, s1.astype(BF16)], axis=1)], axis=0)
                    o_ref[rs, :] += jnp.dot(qt_ref[dirn, rs, :], sbd, preferred_element_type=F32)
                drow = dl_ref[dirn, chunk][0:1, :]
                dcol = jnp.broadcast_to(drow, (hp, hp)).T
                s0 = s0 * dcol[0:GLA_DK, :] + u_ref[dirn, chunk, 0]
                s1 = s1 * dcol[GLA_DK:hp, :] + u_ref[dirn, chunk, 1]
            if with_output and dirn == 1:
                gate_block(r0)
            return s0, s1

        s0, s1 = lax.fori_loop(0, n_blk, seq_step, (s0_ref[0, dirn, 0], s0_ref[0, dirn, 1]), unroll=min(8 if dirn == 0 else 4, n_blk))
        sfin_ref[0, dirn, 0] = s0
        sfin_ref[0, dirn, 1] = s1


def _gla(zq, zk, zv, zog, zdec, l, wd_p, bd_p, g_gla, s0, with_output):
    bsz, t, _ = zq.shape
    hp, vp = 2 * GLA_DK, 2 * GLA_DV
    n_chunks = t // GLA_CHUNK
    seq = lambda wd: pl.BlockSpec((1, t, wd), lambda b, p: (b, 0, p))
    st_spec = pl.BlockSpec((1, 2, 2, GLA_DK, GLA_DV), lambda b, p: (b, 0, p, 0, 0))
    out_shape = [jax.ShapeDtypeStruct((bsz, 2, GLA_HEADS, GLA_DK, GLA_DV), F32)]
    out_specs = [st_spec]
    rows = min(t, GLA_BLOCK)
    nr = GLA_RINGS
    scratch = [pltpu.VMEM((2, n_chunks, 2, GLA_DK, GLA_DV), F32), pltpu.VMEM((2, n_chunks, SUBLANES, hp), F32)]
    staging = [pltpu.VMEM((rows, vp), BF16) for _ in range(nr)] + [pltpu.VMEM((2, rows, hp), BF16) for _ in range(nr)]
    if with_output:
        out_shape = [jax.ShapeDtypeStruct((bsz, t, GLA_V), BF16)] + out_shape
        out_specs = [seq(vp)] + out_specs
        scratch = scratch + [pltpu.VMEM((2, t, hp), BF16), pltpu.VMEM((t, vp), F32)]
        staging = staging + [pltpu.VMEM((2, rows, hp), BF16) for _ in range(nr)]
        staging = staging + [pltpu.VMEM((2, hp, rows), BF16) for _ in range(nr)]
    scratch = scratch + staging
    outs = pl.pallas_call(
        functools.partial(_gla_body, t=t, with_output=with_output),
        out_shape=out_shape,
        grid=(bsz, 2),
        in_specs=[
            seq(hp), seq(hp), seq(vp), seq(vp),
            pl.BlockSpec((1, t, DEC_PAD), lambda b, p: (b, 0, 0)),
            pl.BlockSpec((None, 1, DEC_PAD, 2 * hp), lambda b, p: (l, p, 0, 0)),
            pl.BlockSpec((None, 1, 1, 2 * hp), lambda b, p: (l, p, 0, 0)),
            _layer_spec(g_gla, l),
            st_spec,
        ],
        out_specs=out_specs,
        scratch_shapes=scratch,
        compiler_params=pltpu.CompilerParams(vmem_limit_bytes=VMEM_LIMIT),
        name="gla",
    )(zq, zk, zv, zog, zdec, wd_p, bd_p, g_gla, s0)
    if with_output:
        return outs[0], outs[1]
    return None, outs[0]


RG_BLOCK = 256


def _sigmoid(x):
    return 0.5 * jnp.tanh(0.5 * x) + 0.5


def _sigmoid_of_twice(x):
    return 0.5 * jnp.tanh(x) + 0.5


def _rg_body(rx_ref, rg_ref, wc_ref, bc_ref, wg_ref, bg_ref, lam_ref, h0_ref, *rest, t, line, with_output):
    if with_output:
        y_ref, hfin_ref, hf_ref, ab_ref, bb_ref = rest
    else:
        hfin_ref, hf_ref, ab_ref, bb_ref = rest
    w = RG_W
    rows = min(t, RG_BLOCK)
    n_blk = t // rows
    grp = SUBLANES
    gpb = rows // grp
    pos = lax.broadcasted_iota(jnp.int32, (rows, w), 0) & (line - 1)
    sub = lax.broadcasted_iota(jnp.int32, (rows // grp, grp, w), 1)
    wc = wc_ref[...]
    bc = bc_ref[...]
    wg = wg_ref[...]
    bg = bg_ref[...]
    sp = [jax.nn.softplus(-lam_ref[d:d + 1, :]) * (-RG_C) for d in range(2)]

    def pre_step(blk, hf):
        r0 = pl.multiple_of(blk * rows, rows)
        rs = pl.ds(r0, rows)
        x = rx_ref[0, rs, :]
        xm1 = jnp.where(pos >= 1, pltpu.roll(x, 1, 0), 0.0)
        xp1 = jnp.where(pos <= line - 2, pltpu.roll(x, rows - 1, 0), 0.0)
        xp2 = jnp.where(pos <= line - 3, pltpu.roll(x, rows - 2, 0), 0.0)
        u = bc + xm1 * wc[0:1, :] + x * wc[1:2, :] + xp1 * wc[2:3, :] + xp2 * wc[3:4, :]
        gates = jnp.dot(u.astype(BF16), wg, preferred_element_type=F32) + bg
        coef = []
        for dirn in range(2):
            r = _sigmoid_of_twice(gates[:, (2 * dirn) * w:(2 * dirn + 1) * w])
            i = _sigmoid_of_twice(gates[:, (2 * dirn + 1) * w:(2 * dirn + 2) * w])
            log_a = r * sp[dirn]
            a = jnp.exp(log_a)
            bx = jnp.sqrt(-jnp.tanh(log_a) * (a * a + 1.0)) * (i * u)
            a3 = a.reshape(rows // grp, grp, w)
            b3 = bx.reshape(rows // grp, grp, w)
            for s in (1, 2, 4):
                if dirn == 0:
                    ok = sub >= s
                    sh = s
                else:
                    ok = sub <= grp - 1 - s
                    sh = grp - s
                b3 = jnp.where(ok, a3 * pltpu.roll(b3, sh, 1) + b3, b3)
                a3 = jnp.where(ok, a3 * pltpu.roll(a3, sh, 1), a3)
            coef.append((a3.reshape(rows, w), b3.reshape(rows, w)))
        ab_ref[rs, :] = coef[1][0]
        bb_ref[rs, :] = coef[1][1]
        af, bf = coef[0]
        for gi in range(gpb):
            g0 = gi * grp
            hfull = bf[g0:g0 + grp, :] + af[g0:g0 + grp, :] * hf
            hf_ref[pl.ds(r0 + g0, grp), :] = hfull
            hf = hfull[grp - 1:grp, :]
        return hf

    hf = lax.fori_loop(0, n_blk, pre_step, h0_ref[0, 0:1, :])

    def bwd_step(j, hb):
        blk = n_blk - 1 - j
        r0 = pl.multiple_of(blk * rows, rows)
        parts = []
        for gi in range(gpb - 1, -1, -1):
            g0 = r0 + gi * grp
            hbull = bb_ref[pl.ds(g0, grp), :] + ab_ref[pl.ds(g0, grp), :] * hb
            hb = hbull[0:1, :]
            parts.append(hbull)
        if with_output:
            hbwd = jnp.concatenate(parts[::-1], axis=0)
            rs = pl.ds(r0, rows)
            h = hf_ref[rs, :] + hbwd
            y_ref[0, rs, :] = (h * jax.nn.gelu(rg_ref[0, rs, :])).astype(y_ref.dtype)
        return hb

    hb = lax.fori_loop(0, n_blk, bwd_step, h0_ref[0, 1:2, :])
    hfin_ref[0, 0:1, :] = hf
    hfin_ref[0, 1:2, :] = hb
    hfin_ref[0, 2:grp, :] = jnp.zeros((grp - 2, w), F32)


def _rg(zrx, zrg, l, w_conv, b_conv, w_gate, b_gate, lam, h0, line, with_output):
    bsz, t, w = zrx.shape
    seq = pl.BlockSpec((1, t, w), lambda b: (b, 0, 0))
    st_spec = pl.BlockSpec((1, SUBLANES, w), lambda b: (b, 0, 0))
    out_shape = [jax.ShapeDtypeStruct((bsz, SUBLANES, w), F32)]
    out_specs = [st_spec]
    if with_output:
        out_shape = [jax.ShapeDtypeStruct((bsz, t, w), BF16)] + out_shape
        out_specs = [seq] + out_specs
    outs = pl.pallas_call(
        functools.partial(_rg_body, t=t, line=line, with_output=with_output),
        out_shape=out_shape,
        grid=(bsz,),
        in_specs=[
            seq, seq,
            _layer_spec(w_conv, l), _layer_spec(b_conv, l),
            _layer_spec(w_gate, l), _layer_spec(b_gate, l),
            _layer_spec(lam, l),
            st_spec,
        ],
        out_specs=out_specs,
        scratch_shapes=[pltpu.VMEM((t, w), F32) for _ in range(3)],
        compiler_params=pltpu.CompilerParams(vmem_limit_bytes=VMEM_LIMIT),
        name="rglru",
    )(zrx, zrg, w_conv, b_conv, w_gate, b_gate, lam, h0)
    if with_output:
        return outs[0], outs[1]
    return None, outs[0]


FFT_PITCH = 72


def _dft_mats(n):
    j = np.arange(n)
    ang = 2.0 * np.pi * ((j[:, None] * j[None, :]) % n) / n
    return np.cos(ang), np.sin(ang)


def _fourier_consts(t):
    n2 = GRID_W
    n1 = t // n2
    cc, sc = _dft_mats(FNET_GD)
    eye2 = np.eye(LANES // FNET_GD)
    m3 = np.concatenate([np.kron(eye2, cc), -np.kron(eye2, sc)], axis=0) / 8.0
    c2, s2 = _dft_mats(n2)
    consts = {"m3": jnp.asarray(m3, F32)}
    if n1 == n2:
        consts["cs"] = jnp.asarray(np.concatenate([c2, s2], axis=0) / 8.0, F32)
        consts["m2"] = jnp.asarray(np.block([[c2, -s2], [s2, c2]]) / 8.0, F32)
        wk = (np.arange(n2)[:, None] * np.arange(n1)[None, :]).reshape(-1)
        ang = 2.0 * np.pi * wk / t
        consts["twc"] = jnp.asarray(np.repeat(np.cos(ang)[:, None], LANES, axis=1), F32)
        consts["tws"] = jnp.asarray(np.repeat(np.sin(ang)[:, None], LANES, axis=1), F32)
    else:
        ct, st = _dft_mats(t)
        consts["cs"] = jnp.asarray(np.concatenate([ct, st], axis=0) / np.sqrt(t), F32)
    return consts


def _fft_body(xa_ref, xb_ref, cs_ref, m2_ref, m3_ref, twc_ref, tws_ref, y_ref, *scr, t):
    n = GRID_W
    p = FFT_PITCH
    cs = cs_ref[...].astype(BF16)
    m2 = m2_ref[...].astype(BF16)
    m3 = m3_ref[...].astype(BF16)
    halves = ((xa_ref,) + tuple(scr[0:4]), (xb_ref,) + tuple(scr[4:8]))

    def stage1(h, i):
        x_ref, sr_ref, si_ref, _, _ = halves[h]
        w0 = 2 * i
        xw = jnp.concatenate([x_ref[0, pl.ds(w0 + j, n, stride=p), :] for j in range(2)], axis=1)
        a = jnp.dot(cs, xw.astype(BF16), preferred_element_type=F32)
        for j in range(2):
            ar = a[0:n, j * LANES:(j + 1) * LANES]
            ai = a[n:2 * n, j * LANES:(j + 1) * LANES]
            r0 = pl.multiple_of((w0 + j) * n, n)
            tc = twc_ref[pl.ds(r0, n), :]
            ts = tws_ref[pl.ds(r0, n), :]
            sr_ref[pl.ds(w0 + j, n, stride=p), :] = ar * tc - ai * ts
            si_ref[pl.ds(w0 + j, n, stride=p), :] = ar * ts + ai * tc

    def stage2(h, i):
        _, sr_ref, si_ref, dr_ref, di_ref = halves[h]
        k0 = 2 * i
        cols = []
        for j in range(2):
            r0 = pl.multiple_of((k0 + j) * p, SUBLANES)
            cols.append(jnp.concatenate([sr_ref[pl.ds(r0, n), :], si_ref[pl.ds(r0, n), :]], axis=0))
        xb = jnp.concatenate(cols, axis=1).astype(BF16)
        d = jnp.dot(m2, xb, preferred_element_type=F32)
        for j in range(2):
            dr_ref[pl.ds(k0 + j, n, stride=p), :] = d[0:n, j * LANES:(j + 1) * LANES]
            di_ref[pl.ds(k0 + j, n, stride=p), :] = d[n:2 * n, j * LANES:(j + 1) * LANES]

    per = 8

    def stage3(h, i):
        _, _, _, dr_ref, di_ref = halves[h]
        blocks = []
        for j in range(per):
            r0 = pl.multiple_of((i * per + j) * p, SUBLANES)
            blocks.append(jnp.concatenate([dr_ref[pl.ds(r0, n), :], di_ref[pl.ds(r0, n), :]], axis=1))
        d = jnp.concatenate(blocks, axis=0).astype(BF16)
        rs = pl.ds(pl.multiple_of(i * per * n, per * n), per * n)
        y_ref[0, rs, h * LANES:(h + 1) * LANES] = jnp.dot(d, m3, preferred_element_type=F32).astype(y_ref.dtype)

    def loop(trips, unroll, *stages):
        def body(i, carry):
            for fn in stages:
                fn(i)
            return carry
        lax.fori_loop(0, trips, body, 0, unroll=unroll)

    ratio = (n // 2) // (n // per)
    loop(n // 2, 8, lambda i: stage1(0, i))
    loop(n // 2, 8, lambda i: stage2(0, i), lambda i: stage1(1, i))
    loop(n // per, 2, lambda i: stage3(0, i), lambda i: [stage2(1, ratio * i + k) for k in range(ratio)])
    loop(n // per, 2, lambda i: stage3(1, i))


def _dense_dft_body(x_ref, cs_ref, m3_ref, y_ref, *, t):
    a = jnp.dot(cs_ref[...].astype(BF16), x_ref[0].astype(BF16), preferred_element_type=F32)
    d = jnp.concatenate([a[0:t], a[t:2 * t]], axis=1).astype(BF16)
    y_ref[0] = jnp.dot(d, m3_ref[...].astype(BF16), preferred_element_type=F32).astype(y_ref.dtype)


def _fourier(zf, consts, t):
    bsz, t_in, w = zf.shape
    seq = pl.BlockSpec((1, t, LANES), lambda b, j: (b, 0, j))
    seq_in = pl.BlockSpec((1, t_in, LANES), lambda b, j: (b, 0, j))
    common = dict(
        out_shape=jax.ShapeDtypeStruct((bsz, t, w), BF16),
        grid=(bsz, w // LANES),
        out_specs=seq,
        compiler_params=pltpu.CompilerParams(vmem_limit_bytes=VMEM_LIMIT),
    )
    if "m2" in consts:
        full = pl.BlockSpec((1, t, w), lambda b: (b, 0, 0))
        half = lambda j: pl.BlockSpec((1, t_in, LANES), lambda b: (b, 0, j))
        return pl.pallas_call(
            functools.partial(_fft_body, t=t),
            out_shape=jax.ShapeDtypeStruct((bsz, t, w), BF16),
            grid=(bsz,),
            in_specs=[half(0), half(1), _const_spec(consts["cs"].shape), _const_spec(consts["m2"].shape),
                      _const_spec(consts["m3"].shape), _const_spec((t, LANES)), _const_spec((t, LANES))],
            out_specs=full,
            scratch_shapes=[pltpu.VMEM((GRID_W * FFT_PITCH, LANES), F32) for _ in range(8)],
            compiler_params=pltpu.CompilerParams(vmem_limit_bytes=VMEM_LIMIT),
            name="fourier_fft",
        )(zf, zf, consts["cs"], consts["m2"], consts["m3"], consts["twc"], consts["tws"])
    return pl.pallas_call(
        functools.partial(_dense_dft_body, t=t),
        in_specs=[seq_in, _const_spec(consts["cs"].shape), _const_spec(consts["m3"].shape)],
        name="fourier_dense", **common,
    )(zf, consts["cs"], consts["m3"])


FF_PIECES = (1536, 1280)


def _outffn_body(yg_ref, yf_ref, yr_ref, x_ref, mod_ref, gpm_ref, gpf_ref, gqf_ref,
                 wo_ref, wg_ref, wu_ref, wd_ref, o_ref):
    tm = x_ref.shape[1]
    sub = tm // ROW_SPLIT
    subs = [slice(r * sub, (r + 1) * sub) for r in range(ROW_SPLIT)]
    mixes = []
    for rs in subs:
        y = jnp.concatenate([yg_ref[0, rs, :], yf_ref[0, rs, :], yr_ref[0, rs, :]], axis=1)
        mixes.append(jnp.dot(y, wo_ref[...], preferred_element_type=F32))
    x1s, hfs = [], []
    for rs, mix in zip(subs, mixes):
        x1 = x_ref[0, rs, :] + mod_ref[0, 2:3, :] * _rms(mix, gpm_ref[...])
        x1s.append(x1)
        hfs.append((_rms(x1, gpf_ref[...]) * (1.0 + mod_ref[0, 4:5, :]) + mod_ref[0, 3:4, :]).astype(BF16))
    for rs, x1, hf in zip(subs, x1s, hfs):
        acc = None
        off = 0
        for fw in FF_PIECES:
            gate = jnp.dot(hf, wg_ref[:, off:off + fw], preferred_element_type=F32)
            up = jnp.dot(hf, wu_ref[:, off:off + fw], preferred_element_type=F32)
            act = (gate * jax.nn.sigmoid(gate) * up).astype(BF16)
            part = jnp.dot(act, wd_ref[off:off + fw, :], preferred_element_type=F32)
            acc = part if acc is None else acc + part
            off += fw
        o_ref[0, rs, :] = x1 + mod_ref[0, 5:6, :] * _rms(acc, gqf_ref[...])


def _outffn(yg, yf, yr, x, mod, mod_row, l, gpm, gpf, gqf, wo, wg, wu, wd, tm):
    bsz, t, d = x.shape
    if mod_row is None:
        mod_map = lambda b, i: (l, b, 0, 0)
    else:
        mod_map = lambda b, i: (l, mod_row, 0, 0)
    tile = lambda wdt: pl.BlockSpec((1, tm, wdt), lambda b, i: (b, i, 0))
    return pl.pallas_call(
        _outffn_body,
        out_shape=jax.ShapeDtypeStruct((bsz, t, d), F32),
        grid=(bsz, t // tm),
        in_specs=[
            tile(GLA_V), tile(FNET_W), tile(RG_W), tile(d),
            pl.BlockSpec((None, 1, 6, d), mod_map),
            _layer_spec(gpm, l), _layer_spec(gpf, l), _layer_spec(gqf, l),
            _layer_spec(wo, l), _layer_spec(wg, l), _layer_spec(wu, l), _layer_spec(wd, l),
        ],
        out_specs=tile(d),
        compiler_params=pltpu.CompilerParams(vmem_limit_bytes=VMEM_LIMIT),
        name="outffn",
    )(yg, yf, yr, x, mod, gpm, gpf, gqf, wo, wg, wu, wd)


def _prep_w_in(w_in):
    off_dec = 2 * GLA_QK + GLA_V
    off_og = off_dec + 2 * GLA_RANK
    pad = jnp.zeros(w_in.shape[:2] + (DEC_PAD - 2 * GLA_RANK,), w_in.dtype)
    q_scale = GLA_DK ** -0.5
    return jnp.concatenate([w_in[..., 0:GLA_QK] * q_scale, w_in[..., GLA_QK:off_dec], w_in[..., off_og:],
                            w_in[..., off_dec:off_og], pad], axis=-1).astype(BF16)


def _prep_dec(w_dec, b_dec):
    depth = w_dec.shape[0]
    hp = 2 * GLA_DK
    wr = w_dec.reshape(depth, 2, GLA_RANK, 2, hp)
    wbd = jnp.einsum("ldrpj,de->lpdrej", wr, jnp.eye(2, dtype=w_dec.dtype))
    wbd = wbd.reshape(depth, 2, 2 * GLA_RANK, 2 * hp)
    wbd = jnp.pad(wbd, ((0, 0), (0, 0), (0, DEC_PAD - 2 * GLA_RANK), (0, 0)))
    bd = b_dec.reshape(depth, 2, 2, hp).transpose(0, 2, 1, 3).reshape(depth, 2, 1, 2 * hp)
    return wbd.astype(BF16), bd


def _block_diag(w):
    h, hd = w.shape[-3], w.shape[-2]
    out = jnp.einsum("...hij,hg->...higj", w, jnp.eye(h, dtype=w.dtype))
    return out.reshape(w.shape[:-3] + (h * hd, h * hd))


def _prep_rg(w_a, b_a, w_x, b_x):
    bda, bdx = _block_diag(w_a), _block_diag(w_x)
    wg = jnp.concatenate([bda[:, 0], bdx[:, 0], bda[:, 1], bdx[:, 1]], axis=-1)
    bg = jnp.concatenate([b_a[:, 0], b_x[:, 0], b_a[:, 1], b_x[:, 1]], axis=-1)[:, None, :]
    return (0.5 * wg).astype(BF16), 0.5 * bg


def _flat(a):
    return a.reshape((1, a.shape[0] * a.shape[1]) + a.shape[2:])


def _mixer(h, mod, mod_row, l, lw, consts, s0, h0, line, tm, with_output):
    bsz, t, _ = h.shape
    f_pitch = FFT_PITCH if "m2" in consts else GRID_W
    if mod_row is None:
        zs = _inproj(h, mod, mod_row, l, lw["g_pre_mix"], lw["w_in"], tm, f_pitch)
    else:
        zs = [z.reshape(bsz, t, z.shape[-1])
              for z in _inproj(_flat(h), mod, mod_row, l, lw["g_pre_mix"], lw["w_in"], min(tm, bsz * t), f_pitch)]
    zq, zk, zv, zog, zf, zrx, zrg, zdec = zs
    yg, s_fin = _gla(zq, zk, zv, zog, zdec, l, lw["wd"], lw["bd"], lw["g_gla"], s0, with_output)
    yr, h_fin = _rg(zrx, zrg, l, lw["w_conv"], lw["b_conv"], lw["w_gate"], lw["b_gate"], lw["lam"], h0, line, with_output)
    yf = _fourier(zf, consts, t) if with_output else None
    return (yg, yf, yr), s_fin, h_fin


def kernel(x, c, ctx, c_ctx, w_ada, b_ada, g_pre_mix, g_post_mix, g_pre_ffn, g_post_ffn, w_in, w_dec, b_dec, g_gla,
           w_conv, b_conv, w_rg_a, b_rg_a, w_rg_x, b_rg_x, rg_lam, w_out, w_ffn_gate, w_ffn_up, w_ffn_down):
    bsz, t, d = x.shape
    t_ctx = ctx.shape[1]
    depth = w_ada.shape[0]

    c_all = jnp.concatenate([c, c_ctx[None, :], jnp.zeros((MOD_ROWS - bsz - 1, d), F32)], axis=0)
    mod = _ada(c_all, w_ada, b_ada).reshape(depth, MOD_ROWS, 6, d)

    consts_lat = _fourier_consts(t)
    consts_ctx = _fourier_consts(t_ctx)
    s_zero = jnp.zeros((bsz, 2, GLA_HEADS, GLA_DK, GLA_DV), F32)
    h_zero = jnp.zeros((bsz, SUBLANES, RG_W), F32)
    rows = lambda v: v[:, None, :]

    wd_p, bd_p = _prep_dec(w_dec, b_dec)
    w_gate, b_gate = _prep_rg(w_rg_a, b_rg_a, w_rg_x, b_rg_x)
    lw = dict(g_pre_mix=rows(g_pre_mix), w_in=_prep_w_in(w_in), wd=wd_p, bd=bd_p, g_gla=rows(g_gla),
              w_conv=w_conv, b_conv=rows(b_conv), w_gate=w_gate, b_gate=b_gate, lam=rg_lam)
    ffn = (rows(g_post_mix), rows(g_pre_ffn), rows(g_post_ffn), w_out.astype(BF16),
           w_ffn_gate.astype(BF16), w_ffn_up.astype(BF16), w_ffn_down.astype(BF16))

    h_ctx = ctx
    for l in range(depth):
        last = l == depth - 1
        ys_c, s_ctx, hs_ctx = _mixer(h_ctx, mod, bsz, l, lw, consts_ctx, s_zero, h_zero, t_ctx, 512, not last)
        ys_l, _, _ = _mixer(x, mod, None, l, lw, consts_lat, s_ctx, hs_ctx, GRID_W, 512, True)
        x = _outffn(*ys_l, x, mod, None, l, *ffn, 512)
        if not last:
            h_ctx = _outffn(*[_flat(y) for y in ys_c], _flat(h_ctx), mod, bsz, l, *ffn, min(512, bsz * t_ctx)).reshape(h_ctx.shape)
    return x
```

```python
import functools

import jax
import jax.numpy as jnp
import numpy as np
from jax import lax
from jax.experimental import pallas as pl
from jax.experimental.pallas import tpu as pltpu

F32 = jnp.float32
BF16 = jnp.bfloat16

D_MODEL = 1024
DEPTH = 4
GRID_W = 64
EPS = 1e-6

GLA_HEADS = 4
GLA_DK = 64
GLA_DV = 128
GLA_QK = GLA_HEADS * GLA_DK
GLA_V = GLA_HEADS * GLA_DV
GLA_RANK = 16
GLA_TAU = 16.0
GLA_CHUNK = 64
FNET_W = 256
FNET_GD = 64
RG_HEADS = 4
RG_HD = 64
RG_W = 256
RG_C = 8.0
CONV_W = 4
D_FF = 2816

LANES = 128
SUBLANES = 8
DEC_PAD = LANES
N_IN_PAD = 2 * GLA_QK + 2 * GLA_V + FNET_W + 2 * RG_W + DEC_PAD
MOD_ROWS = 16
VMEM_LIMIT = 56 * 1024 * 1024

_NT = (((1,), (1,)), ((), ()))
_TN = (((0,), (0,)), ((), ()))


def _rms(x, g):
    return x * lax.rsqrt(jnp.mean(x * x, axis=-1, keepdims=True) + EPS) * g


def _const_spec(shape):
    nd = len(shape)
    return pl.BlockSpec(shape, lambda *_: (0,) * nd, pipeline_mode=pl.Buffered(1))


def _layer_spec(arr, l):
    nd = arr.ndim - 1
    return pl.BlockSpec((None,) + arr.shape[1:], lambda *_: (l,) + (0,) * nd, pipeline_mode=pl.Buffered(1))


def _ada_body(c_ref, w_ref, b_ref, o_ref):
    c = c_ref[...]
    s = (c * jax.nn.sigmoid(c)).astype(BF16)
    o_ref[0] = jnp.dot(s, w_ref[0].astype(BF16), preferred_element_type=F32) + b_ref[0]


def _ada(c_all, w_ada, b_ada):
    depth, d, n = w_ada.shape
    tn = 1536
    return pl.pallas_call(
        _ada_body,
        out_shape=jax.ShapeDtypeStruct((depth, MOD_ROWS, n), F32),
        grid=(depth, n // tn),
        in_specs=[
            pl.BlockSpec((MOD_ROWS, d), lambda l, j: (0, 0)),
            pl.BlockSpec((1, d, tn), lambda l, j: (l, 0, j)),
            pl.BlockSpec((1, 1, tn), lambda l, j: (l, 0, j)),
        ],
        out_specs=pl.BlockSpec((1, MOD_ROWS, tn), lambda l, j: (l, 0, j)),
        compiler_params=pltpu.CompilerParams(vmem_limit_bytes=VMEM_LIMIT),
        name="ada",
    )(c_all, w_ada, b_ada.reshape(depth, 1, n))


SUB_ROWS = 256

_IN_WIDTHS = (GLA_QK, GLA_QK, GLA_V, GLA_V, FNET_W, RG_W, RG_W, DEC_PAD)
_IN_F = 4


def _inproj_body(x_ref, mod_ref, g_ref, w_ref, *out_refs, f_pitch):
    tm = x_ref.shape[1]
    sub = min(tm, SUB_ROWS)
    for r in range(tm // sub):
        rs = slice(r * sub, (r + 1) * sub)
        h = _rms(x_ref[0, rs, :], g_ref[...]) * (1.0 + mod_ref[0, 1:2, :]) + mod_ref[0, 0:1, :]
        h = h.astype(BF16)
        off = 0
        for idx, (o_ref, wd) in enumerate(zip(out_refs, _IN_WIDTHS)):
            z = jnp.dot(h, w_ref[:, off:off + wd], preferred_element_type=F32)
            off += wd
            if idx == _IN_F and f_pitch != GRID_W:
                pad = jnp.zeros((f_pitch - GRID_W, wd), F32)
                for ln in range(sub // GRID_W):
                    base = (r * (sub // GRID_W) + ln) * f_pitch
                    o_ref[0, base:base + GRID_W, :] = z[ln * GRID_W:(ln + 1) * GRID_W, :]
                    o_ref[0, base + GRID_W:base + f_pitch, :] = pad
            else:
                o_ref[0, rs, :] = z


def _inproj(x, mod, mod_row, l, g, w_in_p, tm, f_pitch):
    bsz, t, d = x.shape
    if mod_row is None:
        mod_map = lambda b, i: (l, b, 0, 0)
    else:
        mod_map = lambda b, i: (l, mod_row, 0, 0)
    rows_of = lambda idx, n: n // GRID_W * f_pitch if idx == _IN_F else n
    return pl.pallas_call(
        functools.partial(_inproj_body, f_pitch=f_pitch),
        out_shape=[jax.ShapeDtypeStruct((bsz, rows_of(i, t), wd), F32) for i, wd in enumerate(_IN_WIDTHS)],
        grid=(bsz, t // tm),
        in_specs=[
            pl.BlockSpec((1, tm, d), lambda b, i: (b, i, 0)),
            pl.BlockSpec((None, 1, 6, d), mod_map),
            _layer_spec(g, l),
            _layer_spec(w_in_p, l),
        ],
        out_specs=[pl.BlockSpec((1, rows_of(i, tm), wd), lambda b, i: (b, i, 0)) for i, wd in enumerate(_IN_WIDTHS)],
        compiler_params=pltpu.CompilerParams(vmem_limit_bytes=VMEM_LIMIT),
        name="inproj",
    )(x, mod, g, w_in_p)


GLA_BLOCK = 256
GLA_RINGS = 8


def _chunk_cumsum(x, reverse):
    rows, lanes = x.shape
    g = SUBLANES
    c = GLA_CHUNK
    x3 = x.reshape(rows // g, g, lanes)
    sub = lax.broadcasted_iota(jnp.int32, x3.shape, 1)
    for s in (1, 2, 4):
        if reverse:
            x3 = x3 + jnp.where(sub < g - s, pltpu.roll(x3, g - s, 1), 0.0)
        else:
            x3 = x3 + jnp.where(sub >= s, pltpu.roll(x3, s, 1), 0.0)
    edge = x3[:, 0:1, :] if reverse else x3[:, g - 1:g, :]
    tot = jnp.broadcast_to(edge, x3.shape).reshape(rows, lanes)
    pos = lax.broadcasted_iota(jnp.int32, (rows, lanes), 0) & (c - 1)
    acc = tot
    for s in (8, 16, 32):
        if reverse:
            acc = acc + jnp.where(pos < c - s, pltpu.roll(acc, rows - s, 0), 0.0)
        else:
            acc = acc + jnp.where(pos >= s, pltpu.roll(acc, s, 0), 0.0)
    sums = x3.reshape(rows, lanes) + (acc - tot)
    parts = []
    for ci in range(rows // c):
        r = ci * c if reverse else ci * c + c - 1
        parts.append(jnp.broadcast_to(acc[r:r + 1, :], (c, lanes)))
    total = parts[0] if len(parts) == 1 else jnp.concatenate(parts, axis=0)
    return sums, total


def _gla_body(q_ref, k_ref, v_ref, og_ref, dec_ref, wd_ref, bd_ref, g_ref, s0_ref, *rest, t, with_output):
    nr = GLA_RINGS
    if with_output:
        y_ref, sfin_ref, u_ref, dl_ref, qt_ref, o_ref = rest[:6]
        bufs = rest[6:]
        rings = tuple((bufs[r], bufs[nr + r], bufs[2 * nr + r], bufs[3 * nr + r]) for r in range(nr))
    else:
        sfin_ref, u_ref, dl_ref = rest[:3]
        bufs = rest[3:]
        rings = tuple((bufs[r], bufs[nr + r], None, None) for r in range(nr))
    c = GLA_CHUNK
    hp = 2 * GLA_DK
    vp = 2 * GLA_DV
    rows = min(t, GLA_BLOCK)
    cpb = rows // c
    n_blk = t // rows
    scale = GLA_DK ** -0.5

    dk_head0 = lax.broadcasted_iota(jnp.int32, (hp, 2 * c), 0) < GLA_DK
    ar = lax.broadcasted_iota(jnp.int32, (2 * c, 4 * c), 0)
    ac = lax.broadcasted_iota(jnp.int32, (2 * c, 4 * c), 1)
    same_chunk = (ar // c) == ((ac // c) & 1)
    keep_f = same_chunk & ((ar & (c - 1)) >= (ac & (c - 1)))
    keep_b = same_chunk & ((ar & (c - 1)) <= (ac & (c - 1)))
    wd = wd_ref[0]
    bd = bd_ref[0]

    def prepare(blk, ring):
        vb_ring, ke_ring, qt_ring, kt_ring = ring
        r0 = pl.multiple_of(blk * rows, rows)
        code = dec_ref[0, pl.ds(r0, rows), :].astype(BF16)
        logit = jnp.dot(code, wd, preferred_element_type=F32) + bd
        la = (jnp.minimum(logit, 0.0) - jnp.log(1.0 + jnp.exp(-jnp.abs(logit)))) * (1.0 / GLA_TAU)
        k = k_ref[0, pl.ds(r0, rows), :]
        vb_ring[...] = v_ref[0, pl.ds(r0, rows), :].astype(BF16)
        if with_output:
            q = q_ref[0, pl.ds(r0, rows), :] * scale
        for dirn in range(2):
            b, b_all = _chunk_cumsum(la[:, dirn * hp:(dirn + 1) * hp], reverse=dirn == 1)
            ke_ring[dirn] = (k * jnp.exp(b_all - b)).astype(BF16)
            for ci in range(cpb):
                dl_ref[dirn, blk * cpb + ci] = jnp.exp(b_all[ci * c:ci * c + SUBLANES, :])
            if with_output:
                qt = (q * jnp.exp(b)).astype(BF16)
                qt_ref[dirn, pl.ds(r0, rows), :] = qt
                qt_ring[dirn] = qt
                kt_ring[dirn] = (k * jnp.exp(-b)).T.astype(BF16)

    def multiply(blk, ring):
        vb_ring, ke_ring, qt_ring, kt_ring = ring
        r0 = pl.multiple_of(blk * rows, rows)
        vb = vb_ring[...]
        for dirn in range(2):
            ke = ke_ring[dirn]
            for ci in range(cpb):
                sl = slice(ci * c, (ci + 1) * c)
                upd = lax.dot_general(ke[sl], vb[sl], _TN, preferred_element_type=F32)
                u_ref[dirn, blk * cpb + ci, 0] = upd[0:GLA_DK, 0:GLA_DV]
                u_ref[dirn, blk * cpb + ci, 1] = upd[GLA_DK:hp, GLA_DV:vp]
            if not with_output:
                continue
            qt = qt_ring[dirn]
            kt_t = kt_ring[dirn]
            zt = jnp.zeros((hp, 2 * c), BF16)
            zv = jnp.zeros((2 * c, GLA_DV), BF16)
            parts = []
            for cp in range(cpb // 2):
                sl = slice(cp * 2 * c, (cp + 1) * 2 * c)
                kt_cp = kt_t[:, sl]
                kbd = jnp.concatenate([jnp.where(dk_head0, kt_cp, zt), jnp.where(dk_head0, zt, kt_cp)], axis=1)
                sc = jnp.dot(qt[sl], kbd, preferred_element_type=F32)
                p = jnp.where(keep_f if dirn == 0 else keep_b, sc, 0.0).astype(BF16)
                v_cp = vb[sl]
                vbd = jnp.concatenate(
                    [jnp.concatenate([v_cp[:, 0:GLA_DV], zv], axis=1),
                     jnp.concatenate([zv, v_cp[:, GLA_DV:vp]], axis=1)], axis=0)
                parts.append(jnp.dot(p, vbd, preferred_element_type=F32))
            o_intra = parts[0] if len(parts) == 1 else jnp.concatenate(parts, axis=0)
            if dirn == 0:
                o_ref[pl.ds(r0, rows), :] = o_intra
            else:
                o_ref[pl.ds(r0, rows), :] += o_intra

    per = min(n_blk, nr)

    def pre_step(j, carry):
        i = per * j
        prepare(i, rings[0])
        for r in range(per):
            if r + 1 < per:
                prepare(i + r + 1, rings[r + 1])
            multiply(i + r, rings[r])
        return carry

    lax.fori_loop(0, n_blk // per, pre_step, 0)

    zs = jnp.zeros((GLA_DK, GLA_DV), BF16)

    def gate_block(r0):
        rs = pl.ds(r0, rows)
        o = o_ref[rs, :]
        og = og_ref[0, rs, :]
        g = g_ref[...]
        o0 = _rms(o[:, 0:GLA_DV], g)
        o1 = _rms(o[:, GLA_DV:vp], g)
        gate = og * jax.nn.sigmoid(og)
        y_ref[0, rs, :] = (jnp.concatenate([o0, o1], axis=1) * gate).astype(y_ref.dtype)

    for dirn in range(2):
        def seq_step(i, carry, dirn=dirn):
            s0, s1 = carry
            blk = i if dirn == 0 else n_blk - 1 - i
            r0 = pl.multiple_of(blk * rows, rows)
            for cj in range(cpb):
                ci = cj if dirn == 0 else cpb - 1 - cj
                chunk = blk * cpb + ci
                if with_output:
                    rs = pl.ds(r0 + ci * c, c)
                    sbd = jnp.concatenate(
                        [jnp.concatenate([s0.astype(BF16), zs], axis=1),
                         jnp.concatenate([zs, s1.astype(BF16)], axis=1)], axis=0)
                    o_ref[rs, :] += jnp.dot(qt_ref[dirn, rs, :], sbd, preferred_element_type=F32)
                drow = dl_ref[dirn, chunk][0:1, :]
                dcol = jnp.broadcast_to(drow, (hp, hp)).T
                s0 = s0 * dcol[0:GLA_DK, :] + u_ref[dirn, chunk, 0]
                s1 = s1 * dcol[GLA_DK:hp, :] + u_ref[dirn, chunk, 1]
            if with_output and dirn == 1:
                gate_block(r0)
            return s0, s1

        s0, s1 = lax.fori_loop(0, n_blk, seq_step, (s0_ref[0, dirn, 0], s0_ref[0, dirn, 1]), unroll=min(8 if dirn == 0 else 4, n_blk))
        sfin_ref[0, dirn, 0] = s0
        sfin_ref[0, dirn, 1] = s1


def _gla(zq, zk, zv, zog, zdec, l, wd_p, bd_p, g_gla, s0, with_output):
    bsz, t, _ = zq.shape
    hp, vp = 2 * GLA_DK, 2 * GLA_DV
    n_chunks = t // GLA_CHUNK
    seq = lambda wd: pl.BlockSpec((1, t, wd), lambda b, p: (b, 0, p))
    st_spec = pl.BlockSpec((1, 2, 2, GLA_DK, GLA_DV), lambda b, p: (b, 0, p, 0, 0))
    out_shape = [jax.ShapeDtypeStruct((bsz, 2, GLA_HEADS, GLA_DK, GLA_DV), F32)]
    out_specs = [st_spec]
    rows = min(t, GLA_BLOCK)
    nr = GLA_RINGS
    scratch = [pltpu.VMEM((2, n_chunks, 2, GLA_DK, GLA_DV), F32), pltpu.VMEM((2, n_chunks, SUBLANES, hp), F32)]
    staging = [pltpu.VMEM((rows, vp), BF16) for _ in range(nr)] + [pltpu.VMEM((2, rows, hp), BF16) for _ in range(nr)]
    if with_output:
        out_shape = [jax.ShapeDtypeStruct((bsz, t, GLA_V), BF16)] + out_shape
        out_specs = [seq(vp)] + out_specs
        scratch = scratch + [pltpu.VMEM((2, t, hp), BF16), pltpu.VMEM((t, vp), F32)]
        staging = staging + [pltpu.VMEM((2, rows, hp), BF16) for _ in range(nr)]
        staging = staging + [pltpu.VMEM((2, hp, rows), BF16) for _ in range(nr)]
    scratch = scratch + staging
    outs = pl.pallas_call(
        functools.partial(_gla_body, t=t, with_output=with_output),
        out_shape=out_shape,
        grid=(bsz, 2),
        in_specs=[
            seq(hp), seq(hp), seq(vp), seq(vp),
            pl.BlockSpec((1, t, DEC_PAD), lambda b, p: (b, 0, 0)),
            pl.BlockSpec((None, 1, DEC_PAD, 2 * hp), lambda b, p: (l, p, 0, 0)),
            pl.BlockSpec((None, 1, 1, 2 * hp), lambda b, p: (l, p, 0, 0)),
            _layer_spec(g_gla, l),
            st_spec,
        ],
        out_specs=out_specs,
        scratch_shapes=scratch,
        compiler_params=pltpu.CompilerParams(vmem_limit_bytes=VMEM_LIMIT),
        name="gla",
    )(zq, zk, zv, zog, zdec, wd_p, bd_p, g_gla, s0)
    if with_output:
        return outs[0], outs[1]
    return None, outs[0]


RG_BLOCK = 256


def _sigmoid(x):
    return 0.5 * jnp.tanh(0.5 * x) + 0.5


def _rg_body(rx_ref, rg_ref, wc_ref, bc_ref, wg_ref, bg_ref, lam_ref, h0_ref, *rest, t, line, with_output):
    if with_output:
        y_ref, hfin_ref, hf_ref, ab_ref, bb_ref = rest
    else:
        hfin_ref, hf_ref, ab_ref, bb_ref = rest
    w = RG_W
    rows = min(t, RG_BLOCK)
    n_blk = t // rows
    grp = SUBLANES
    gpb = rows // grp
    pos = lax.broadcasted_iota(jnp.int32, (rows, w), 0) & (line - 1)
    sub = lax.broadcasted_iota(jnp.int32, (rows // grp, grp, w), 1)
    wc = wc_ref[...]
    bc = bc_ref[...]
    wg = wg_ref[...]
    bg = bg_ref[...]
    sp = [jax.nn.softplus(-lam_ref[d:d + 1, :]) * (-RG_C) for d in range(2)]

    def pre_step(blk, hf):
        r0 = pl.multiple_of(blk * rows, rows)
        rs = pl.ds(r0, rows)
        x = rx_ref[0, rs, :]
        xm1 = jnp.where(pos >= 1, pltpu.roll(x, 1, 0), 0.0)
        xp1 = jnp.where(pos <= line - 2, pltpu.roll(x, rows - 1, 0), 0.0)
        xp2 = jnp.where(pos <= line - 3, pltpu.roll(x, rows - 2, 0), 0.0)
        u = bc + xm1 * wc[0:1, :] + x * wc[1:2, :] + xp1 * wc[2:3, :] + xp2 * wc[3:4, :]
        gates = jnp.dot(u.astype(BF16), wg, preferred_element_type=F32) + bg
        coef = []
        for dirn in range(2):
            r = _sigmoid(gates[:, (2 * dirn) * w:(2 * dirn + 1) * w])
            i = _sigmoid(gates[:, (2 * dirn + 1) * w:(2 * dirn + 2) * w])
            log_a = r * sp[dirn]
            a = jnp.exp(log_a)
            bx = jnp.sqrt(-jnp.tanh(log_a) * (a * a + 1.0)) * (i * u)
            a3 = a.reshape(rows // grp, grp, w)
            b3 = bx.reshape(rows // grp, grp, w)
            for s in (1, 2, 4):
                if dirn == 0:
                    ok = sub >= s
                    sh = s
                else:
                    ok = sub <= grp - 1 - s
                    sh = grp - s
                b3 = jnp.where(ok, a3 * pltpu.roll(b3, sh, 1) + b3, b3)
                a3 = jnp.where(ok, a3 * pltpu.roll(a3, sh, 1), a3)
            coef.append((a3.reshape(rows, w), b3.reshape(rows, w)))
        ab_ref[rs, :] = coef[1][0]
        bb_ref[rs, :] = coef[1][1]
        af, bf = coef[0]
        for gi in range(gpb):
            g0 = gi * grp
            hfull = bf[g0:g0 + grp, :] + af[g0:g0 + grp, :] * hf
            hf_ref[pl.ds(r0 + g0, grp), :] = hfull
            hf = hfull[grp - 1:grp, :]
        return hf

    hf = lax.fori_loop(0, n_blk, pre_step, h0_ref[0, 0:1, :])

    def bwd_step(j, hb):
        blk = n_blk - 1 - j
        r0 = pl.multiple_of(blk * rows, rows)
        parts = []
        for gi in range(gpb - 1, -1, -1):
            g0 = r0 + gi * grp
            hbull = bb_ref[pl.ds(g0, grp), :] + ab_ref[pl.ds(g0, grp), :] * hb
            hb = hbull[0:1, :]
            parts.append(hbull)
        if with_output:
            hbwd = jnp.concatenate(parts[::-1], axis=0)
            rs = pl.ds(r0, rows)
            h = hf_ref[rs, :] + hbwd
            y_ref[0, rs, :] = (h * jax.nn.gelu(rg_ref[0, rs, :])).astype(y_ref.dtype)
        return hb

    hb = lax.fori_loop(0, n_blk, bwd_step, h0_ref[0, 1:2, :])
    hfin_ref[0, 0:1, :] = hf
    hfin_ref[0, 1:2, :] = hb
    hfin_ref[0, 2:grp, :] = jnp.zeros((grp - 2, w), F32)


def _rg(zrx, zrg, l, w_conv, b_conv, w_gate, b_gate, lam, h0, line, with_output):
    bsz, t, w = zrx.shape
    seq = pl.BlockSpec((1, t, w), lambda b: (b, 0, 0))
    st_spec = pl.BlockSpec((1, SUBLANES, w), lambda b: (b, 0, 0))
    out_shape = [jax.ShapeDtypeStruct((bsz, SUBLANES, w), F32)]
    out_specs = [st_spec]
    if with_output:
        out_shape = [jax.ShapeDtypeStruct((bsz, t, w), BF16)] + out_shape
        out_specs = [seq] + out_specs
    outs = pl.pallas_call(
        functools.partial(_rg_body, t=t, line=line, with_output=with_output),
        out_shape=out_shape,
        grid=(bsz,),
        in_specs=[
            seq, seq,
            _layer_spec(w_conv, l), _layer_spec(b_conv, l),
            _layer_spec(w_gate, l), _layer_spec(b_gate, l),
            _layer_spec(lam, l),
            st_spec,
        ],
        out_specs=out_specs,
        scratch_shapes=[pltpu.VMEM((t, w), F32) for _ in range(3)],
        compiler_params=pltpu.CompilerParams(vmem_limit_bytes=VMEM_LIMIT),
        name="rglru",
    )(zrx, zrg, w_conv, b_conv, w_gate, b_gate, lam, h0)
    if with_output:
        return outs[0], outs[1]
    return None, outs[0]


FFT_PITCH = 72


def _dft_mats(n):
    j = np.arange(n)
    ang = 2.0 * np.pi * ((j[:, None] * j[None, :]) % n) / n
    return np.cos(ang), np.sin(ang)


def _fourier_consts(t):
    n2 = GRID_W
    n1 = t // n2
    cc, sc = _dft_mats(FNET_GD)
    eye2 = np.eye(LANES // FNET_GD)
    m3 = np.concatenate([np.kron(eye2, cc), -np.kron(eye2, sc)], axis=0) / 8.0
    c2, s2 = _dft_mats(n2)
    consts = {"m3": jnp.asarray(m3, F32)}
    if n1 == n2:
        consts["cs"] = jnp.asarray(np.concatenate([c2, s2], axis=0) / 8.0, F32)
        consts["m2"] = jnp.asarray(np.block([[c2, -s2], [s2, c2]]) / 8.0, F32)
        wk = (np.arange(n2)[:, None] * np.arange(n1)[None, :]).reshape(-1)
        ang = 2.0 * np.pi * wk / t
        consts["twc"] = jnp.asarray(np.repeat(np.cos(ang)[:, None], LANES, axis=1), F32)
        consts["tws"] = jnp.asarray(np.repeat(np.sin(ang)[:, None], LANES, axis=1), F32)
    else:
        ct, st = _dft_mats(t)
        consts["cs"] = jnp.asarray(np.concatenate([ct, st], axis=0) / np.sqrt(t), F32)
    return consts


def _fft_body(xa_ref, xb_ref, cs_ref, m2_ref, m3_ref, twc_ref, tws_ref, y_ref, *scr, t):
    n = GRID_W
    p = FFT_PITCH
    cs = cs_ref[...].astype(BF16)
    m2 = m2_ref[...].astype(BF16)
    m3 = m3_ref[...].astype(BF16)
    halves = ((xa_ref,) + tuple(scr[0:4]), (xb_ref,) + tuple(scr[4:8]))

    def stage1(h, i):
        x_ref, sr_ref, si_ref, _, _ = halves[h]
        w0 = 2 * i
        xw = jnp.concatenate([x_ref[0, pl.ds(w0 + j, n, stride=p), :] for j in range(2)], axis=1)
        a = jnp.dot(cs, xw.astype(BF16), preferred_element_type=F32)
        for j in range(2):
            ar = a[0:n, j * LANES:(j + 1) * LANES]
            ai = a[n:2 * n, j * LANES:(j + 1) * LANES]
            r0 = pl.multiple_of((w0 + j) * n, n)
            tc = twc_ref[pl.ds(r0, n), :]
            ts = tws_ref[pl.ds(r0, n), :]
            sr_ref[pl.ds(w0 + j, n, stride=p), :] = ar * tc - ai * ts
            si_ref[pl.ds(w0 + j, n, stride=p), :] = ar * ts + ai * tc

    def stage2(h, i):
        _, sr_ref, si_ref, dr_ref, di_ref = halves[h]
        k0 = 2 * i
        cols = []
        for j in range(2):
            r0 = pl.multiple_of((k0 + j) * p, SUBLANES)
            cols.append(jnp.concatenate([sr_ref[pl.ds(r0, n), :], si_ref[pl.ds(r0, n), :]], axis=0))
        xb = jnp.concatenate(cols, axis=1).astype(BF16)
        d = jnp.dot(m2, xb, preferred_element_type=F32)
        for j in range(2):
            dr_ref[pl.ds(k0 + j, n, stride=p), :] = d[0:n, j * LANES:(j + 1) * LANES]
            di_ref[pl.ds(k0 + j, n, stride=p), :] = d[n:2 * n, j * LANES:(j + 1) * LANES]

    per = 8

    def stage3(h, i):
        _, _, _, dr_ref, di_ref = halves[h]
        blocks = []
        for j in range(per):
            r0 = pl.multiple_of((i * per + j) * p, SUBLANES)
            blocks.append(jnp.concatenate([dr_ref[pl.ds(r0, n), :], di_ref[pl.ds(r0, n), :]], axis=1))
        d = jnp.concatenate(blocks, axis=0).astype(BF16)
        rs = pl.ds(pl.multiple_of(i * per * n, per * n), per * n)
        y_ref[0, rs, h * LANES:(h + 1) * LANES] = jnp.dot(d, m3, preferred_element_type=F32).astype(y_ref.dtype)

    def loop(trips, unroll, *stages):
        def body(i, carry):
            for fn in stages:
                fn(i)
            return carry
        lax.fori_loop(0, trips, body, 0, unroll=unroll)

    ratio = (n // 2) // (n // per)
    loop(n // 2, 8, lambda i: stage1(0, i))
    loop(n // 2, 8, lambda i: stage2(0, i), lambda i: stage1(1, i))
    loop(n // per, 2, lambda i: stage3(0, i), lambda i: [stage2(1, ratio * i + k) for k in range(ratio)])
    loop(n // per, 2, lambda i: stage3(1, i))


def _dense_dft_body(x_ref, cs_ref, m3_ref, y_ref, *, t):
    a = jnp.dot(cs_ref[...].astype(BF16), x_ref[0].astype(BF16), preferred_element_type=F32)
    d = jnp.concatenate([a[0:t], a[t:2 * t]], axis=1).astype(BF16)
    y_ref[0] = jnp.dot(d, m3_ref[...].astype(BF16), preferred_element_type=F32).astype(y_ref.dtype)


def _fourier(zf, consts, t):
    bsz, t_in, w = zf.shape
    seq = pl.BlockSpec((1, t, LANES), lambda b, j: (b, 0, j))
    seq_in = pl.BlockSpec((1, t_in, LANES), lambda b, j: (b, 0, j))
    common = dict(
        out_shape=jax.ShapeDtypeStruct((bsz, t, w), BF16),
        grid=(bsz, w // LANES),
        out_specs=seq,
        compiler_params=pltpu.CompilerParams(vmem_limit_bytes=VMEM_LIMIT),
    )
    if "m2" in consts:
        full = pl.BlockSpec((1, t, w), lambda b: (b, 0, 0))
        half = lambda j: pl.BlockSpec((1, t_in, LANES), lambda b: (b, 0, j))
        return pl.pallas_call(
            functools.partial(_fft_body, t=t),
            out_shape=jax.ShapeDtypeStruct((bsz, t, w), BF16),
            grid=(bsz,),
            in_specs=[half(0), half(1), _const_spec(consts["cs"].shape), _const_spec(consts["m2"].shape),
                      _const_spec(consts["m3"].shape), _const_spec((t, LANES)), _const_spec((t, LANES))],
            out_specs=full,
            scratch_shapes=[pltpu.VMEM((GRID_W * FFT_PITCH, LANES), F32) for _ in range(8)],
            compiler_params=pltpu.CompilerParams(vmem_limit_bytes=VMEM_LIMIT),
            name="fourier_fft",
        )(zf, zf, consts["cs"], consts["m2"], consts["m3"], consts["twc"], consts["tws"])
    return pl.pallas_call(
        functools.partial(_dense_dft_body, t=t),
        in_specs=[seq_in, _const_spec(consts["cs"].shape), _const_spec(consts["m3"].shape)],
        name="fourier_dense", **common,
    )(zf, consts["cs"], consts["m3"])


FF_PIECES = (1536, 1280)


def _outffn_body(yg_ref, yf_ref, yr_ref, x_ref, mod_ref, gpm_ref, gpf_ref, gqf_ref,
                 wo_ref, wg_ref, wu_ref, wd_ref, o_ref):
    tm = x_ref.shape[1]
    sub = min(tm, SUB_ROWS)
    subs = [slice(r * sub, (r + 1) * sub) for r in range(tm // sub)]
    mixes = []
    for rs in subs:
        y = jnp.concatenate([yg_ref[0, rs, :], yf_ref[0, rs, :], yr_ref[0, rs, :]], axis=1)
        mixes.append(jnp.dot(y, wo_ref[...], preferred_element_type=F32))
    x1s, hfs = [], []
    for rs, mix in zip(subs, mixes):
        x1 = x_ref[0, rs, :] + mod_ref[0, 2:3, :] * _rms(mix, gpm_ref[...])
        x1s.append(x1)
        hfs.append((_rms(x1, gpf_ref[...]) * (1.0 + mod_ref[0, 4:5, :]) + mod_ref[0, 3:4, :]).astype(BF16))
    for rs, x1, hf in zip(subs, x1s, hfs):
        acc = None
        off = 0
        for fw in FF_PIECES:
            gate = jnp.dot(hf, wg_ref[:, off:off + fw], preferred_element_type=F32)
            up = jnp.dot(hf, wu_ref[:, off:off + fw], preferred_element_type=F32)
            act = (gate * jax.nn.sigmoid(gate) * up).astype(BF16)
            part = jnp.dot(act, wd_ref[off:off + fw, :], preferred_element_type=F32)
            acc = part if acc is None else acc + part
            off += fw
        o_ref[0, rs, :] = x1 + mod_ref[0, 5:6, :] * _rms(acc, gqf_ref[...])


def _outffn(yg, yf, yr, x, mod, mod_row, l, gpm, gpf, gqf, wo, wg, wu, wd, tm):
    bsz, t, d = x.shape
    if mod_row is None:
        mod_map = lambda b, i: (l, b, 0, 0)
    else:
        mod_map = lambda b, i: (l, mod_row, 0, 0)
    tile = lambda wdt: pl.BlockSpec((1, tm, wdt), lambda b, i: (b, i, 0))
    return pl.pallas_call(
        _outffn_body,
        out_shape=jax.ShapeDtypeStruct((bsz, t, d), F32),
        grid=(bsz, t // tm),
        in_specs=[
            tile(GLA_V), tile(FNET_W), tile(RG_W), tile(d),
            pl.BlockSpec((None, 1, 6, d), mod_map),
            _layer_spec(gpm, l), _layer_spec(gpf, l), _layer_spec(gqf, l),
            _layer_spec(wo, l), _layer_spec(wg, l), _layer_spec(wu, l), _layer_spec(wd, l),
        ],
        out_specs=tile(d),
        compiler_params=pltpu.CompilerParams(vmem_limit_bytes=VMEM_LIMIT),
        name="outffn",
    )(yg, yf, yr, x, mod, gpm, gpf, gqf, wo, wg, wu, wd)


def _prep_w_in(w_in):
    off_dec = 2 * GLA_QK + GLA_V
    off_og = off_dec + 2 * GLA_RANK
    pad = jnp.zeros(w_in.shape[:2] + (DEC_PAD - 2 * GLA_RANK,), w_in.dtype)
    return jnp.concatenate([w_in[..., 0:off_dec], w_in[..., off_og:], w_in[..., off_dec:off_og], pad], axis=-1).astype(BF16)


def _prep_dec(w_dec, b_dec):
    depth = w_dec.shape[0]
    hp = 2 * GLA_DK
    wr = w_dec.reshape(depth, 2, GLA_RANK, 2, hp)
    wbd = jnp.einsum("ldrpj,de->lpdrej", wr, jnp.eye(2, dtype=w_dec.dtype))
    wbd = wbd.reshape(depth, 2, 2 * GLA_RANK, 2 * hp)
    wbd = jnp.pad(wbd, ((0, 0), (0, 0), (0, DEC_PAD - 2 * GLA_RANK), (0, 0)))
    bd = b_dec.reshape(depth, 2, 2, hp).transpose(0, 2, 1, 3).reshape(depth, 2, 1, 2 * hp)
    return wbd.astype(BF16), bd


def _block_diag(w):
    h, hd = w.shape[-3], w.shape[-2]
    out = jnp.einsum("...hij,hg->...higj", w, jnp.eye(h, dtype=w.dtype))
    return out.reshape(w.shape[:-3] + (h * hd, h * hd))


def _prep_rg(w_a, b_a, w_x, b_x):
    bda, bdx = _block_diag(w_a), _block_diag(w_x)
    wg = jnp.concatenate([bda[:, 0], bdx[:, 0], bda[:, 1], bdx[:, 1]], axis=-1)
    bg = jnp.concatenate([b_a[:, 0], b_x[:, 0], b_a[:, 1], b_x[:, 1]], axis=-1)[:, None, :]
    return wg.astype(BF16), bg


def _mixer(h, mod, mod_row, l, lw, consts, s0, h0, line, tm, with_output):
    t = h.shape[1]
    f_pitch = FFT_PITCH if "m2" in consts else GRID_W
    zq, zk, zv, zog, zf, zrx, zrg, zdec = _inproj(h, mod, mod_row, l, lw["g_pre_mix"], lw["w_in"], tm, f_pitch)
    yg, s_fin = _gla(zq, zk, zv, zog, zdec, l, lw["wd"], lw["bd"], lw["g_gla"], s0, with_output)
    yr, h_fin = _rg(zrx, zrg, l, lw["w_conv"], lw["b_conv"], lw["w_gate"], lw["b_gate"], lw["lam"], h0, line, with_output)
    yf = _fourier(zf, consts, t) if with_output else None
    return (yg, yf, yr), s_fin, h_fin


def kernel(x, c, ctx, c_ctx, w_ada, b_ada, g_pre_mix, g_post_mix, g_pre_ffn, g_post_ffn, w_in, w_dec, b_dec, g_gla,
           w_conv, b_conv, w_rg_a, b_rg_a, w_rg_x, b_rg_x, rg_lam, w_out, w_ffn_gate, w_ffn_up, w_ffn_down):
    bsz, t, d = x.shape
    t_ctx = ctx.shape[1]
    depth = w_ada.shape[0]

    c_all = jnp.concatenate([c, c_ctx[None, :], jnp.zeros((MOD_ROWS - bsz - 1, d), F32)], axis=0)
    mod = _ada(c_all, w_ada, b_ada).reshape(depth, MOD_ROWS, 6, d)

    consts_lat = _fourier_consts(t)
    consts_ctx = _fourier_consts(t_ctx)
    s_zero = jnp.zeros((bsz, 2, GLA_HEADS, GLA_DK, GLA_DV), F32)
    h_zero = jnp.zeros((bsz, SUBLANES, RG_W), F32)
    rows = lambda v: v[:, None, :]

    wd_p, bd_p = _prep_dec(w_dec, b_dec)
    w_gate, b_gate = _prep_rg(w_rg_a, b_rg_a, w_rg_x, b_rg_x)
    lw = dict(g_pre_mix=rows(g_pre_mix), w_in=_prep_w_in(w_in), wd=wd_p, bd=bd_p, g_gla=rows(g_gla),
              w_conv=w_conv, b_conv=rows(b_conv), w_gate=w_gate, b_gate=b_gate, lam=rg_lam)
    ffn = (rows(g_post_mix), rows(g_pre_ffn), rows(g_post_ffn), w_out.astype(BF16),
           w_ffn_gate.astype(BF16), w_ffn_up.astype(BF16), w_ffn_down.astype(BF16))

    h_ctx = ctx
    for l in range(depth):
        last = l == depth - 1
        ys_c, s_ctx, hs_ctx = _mixer(h_ctx, mod, bsz, l, lw, consts_ctx, s_zero, h_zero, t_ctx, t_ctx, not last)
        ys_l, _, _ = _mixer(x, mod, None, l, lw, consts_lat, s_ctx, hs_ctx, GRID_W, 1024, True)
        x = _outffn(*ys_l, x, mod, None, l, *ffn, 1024)
        if not last:
            h_ctx = _outffn(*ys_c, h_ctx, mod, bsz, l, *ffn, t_ctx)
    return x
```

```python
import functools

import jax
import jax.numpy as jnp
import numpy as np
from jax import lax
from jax.experimental import pallas as pl
from jax.experimental.pallas import tpu as pltpu

F32 = jnp.float32
BF16 = jnp.bfloat16

D_MODEL = 1024
DEPTH = 4
GRID_W = 64
EPS = 1e-6

GLA_HEADS = 4
GLA_DK = 64
GLA_DV = 128
GLA_QK = GLA_HEADS * GLA_DK
GLA_V = GLA_HEADS * GLA_DV
GLA_RANK = 16
GLA_TAU = 16.0
GLA_CHUNK = 64
FNET_W = 256
FNET_GD = 64
RG_HEADS = 4
RG_HD = 64
RG_W = 256
RG_C = 8.0
CONV_W = 4
D_FF = 2816

LANES = 128
SUBLANES = 8
DEC_PAD = LANES
N_IN_PAD = 2 * GLA_QK + 2 * GLA_V + FNET_W + 2 * RG_W + DEC_PAD
MOD_ROWS = 16
VMEM_LIMIT = 56 * 1024 * 1024

_NT = (((1,), (1,)), ((), ()))
_TN = (((0,), (0,)), ((), ()))


def _rms(x, g):
    return x * lax.rsqrt(jnp.mean(x * x, axis=-1, keepdims=True) + EPS) * g


def _const_spec(shape):
    nd = len(shape)
    return pl.BlockSpec(shape, lambda *_: (0,) * nd, pipeline_mode=pl.Buffered(1))


def _layer_spec(arr, l):
    nd = arr.ndim - 1
    return pl.BlockSpec((None,) + arr.shape[1:], lambda *_: (l,) + (0,) * nd, pipeline_mode=pl.Buffered(1))


def _ada_body(c_ref, w_ref, b_ref, o_ref):
    c = c_ref[...]
    s = (c * jax.nn.sigmoid(c)).astype(BF16)
    o_ref[0] = jnp.dot(s, w_ref[0].astype(BF16), preferred_element_type=F32) + b_ref[0]


def _ada(c_all, w_ada, b_ada):
    depth, d, n = w_ada.shape
    tn = 1536
    return pl.pallas_call(
        _ada_body,
        out_shape=jax.ShapeDtypeStruct((depth, MOD_ROWS, n), F32),
        grid=(depth, n // tn),
        in_specs=[
            pl.BlockSpec((MOD_ROWS, d), lambda l, j: (0, 0)),
            pl.BlockSpec((1, d, tn), lambda l, j: (l, 0, j)),
            pl.BlockSpec((1, 1, tn), lambda l, j: (l, 0, j)),
        ],
        out_specs=pl.BlockSpec((1, MOD_ROWS, tn), lambda l, j: (l, 0, j)),
        compiler_params=pltpu.CompilerParams(vmem_limit_bytes=VMEM_LIMIT),
        name="ada",
    )(c_all, w_ada, b_ada.reshape(depth, 1, n))


ROW_TILE = 1024
SUB_ROWS = 256

_IN_WIDTHS = (GLA_QK, GLA_QK, GLA_V, GLA_V, FNET_W, RG_W, RG_W, DEC_PAD)
_IN_F = 4


def _inproj_body(x_ref, mod_ref, g_ref, w_ref, *out_refs, f_pitch):
    tm = x_ref.shape[1]
    sub = min(tm, SUB_ROWS)
    for r in range(tm // sub):
        rs = slice(r * sub, (r + 1) * sub)
        h = _rms(x_ref[0, rs, :], g_ref[...]) * (1.0 + mod_ref[0, 1:2, :]) + mod_ref[0, 0:1, :]
        h = h.astype(BF16)
        off = 0
        for idx, (o_ref, wd) in enumerate(zip(out_refs, _IN_WIDTHS)):
            z = jnp.dot(h, w_ref[:, off:off + wd], preferred_element_type=F32)
            off += wd
            if idx == _IN_F and f_pitch != GRID_W:
                pad = jnp.zeros((f_pitch - GRID_W, wd), F32)
                for ln in range(sub // GRID_W):
                    base = (r * (sub // GRID_W) + ln) * f_pitch
                    o_ref[0, base:base + GRID_W, :] = z[ln * GRID_W:(ln + 1) * GRID_W, :]
                    o_ref[0, base + GRID_W:base + f_pitch, :] = pad
            else:
                o_ref[0, rs, :] = z


def _inproj(x, mod, mod_row, l, g, w_in_p, tm, f_pitch):
    bsz, t, d = x.shape
    if mod_row is None:
        mod_map = lambda b, i: (l, b, 0, 0)
    else:
        mod_map = lambda b, i: (l, mod_row, 0, 0)
    rows_of = lambda idx, n: n // GRID_W * f_pitch if idx == _IN_F else n
    return pl.pallas_call(
        functools.partial(_inproj_body, f_pitch=f_pitch),
        out_shape=[jax.ShapeDtypeStruct((bsz, rows_of(i, t), wd), F32) for i, wd in enumerate(_IN_WIDTHS)],
        grid=(bsz, t // tm),
        in_specs=[
            pl.BlockSpec((1, tm, d), lambda b, i: (b, i, 0)),
            pl.BlockSpec((None, 1, 6, d), mod_map),
            _layer_spec(g, l),
            _layer_spec(w_in_p, l),
        ],
        out_specs=[pl.BlockSpec((1, rows_of(i, tm), wd), lambda b, i: (b, i, 0)) for i, wd in enumerate(_IN_WIDTHS)],
        compiler_params=pltpu.CompilerParams(vmem_limit_bytes=VMEM_LIMIT),
        name="inproj",
    )(x, mod, g, w_in_p)


GLA_BLOCK = 256
GLA_RINGS = 8


def _chunk_cumsum(x, reverse):
    rows, lanes = x.shape
    g = SUBLANES
    c = GLA_CHUNK
    x3 = x.reshape(rows // g, g, lanes)
    sub = lax.broadcasted_iota(jnp.int32, x3.shape, 1)
    for s in (1, 2, 4):
        if reverse:
            x3 = x3 + jnp.where(sub < g - s, pltpu.roll(x3, g - s, 1), 0.0)
        else:
            x3 = x3 + jnp.where(sub >= s, pltpu.roll(x3, s, 1), 0.0)
    edge = x3[:, 0:1, :] if reverse else x3[:, g - 1:g, :]
    tot = jnp.broadcast_to(edge, x3.shape).reshape(rows, lanes)
    pos = lax.broadcasted_iota(jnp.int32, (rows, lanes), 0) & (c - 1)
    acc = tot
    for s in (8, 16, 32):
        if reverse:
            acc = acc + jnp.where(pos < c - s, pltpu.roll(acc, rows - s, 0), 0.0)
        else:
            acc = acc + jnp.where(pos >= s, pltpu.roll(acc, s, 0), 0.0)
    sums = x3.reshape(rows, lanes) + (acc - tot)
    parts = []
    for ci in range(rows // c):
        r = ci * c if reverse else ci * c + c - 1
        parts.append(jnp.broadcast_to(acc[r:r + 1, :], (c, lanes)))
    total = parts[0] if len(parts) == 1 else jnp.concatenate(parts, axis=0)
    return sums, total


def _gla_body(q_ref, k_ref, v_ref, og_ref, dec_ref, wd_ref, bd_ref, g_ref, s0_ref, *rest, t, with_output):
    nr = GLA_RINGS
    if with_output:
        y_ref, sfin_ref, u_ref, dl_ref, qt_ref, o_ref = rest[:6]
        bufs = rest[6:]
        rings = tuple((bufs[r], bufs[nr + r], bufs[2 * nr + r], bufs[3 * nr + r]) for r in range(nr))
    else:
        sfin_ref, u_ref, dl_ref = rest[:3]
        bufs = rest[3:]
        rings = tuple((bufs[r], bufs[nr + r], None, None) for r in range(nr))
    c = GLA_CHUNK
    hp = 2 * GLA_DK
    vp = 2 * GLA_DV
    rows = min(t, GLA_BLOCK)
    cpb = rows // c
    n_blk = t // rows
    scale = GLA_DK ** -0.5

    dk_head0 = lax.broadcasted_iota(jnp.int32, (hp, 2 * c), 0) < GLA_DK
    ar = lax.broadcasted_iota(jnp.int32, (2 * c, 4 * c), 0)
    ac = lax.broadcasted_iota(jnp.int32, (2 * c, 4 * c), 1)
    same_chunk = (ar // c) == ((ac // c) & 1)
    keep_f = same_chunk & ((ar & (c - 1)) >= (ac & (c - 1)))
    keep_b = same_chunk & ((ar & (c - 1)) <= (ac & (c - 1)))
    wd = wd_ref[0]
    bd = bd_ref[0]

    def prepare(blk, ring):
        vb_ring, ke_ring, qt_ring, kt_ring = ring
        r0 = pl.multiple_of(blk * rows, rows)
        code = dec_ref[0, pl.ds(r0, rows), :].astype(BF16)
        logit = jnp.dot(code, wd, preferred_element_type=F32) + bd
        la = (jnp.minimum(logit, 0.0) - jnp.log(1.0 + jnp.exp(-jnp.abs(logit)))) * (1.0 / GLA_TAU)
        k = k_ref[0, pl.ds(r0, rows), :]
        vb_ring[...] = v_ref[0, pl.ds(r0, rows), :].astype(BF16)
        if with_output:
            q = q_ref[0, pl.ds(r0, rows), :] * scale
        for dirn in range(2):
            b, b_all = _chunk_cumsum(la[:, dirn * hp:(dirn + 1) * hp], reverse=dirn == 1)
            ke_ring[dirn] = (k * jnp.exp(b_all - b)).astype(BF16)
            for ci in range(cpb):
                dl_ref[dirn, blk * cpb + ci] = jnp.exp(b_all[ci * c:ci * c + SUBLANES, :])
            if with_output:
                qt = (q * jnp.exp(b)).astype(BF16)
                qt_ref[dirn, pl.ds(r0, rows), :] = qt
                qt_ring[dirn] = qt
                kt_ring[dirn] = (k * jnp.exp(-b)).T.astype(BF16)

    def multiply(blk, ring):
        vb_ring, ke_ring, qt_ring, kt_ring = ring
        r0 = pl.multiple_of(blk * rows, rows)
        vb = vb_ring[...]
        for dirn in range(2):
            ke = ke_ring[dirn]
            for ci in range(cpb):
                sl = slice(ci * c, (ci + 1) * c)
                upd = lax.dot_general(ke[sl], vb[sl], _TN, preferred_element_type=F32)
                u_ref[dirn, blk * cpb + ci, 0] = upd[0:GLA_DK, 0:GLA_DV]
                u_ref[dirn, blk * cpb + ci, 1] = upd[GLA_DK:hp, GLA_DV:vp]
            if not with_output:
                continue
            qt = qt_ring[dirn]
            kt_t = kt_ring[dirn]
            zt = jnp.zeros((hp, 2 * c), BF16)
            zv = jnp.zeros((2 * c, GLA_DV), BF16)
            parts = []
            for cp in range(cpb // 2):
                sl = slice(cp * 2 * c, (cp + 1) * 2 * c)
                kt_cp = kt_t[:, sl]
                kbd = jnp.concatenate([jnp.where(dk_head0, kt_cp, zt), jnp.where(dk_head0, zt, kt_cp)], axis=1)
                sc = jnp.dot(qt[sl], kbd, preferred_element_type=F32)
                p = jnp.where(keep_f if dirn == 0 else keep_b, sc, 0.0).astype(BF16)
                v_cp = vb[sl]
                vbd = jnp.concatenate(
                    [jnp.concatenate([v_cp[:, 0:GLA_DV], zv], axis=1),
                     jnp.concatenate([zv, v_cp[:, GLA_DV:vp]], axis=1)], axis=0)
                parts.append(jnp.dot(p, vbd, preferred_element_type=F32))
            o_intra = parts[0] if len(parts) == 1 else jnp.concatenate(parts, axis=0)
            if dirn == 0:
                o_ref[pl.ds(r0, rows), :] = o_intra
            else:
                o_ref[pl.ds(r0, rows), :] += o_intra

    prepare(0, rings[0])
    for i in range(n_blk):
        if i + 1 < n_blk:
            prepare(i + 1, rings[(i + 1) % nr])
        multiply(i, rings[i % nr])

    zs = jnp.zeros((GLA_DK, GLA_DV), BF16)

    def gate_block(r0):
        rs = pl.ds(r0, rows)
        o = o_ref[rs, :]
        og = og_ref[0, rs, :]
        g = g_ref[...]
        o0 = _rms(o[:, 0:GLA_DV], g)
        o1 = _rms(o[:, GLA_DV:vp], g)
        gate = og * jax.nn.sigmoid(og)
        y_ref[0, rs, :] = (jnp.concatenate([o0, o1], axis=1) * gate).astype(y_ref.dtype)

    for dirn in range(2):
        def seq_step(i, carry, dirn=dirn):
            s0, s1 = carry
            blk = i if dirn == 0 else n_blk - 1 - i
            r0 = pl.multiple_of(blk * rows, rows)
            for cj in range(cpb):
                ci = cj if dirn == 0 else cpb - 1 - cj
                chunk = blk * cpb + ci
                if with_output:
                    rs = pl.ds(r0 + ci * c, c)
                    sbd = jnp.concatenate(
                        [jnp.concatenate([s0.astype(BF16), zs], axis=1),
                         jnp.concatenate([zs, s1.astype(BF16)], axis=1)], axis=0)
                    o_ref[rs, :] += jnp.dot(qt_ref[dirn, rs, :], sbd, preferred_element_type=F32)
                drow = dl_ref[dirn, chunk][0:1, :]
                dcol = jnp.broadcast_to(drow, (hp, hp)).T
                s0 = s0 * dcol[0:GLA_DK, :] + u_ref[dirn, chunk, 0]
                s1 = s1 * dcol[GLA_DK:hp, :] + u_ref[dirn, chunk, 1]
            if with_output and dirn == 1:
                gate_block(r0)
            return s0, s1

        s0, s1 = lax.fori_loop(0, n_blk, seq_step, (s0_ref[0, dirn, 0], s0_ref[0, dirn, 1]), unroll=min(8 if dirn == 0 else 4, n_blk))
        sfin_ref[0, dirn, 0] = s0
        sfin_ref[0, dirn, 1] = s1


def _gla(zq, zk, zv, zog, zdec, l, wd_p, bd_p, g_gla, s0, with_output):
    bsz, t, _ = zq.shape
    hp, vp = 2 * GLA_DK, 2 * GLA_DV
    n_chunks = t // GLA_CHUNK
    seq = lambda wd: pl.BlockSpec((1, t, wd), lambda b, p: (b, 0, p))
    st_spec = pl.BlockSpec((1, 2, 2, GLA_DK, GLA_DV), lambda b, p: (b, 0, p, 0, 0))
    out_shape = [jax.ShapeDtypeStruct((bsz, 2, GLA_HEADS, GLA_DK, GLA_DV), F32)]
    out_specs = [st_spec]
    rows = min(t, GLA_BLOCK)
    nr = GLA_RINGS
    scratch = [pltpu.VMEM((2, n_chunks, 2, GLA_DK, GLA_DV), F32), pltpu.VMEM((2, n_chunks, SUBLANES, hp), F32)]
    staging = [pltpu.VMEM((rows, vp), BF16) for _ in range(nr)] + [pltpu.VMEM((2, rows, hp), BF16) for _ in range(nr)]
    if with_output:
        out_shape = [jax.ShapeDtypeStruct((bsz, t, GLA_V), BF16)] + out_shape
        out_specs = [seq(vp)] + out_specs
        scratch = scratch + [pltpu.VMEM((2, t, hp), BF16), pltpu.VMEM((t, vp), F32)]
        staging = staging + [pltpu.VMEM((2, rows, hp), BF16) for _ in range(nr)]
        staging = staging + [pltpu.VMEM((2, hp, rows), BF16) for _ in range(nr)]
    scratch = scratch + staging
    outs = pl.pallas_call(
        functools.partial(_gla_body, t=t, with_output=with_output),
        out_shape=out_shape,
        grid=(bsz, 2),
        in_specs=[
            seq(hp), seq(hp), seq(vp), seq(vp),
            pl.BlockSpec((1, t, DEC_PAD), lambda b, p: (b, 0, 0)),
            pl.BlockSpec((None, 1, DEC_PAD, 2 * hp), lambda b, p: (l, p, 0, 0)),
            pl.BlockSpec((None, 1, 1, 2 * hp), lambda b, p: (l, p, 0, 0)),
            _layer_spec(g_gla, l),
            st_spec,
        ],
        out_specs=out_specs,
        scratch_shapes=scratch,
        compiler_params=pltpu.CompilerParams(vmem_limit_bytes=VMEM_LIMIT),
        name="gla",
    )(zq, zk, zv, zog, zdec, wd_p, bd_p, g_gla, s0)
    if with_output:
        return outs[0], outs[1]
    return None, outs[0]


RG_BLOCK = 512


def _sigmoid_of_twice(x):
    return 0.5 * jnp.tanh(x) + 0.5


def _rg_body(rx_ref, rg_ref, wc_ref, bc_ref, wg_ref, bg_ref, lam_ref, h0_ref, *rest, t, line, with_output):
    if with_output:
        y_ref, hfin_ref, hf_ref, ab_ref, bb_ref = rest
    else:
        hfin_ref, hf_ref, ab_ref, bb_ref = rest
    w = RG_W
    rows = min(t, RG_BLOCK)
    n_blk = t // rows
    grp = SUBLANES
    gpb = rows // grp
    pos = lax.broadcasted_iota(jnp.int32, (rows, w), 0) & (line - 1)
    sub = lax.broadcasted_iota(jnp.int32, (rows // grp, grp, w), 1)
    wc = wc_ref[...]
    bc = bc_ref[...]
    wg = wg_ref[...]
    bg = bg_ref[...]
    sp = [jax.nn.softplus(-lam_ref[d:d + 1, :]) * (-RG_C) for d in range(2)]

    def pre_step(blk, hf):
        r0 = pl.multiple_of(blk * rows, rows)
        rs = pl.ds(r0, rows)
        x = rx_ref[0, rs, :]
        xm1 = jnp.where(pos >= 1, pltpu.roll(x, 1, 0), 0.0)
        xp1 = jnp.where(pos <= line - 2, pltpu.roll(x, rows - 1, 0), 0.0)
        xp2 = jnp.where(pos <= line - 3, pltpu.roll(x, rows - 2, 0), 0.0)
        u = bc + xm1 * wc[0:1, :] + x * wc[1:2, :] + xp1 * wc[2:3, :] + xp2 * wc[3:4, :]
        gates = jnp.dot(u.astype(BF16), wg, preferred_element_type=F32) + bg
        coef = []
        for dirn in range(2):
            r = _sigmoid_of_twice(gates[:, (2 * dirn) * w:(2 * dirn + 1) * w])
            i = _sigmoid_of_twice(gates[:, (2 * dirn + 1) * w:(2 * dirn + 2) * w])
            log_a = r * sp[dirn]
            a = jnp.exp(log_a)
            bx = jnp.sqrt(-jnp.tanh(log_a) * (a * a + 1.0)) * (i * u)
            a3 = a.reshape(rows // grp, grp, w)
            b3 = bx.reshape(rows // grp, grp, w)
            for s in (1, 2, 4):
                if dirn == 0:
                    ok = sub >= s
                    sh = s
                else:
                    ok = sub <= grp - 1 - s
                    sh = grp - s
                b3 = jnp.where(ok, a3 * pltpu.roll(b3, sh, 1) + b3, b3)
                a3 = jnp.where(ok, a3 * pltpu.roll(a3, sh, 1), a3)
            coef.append((a3.reshape(rows, w), b3.reshape(rows, w)))
        ab_ref[rs, :] = coef[1][0]
        bb_ref[rs, :] = coef[1][1]
        af, bf = coef[0]
        for gi in range(gpb):
            g0 = gi * grp
            hfull = bf[g0:g0 + grp, :] + af[g0:g0 + grp, :] * hf
            hf_ref[pl.ds(r0 + g0, grp), :] = hfull
            hf = hfull[grp - 1:grp, :]
        return hf

    hf = lax.fori_loop(0, n_blk, pre_step, h0_ref[0, 0:1, :])

    def bwd_step(j, hb):
        blk = n_blk - 1 - j
        r0 = pl.multiple_of(blk * rows, rows)
        parts = []
        for gi in range(gpb - 1, -1, -1):
            g0 = r0 + gi * grp
            hbull = bb_ref[pl.ds(g0, grp), :] + ab_ref[pl.ds(g0, grp), :] * hb
            hb = hbull[0:1, :]
            parts.append(hbull)
        if with_output:
            hbwd = jnp.concatenate(parts[::-1], axis=0)
            rs = pl.ds(r0, rows)
            h = hf_ref[rs, :] + hbwd
            y_ref[0, rs, :] = (h * jax.nn.gelu(rg_ref[0, rs, :])).astype(y_ref.dtype)
        return hb

    hb = lax.fori_loop(0, n_blk, bwd_step, h0_ref[0, 1:2, :])
    hfin_ref[0, 0:1, :] = hf
    hfin_ref[0, 1:2, :] = hb
    hfin_ref[0, 2:grp, :] = jnp.zeros((grp - 2, w), F32)


def _rg(zrx, zrg, l, w_conv, b_conv, w_gate, b_gate, lam, h0, line, with_output):
    bsz, t, w = zrx.shape
    seq = pl.BlockSpec((1, t, w), lambda b: (b, 0, 0))
    st_spec = pl.BlockSpec((1, SUBLANES, w), lambda b: (b, 0, 0))
    out_shape = [jax.ShapeDtypeStruct((bsz, SUBLANES, w), F32)]
    out_specs = [st_spec]
    if with_output:
        out_shape = [jax.ShapeDtypeStruct((bsz, t, w), BF16)] + out_shape
        out_specs = [seq] + out_specs
    outs = pl.pallas_call(
        functools.partial(_rg_body, t=t, line=line, with_output=with_output),
        out_shape=out_shape,
        grid=(bsz,),
        in_specs=[
            seq, seq,
            _layer_spec(w_conv, l), _layer_spec(b_conv, l),
            _layer_spec(w_gate, l), _layer_spec(b_gate, l),
            _layer_spec(lam, l),
            st_spec,
        ],
        out_specs=out_specs,
        scratch_shapes=[pltpu.VMEM((t, w), F32) for _ in range(3)],
        compiler_params=pltpu.CompilerParams(vmem_limit_bytes=VMEM_LIMIT),
        name="rglru",
    )(zrx, zrg, w_conv, b_conv, w_gate, b_gate, lam, h0)
    if with_output:
        return outs[0], outs[1]
    return None, outs[0]


FFT_PITCH = 72


def _dft_mats(n):
    j = np.arange(n)
    ang = 2.0 * np.pi * ((j[:, None] * j[None, :]) % n) / n
    return np.cos(ang), np.sin(ang)


def _fourier_consts(t):
    n2 = GRID_W
    n1 = t // n2
    cc, sc = _dft_mats(FNET_GD)
    eye2 = np.eye(LANES // FNET_GD)
    m3 = np.concatenate([np.kron(eye2, cc), -np.kron(eye2, sc)], axis=0) / 8.0
    c2, s2 = _dft_mats(n2)
    consts = {"m3": jnp.asarray(m3, F32)}
    if n1 == n2:
        consts["cs"] = jnp.asarray(np.concatenate([c2, s2], axis=0) / 8.0, F32)
        consts["m2"] = jnp.asarray(np.block([[c2, -s2], [s2, c2]]) / 8.0, F32)
        wk = (np.arange(n2)[:, None] * np.arange(n1)[None, :]).reshape(-1)
        ang = 2.0 * np.pi * wk / t
        consts["twc"] = jnp.asarray(np.repeat(np.cos(ang)[:, None], LANES, axis=1), F32)
        consts["tws"] = jnp.asarray(np.repeat(np.sin(ang)[:, None], LANES, axis=1), F32)
    else:
        ct, st = _dft_mats(t)
        consts["cs"] = jnp.asarray(np.concatenate([ct, st], axis=0) / np.sqrt(t), F32)
    return consts


def _fft_body(xa_ref, xb_ref, cs_ref, m2_ref, m3_ref, twc_ref, tws_ref, y_ref, *scr, t):
    n = GRID_W
    p = FFT_PITCH
    cs = cs_ref[...].astype(BF16)
    m2 = m2_ref[...].astype(BF16)
    m3 = m3_ref[...].astype(BF16)
    halves = ((xa_ref,) + tuple(scr[0:4]), (xb_ref,) + tuple(scr[4:8]))

    def stage1(h, i):
        x_ref, sr_ref, si_ref, _, _ = halves[h]
        w0 = 2 * i
        xw = jnp.concatenate([x_ref[0, pl.ds(w0 + j, n, stride=p), :] for j in range(2)], axis=1)
        a = jnp.dot(cs, xw.astype(BF16), preferred_element_type=F32)
        for j in range(2):
            ar = a[0:n, j * LANES:(j + 1) * LANES]
            ai = a[n:2 * n, j * LANES:(j + 1) * LANES]
            r0 = pl.multiple_of((w0 + j) * n, n)
            tc = twc_ref[pl.ds(r0, n), :]
            ts = tws_ref[pl.ds(r0, n), :]
            sr_ref[pl.ds(w0 + j, n, stride=p), :] = ar * tc - ai * ts
            si_ref[pl.ds(w0 + j, n, stride=p), :] = ar * ts + ai * tc

    def stage2(h, i):
        _, sr_ref, si_ref, dr_ref, di_ref = halves[h]
        k0 = 2 * i
        cols = []
        for j in range(2):
            r0 = pl.multiple_of((k0 + j) * p, SUBLANES)
            cols.append(jnp.concatenate([sr_ref[pl.ds(r0, n), :], si_ref[pl.ds(r0, n), :]], axis=0))
        xb = jnp.concatenate(cols, axis=1).astype(BF16)
        d = jnp.dot(m2, xb, preferred_element_type=F32)
        for j in range(2):
            dr_ref[pl.ds(k0 + j, n, stride=p), :] = d[0:n, j * LANES:(j + 1) * LANES]
            di_ref[pl.ds(k0 + j, n, stride=p), :] = d[n:2 * n, j * LANES:(j + 1) * LANES]

    per = 8

    def stage3(h, i):
        _, _, _, dr_ref, di_ref = halves[h]
        blocks = []
        for j in range(per):
            r0 = pl.multiple_of((i * per + j) * p, SUBLANES)
            blocks.append(jnp.concatenate([dr_ref[pl.ds(r0, n), :], di_ref[pl.ds(r0, n), :]], axis=1))
        d = jnp.concatenate(blocks, axis=0).astype(BF16)
        rs = pl.ds(pl.multiple_of(i * per * n, per * n), per * n)
        y_ref[0, rs, h * LANES:(h + 1) * LANES] = jnp.dot(d, m3, preferred_element_type=F32).astype(y_ref.dtype)

    def loop(trips, unroll, *stages):
        def body(i, carry):
            for fn in stages:
                fn(i)
            return carry
        lax.fori_loop(0, trips, body, 0, unroll=unroll)

    ratio = (n // 2) // (n // per)
    loop(n // 2, 8, lambda i: stage1(0, i))
    loop(n // 2, 8, lambda i: stage2(0, i), lambda i: stage1(1, i))
    loop(n // per, 2, lambda i: stage3(0, i), lambda i: [stage2(1, ratio * i + k) for k in range(ratio)])
    loop(n // per, 2, lambda i: stage3(1, i))


def _dense_dft_body(x_ref, cs_ref, m3_ref, y_ref, *, t):
    a = jnp.dot(cs_ref[...].astype(BF16), x_ref[0].astype(BF16), preferred_element_type=F32)
    d = jnp.concatenate([a[0:t], a[t:2 * t]], axis=1).astype(BF16)
    y_ref[0] = jnp.dot(d, m3_ref[...].astype(BF16), preferred_element_type=F32).astype(y_ref.dtype)


def _fourier(zf, consts, t):
    bsz, t_in, w = zf.shape
    seq = pl.BlockSpec((1, t, LANES), lambda b, j: (b, 0, j))
    seq_in = pl.BlockSpec((1, t_in, LANES), lambda b, j: (b, 0, j))
    common = dict(
        out_shape=jax.ShapeDtypeStruct((bsz, t, w), BF16),
        grid=(bsz, w // LANES),
        out_specs=seq,
        compiler_params=pltpu.CompilerParams(vmem_limit_bytes=VMEM_LIMIT),
    )
    if "m2" in consts:
        full = pl.BlockSpec((1, t, w), lambda b: (b, 0, 0))
        half = lambda j: pl.BlockSpec((1, t_in, LANES), lambda b: (b, 0, j))
        return pl.pallas_call(
            functools.partial(_fft_body, t=t),
            out_shape=jax.ShapeDtypeStruct((bsz, t, w), BF16),
            grid=(bsz,),
            in_specs=[half(0), half(1), _const_spec(consts["cs"].shape), _const_spec(consts["m2"].shape),
                      _const_spec(consts["m3"].shape), _const_spec((t, LANES)), _const_spec((t, LANES))],
            out_specs=full,
            scratch_shapes=[pltpu.VMEM((GRID_W * FFT_PITCH, LANES), F32) for _ in range(8)],
            compiler_params=pltpu.CompilerParams(vmem_limit_bytes=VMEM_LIMIT),
            name="fourier_fft",
        )(zf, zf, consts["cs"], consts["m2"], consts["m3"], consts["twc"], consts["tws"])
    return pl.pallas_call(
        functools.partial(_dense_dft_body, t=t),
        in_specs=[seq_in, _const_spec(consts["cs"].shape), _const_spec(consts["m3"].shape)],
        name="fourier_dense", **common,
    )(zf, consts["cs"], consts["m3"])


FF_PIECES = (1536, 1280)


def _outffn_body(yg_ref, yf_ref, yr_ref, x_ref, mod_ref, gpm_ref, gpf_ref, gqf_ref,
                 wo_ref, wg_ref, wu_ref, wd_ref, o_ref):
    tm = x_ref.shape[1]
    sub = min(tm, SUB_ROWS)
    subs = [slice(r * sub, (r + 1) * sub) for r in range(tm // sub)]
    mixes = []
    for rs in subs:
        y = jnp.concatenate([yg_ref[0, rs, :], yf_ref[0, rs, :], yr_ref[0, rs, :]], axis=1)
        mixes.append(jnp.dot(y, wo_ref[...], preferred_element_type=F32))
    x1s, hfs = [], []
    for rs, mix in zip(subs, mixes):
        x1 = x_ref[0, rs, :] + mod_ref[0, 2:3, :] * _rms(mix, gpm_ref[...])
        x1s.append(x1)
        hfs.append((_rms(x1, gpf_ref[...]) * (1.0 + mod_ref[0, 4:5, :]) + mod_ref[0, 3:4, :]).astype(BF16))
    for rs, x1, hf in zip(subs, x1s, hfs):
        acc = None
        off = 0
        for fw in FF_PIECES:
            gate = jnp.dot(hf, wg_ref[:, off:off + fw], preferred_element_type=F32)
            up = jnp.dot(hf, wu_ref[:, off:off + fw], preferred_element_type=F32)
            act = (gate * jax.nn.sigmoid(gate) * up).astype(BF16)
            part = jnp.dot(act, wd_ref[off:off + fw, :], preferred_element_type=F32)
            acc = part if acc is None else acc + part
            off += fw
        o_ref[0, rs, :] = x1 + mod_ref[0, 5:6, :] * _rms(acc, gqf_ref[...])


def _outffn(yg, yf, yr, x, mod, mod_row, l, gpm, gpf, gqf, wo, wg, wu, wd, tm):
    bsz, t, d = x.shape
    if mod_row is None:
        mod_map = lambda b, i: (l, b, 0, 0)
    else:
        mod_map = lambda b, i: (l, mod_row, 0, 0)
    tile = lambda wdt: pl.BlockSpec((1, tm, wdt), lambda b, i: (b, i, 0))
    return pl.pallas_call(
        _outffn_body,
        out_shape=jax.ShapeDtypeStruct((bsz, t, d), F32),
        grid=(bsz, t // tm),
        in_specs=[
            tile(GLA_V), tile(FNET_W), tile(RG_W), tile(d),
            pl.BlockSpec((None, 1, 6, d), mod_map),
            _layer_spec(gpm, l), _layer_spec(gpf, l), _layer_spec(gqf, l),
            _layer_spec(wo, l), _layer_spec(wg, l), _layer_spec(wu, l), _layer_spec(wd, l),
        ],
        out_specs=tile(d),
        compiler_params=pltpu.CompilerParams(vmem_limit_bytes=VMEM_LIMIT),
        name="outffn",
    )(yg, yf, yr, x, mod, gpm, gpf, gqf, wo, wg, wu, wd)


def _prep_w_in(w_in):
    off_dec = 2 * GLA_QK + GLA_V
    off_og = off_dec + 2 * GLA_RANK
    pad = jnp.zeros(w_in.shape[:2] + (DEC_PAD - 2 * GLA_RANK,), w_in.dtype)
    return jnp.concatenate([w_in[..., 0:off_dec], w_in[..., off_og:], w_in[..., off_dec:off_og], pad], axis=-1).astype(BF16)


def _prep_dec(w_dec, b_dec):
    depth = w_dec.shape[0]
    hp = 2 * GLA_DK
    wr = w_dec.reshape(depth, 2, GLA_RANK, 2, hp)
    wbd = jnp.einsum("ldrpj,de->lpdrej", wr, jnp.eye(2, dtype=w_dec.dtype))
    wbd = wbd.reshape(depth, 2, 2 * GLA_RANK, 2 * hp)
    wbd = jnp.pad(wbd, ((0, 0), (0, 0), (0, DEC_PAD - 2 * GLA_RANK), (0, 0)))
    bd = b_dec.reshape(depth, 2, 2, hp).transpose(0, 2, 1, 3).reshape(depth, 2, 1, 2 * hp)
    return wbd.astype(BF16), bd


def _block_diag(w):
    h, hd = w.shape[-3], w.shape[-2]
    out = jnp.einsum("...hij,hg->...higj", w, jnp.eye(h, dtype=w.dtype))
    return out.reshape(w.shape[:-3] + (h * hd, h * hd))


def _prep_rg(w_a, b_a, w_x, b_x):
    bda, bdx = _block_diag(w_a), _block_diag(w_x)
    wg = jnp.concatenate([bda[:, 0], bdx[:, 0], bda[:, 1], bdx[:, 1]], axis=-1)
    bg = jnp.concatenate([b_a[:, 0], b_x[:, 0], b_a[:, 1], b_x[:, 1]], axis=-1)[:, None, :]
    return (0.5 * wg).astype(BF16), 0.5 * bg


def _mixer(h, mod, mod_row, l, lw, consts, s0, h0, line, tm, with_output):
    t = h.shape[1]
    f_pitch = FFT_PITCH if "m2" in consts else GRID_W
    zq, zk, zv, zog, zf, zrx, zrg, zdec = _inproj(h, mod, mod_row, l, lw["g_pre_mix"], lw["w_in"], tm, f_pitch)
    yg, s_fin = _gla(zq, zk, zv, zog, zdec, l, lw["wd"], lw["bd"], lw["g_gla"], s0, with_output)
    yr, h_fin = _rg(zrx, zrg, l, lw["w_conv"], lw["b_conv"], lw["w_gate"], lw["b_gate"], lw["lam"], h0, line, with_output)
    yf = _fourier(zf, consts, t) if with_output else None
    return (yg, yf, yr), s_fin, h_fin


def kernel(x, c, ctx, c_ctx, w_ada, b_ada, g_pre_mix, g_post_mix, g_pre_ffn, g_post_ffn, w_in, w_dec, b_dec, g_gla,
           w_conv, b_conv, w_rg_a, b_rg_a, w_rg_x, b_rg_x, rg_lam, w_out, w_ffn_gate, w_ffn_up, w_ffn_down):
    bsz, t, d = x.shape
    t_ctx = ctx.shape[1]
    depth = w_ada.shape[0]

    c_all = jnp.concatenate([c, c_ctx[None, :], jnp.zeros((MOD_ROWS - bsz - 1, d), F32)], axis=0)
    mod = _ada(c_all, w_ada, b_ada).reshape(depth, MOD_ROWS, 6, d)

    consts_lat = _fourier_consts(t)
    consts_ctx = _fourier_consts(t_ctx)
    s_zero = jnp.zeros((bsz, 2, GLA_HEADS, GLA_DK, GLA_DV), F32)
    h_zero = jnp.zeros((bsz, SUBLANES, RG_W), F32)
    rows = lambda v: v[:, None, :]

    wd_p, bd_p = _prep_dec(w_dec, b_dec)
    w_gate, b_gate = _prep_rg(w_rg_a, b_rg_a, w_rg_x, b_rg_x)
    lw = dict(g_pre_mix=rows(g_pre_mix), w_in=_prep_w_in(w_in), wd=wd_p, bd=bd_p, g_gla=rows(g_gla),
              w_conv=w_conv, b_conv=rows(b_conv), w_gate=w_gate, b_gate=b_gate, lam=rg_lam)
    ffn = (rows(g_post_mix), rows(g_pre_ffn), rows(g_post_ffn), w_out.astype(BF16),
           w_ffn_gate.astype(BF16), w_ffn_up.astype(BF16), w_ffn_down.astype(BF16))

    tm = min(t, ROW_TILE)
    assert t % tm == 0 and tm % GRID_W == 0 and t_ctx % GRID_W == 0
    h_ctx = ctx
    for l in range(depth):
        last = l == depth - 1
        ys_c, s_ctx, hs_ctx = _mixer(h_ctx, mod, bsz, l, lw, consts_ctx, s_zero, h_zero, t_ctx, t_ctx, not last)
        ys_l, _, _ = _mixer(x, mod, None, l, lw, consts_lat, s_ctx, hs_ctx, GRID_W, tm, True)
        x = _outffn(*ys_l, x, mod, None, l, *ffn, tm)
        if not last:
            h_ctx = _outffn(*ys_c, h_ctx, mod, bsz, l, *ffn, t_ctx)
    return x
```

```python
import functools

import jax
import jax.numpy as jnp
import numpy as np
from jax import lax
from jax.experimental import pallas as pl
from jax.experimental.pallas import tpu as pltpu

F32 = jnp.float32
BF16 = jnp.bfloat16

D_MODEL = 1024
DEPTH = 4
GRID_W = 64
EPS = 1e-6

GLA_HEADS = 4
GLA_DK = 64
GLA_DV = 128
GLA_QK = GLA_HEADS * GLA_DK
GLA_V = GLA_HEADS * GLA_DV
GLA_RANK = 16
GLA_TAU = 16.0
GLA_CHUNK = 64
FNET_W = 256
FNET_GD = 64
RG_HEADS = 4
RG_HD = 64
RG_W = 256
RG_C = 8.0
CONV_W = 4
D_FF = 2816

LANES = 128
SUBLANES = 8
DEC_PAD = LANES
N_IN_PAD = 2 * GLA_QK + 2 * GLA_V + FNET_W + 2 * RG_W + DEC_PAD
MOD_ROWS = 16
VMEM_LIMIT = 56 * 1024 * 1024

_NT = (((1,), (1,)), ((), ()))
_TN = (((0,), (0,)), ((), ()))


def _rms(x, g):
    return x * lax.rsqrt(jnp.mean(x * x, axis=-1, keepdims=True) + EPS) * g


def _const_spec(shape):
    nd = len(shape)
    return pl.BlockSpec(shape, lambda *_: (0,) * nd, pipeline_mode=pl.Buffered(1))


def _layer_spec(arr, l):
    nd = arr.ndim - 1
    return pl.BlockSpec((None,) + arr.shape[1:], lambda *_: (l,) + (0,) * nd, pipeline_mode=pl.Buffered(1))


def _ada_body(c_ref, w_ref, b_ref, o_ref):
    c = c_ref[...]
    s = (c * jax.nn.sigmoid(c)).astype(BF16)
    o_ref[0] = jnp.dot(s, w_ref[0].astype(BF16), preferred_element_type=F32) + b_ref[0]


def _ada(c_all, w_ada, b_ada):
    depth, d, n = w_ada.shape
    tn = 1536
    return pl.pallas_call(
        _ada_body,
        out_shape=jax.ShapeDtypeStruct((depth, MOD_ROWS, n), F32),
        grid=(depth, n // tn),
        in_specs=[
            pl.BlockSpec((MOD_ROWS, d), lambda l, j: (0, 0)),
            pl.BlockSpec((1, d, tn), lambda l, j: (l, 0, j)),
            pl.BlockSpec((1, 1, tn), lambda l, j: (l, 0, j)),
        ],
        out_specs=pl.BlockSpec((1, MOD_ROWS, tn), lambda l, j: (l, 0, j)),
        compiler_params=pltpu.CompilerParams(vmem_limit_bytes=VMEM_LIMIT),
        name="ada",
    )(c_all, w_ada, b_ada.reshape(depth, 1, n))


ROW_TILE = 1024
SUB_ROWS = 256

_IN_WIDTHS = (GLA_QK, GLA_QK, GLA_V, GLA_V, FNET_W, RG_W, RG_W, DEC_PAD)
_IN_F = 4


def _inproj_body(x_ref, mod_ref, g_ref, w_ref, *out_refs, f_pitch):
    tm = x_ref.shape[1]
    sub = min(tm, SUB_ROWS)
    for r in range(tm // sub):
        rs = slice(r * sub, (r + 1) * sub)
        h = _rms(x_ref[0, rs, :], g_ref[...]) * (1.0 + mod_ref[0, 1:2, :]) + mod_ref[0, 0:1, :]
        h = h.astype(BF16)
        off = 0
        for idx, (o_ref, wd) in enumerate(zip(out_refs, _IN_WIDTHS)):
            z = jnp.dot(h, w_ref[:, off:off + wd], preferred_element_type=F32)
            off += wd
            if idx == _IN_F and f_pitch != GRID_W:
                pad = jnp.zeros((f_pitch - GRID_W, wd), F32)
                for ln in range(sub // GRID_W):
                    base = (r * (sub // GRID_W) + ln) * f_pitch
                    o_ref[0, base:base + GRID_W, :] = z[ln * GRID_W:(ln + 1) * GRID_W, :]
                    o_ref[0, base + GRID_W:base + f_pitch, :] = pad
            else:
                o_ref[0, rs, :] = z


def _inproj(x, mod, mod_row, l, g, w_in_p, tm, f_pitch):
    bsz, t, d = x.shape
    if mod_row is None:
        mod_map = lambda b, i: (l, b, 0, 0)
    else:
        mod_map = lambda b, i: (l, mod_row, 0, 0)
    rows_of = lambda idx, n: n // GRID_W * f_pitch if idx == _IN_F else n
    return pl.pallas_call(
        functools.partial(_inproj_body, f_pitch=f_pitch),
        out_shape=[jax.ShapeDtypeStruct((bsz, rows_of(i, t), wd), F32) for i, wd in enumerate(_IN_WIDTHS)],
        grid=(bsz, t // tm),
        in_specs=[
            pl.BlockSpec((1, tm, d), lambda b, i: (b, i, 0)),
            pl.BlockSpec((None, 1, 6, d), mod_map),
            _layer_spec(g, l),
            _layer_spec(w_in_p, l),
        ],
        out_specs=[pl.BlockSpec((1, rows_of(i, tm), wd), lambda b, i: (b, i, 0)) for i, wd in enumerate(_IN_WIDTHS)],
        compiler_params=pltpu.CompilerParams(vmem_limit_bytes=VMEM_LIMIT),
        name="inproj",
    )(x, mod, g, w_in_p)


GLA_BLOCK = 256
GLA_RINGS = 8


def _chunk_cumsum(x, reverse):
    rows, lanes = x.shape
    g = SUBLANES
    c = GLA_CHUNK
    x3 = x.reshape(rows // g, g, lanes)
    sub = lax.broadcasted_iota(jnp.int32, x3.shape, 1)
    for s in (1, 2, 4):
        if reverse:
            x3 = x3 + jnp.where(sub < g - s, pltpu.roll(x3, g - s, 1), 0.0)
        else:
            x3 = x3 + jnp.where(sub >= s, pltpu.roll(x3, s, 1), 0.0)
    edge = x3[:, 0:1, :] if reverse else x3[:, g - 1:g, :]
    tot = jnp.broadcast_to(edge, x3.shape).reshape(rows, lanes)
    pos = lax.broadcasted_iota(jnp.int32, (rows, lanes), 0) & (c - 1)
    acc = tot
    for s in (8, 16, 32):
        if reverse:
            acc = acc + jnp.where(pos < c - s, pltpu.roll(acc, rows - s, 0), 0.0)
        else:
            acc = acc + jnp.where(pos >= s, pltpu.roll(acc, s, 0), 0.0)
    sums = x3.reshape(rows, lanes) + (acc - tot)
    parts = []
    for ci in range(rows // c):
        r = ci * c if reverse else ci * c + c - 1
        parts.append(jnp.broadcast_to(acc[r:r + 1, :], (c, lanes)))
    total = parts[0] if len(parts) == 1 else jnp.concatenate(parts, axis=0)
    return sums, total


def _gla_body(q_ref, k_ref, v_ref, og_ref, dec_ref, wd_ref, bd_ref, g_ref, s0_ref, *rest, t, with_output):
    nr = GLA_RINGS
    if with_output:
        y_ref, sfin_ref, u_ref, dl_ref, qt_ref, o_ref = rest[:6]
        bufs = rest[6:]
        rings = tuple((bufs[r], bufs[nr + r], bufs[2 * nr + r], bufs[3 * nr + r]) for r in range(nr))
    else:
        sfin_ref, u_ref, dl_ref = rest[:3]
        bufs = rest[3:]
        rings = tuple((bufs[r], bufs[nr + r], None, None) for r in range(nr))
    c = GLA_CHUNK
    hp = 2 * GLA_DK
    vp = 2 * GLA_DV
    rows = min(t, GLA_BLOCK)
    cpb = rows // c
    n_blk = t // rows
    scale = GLA_DK ** -0.5

    dk_head0 = lax.broadcasted_iota(jnp.int32, (hp, 2 * c), 0) < GLA_DK
    ar = lax.broadcasted_iota(jnp.int32, (2 * c, 4 * c), 0)
    ac = lax.broadcasted_iota(jnp.int32, (2 * c, 4 * c), 1)
    same_chunk = (ar // c) == ((ac // c) & 1)
    keep_f = same_chunk & ((ar & (c - 1)) >= (ac & (c - 1)))
    keep_b = same_chunk & ((ar & (c - 1)) <= (ac & (c - 1)))
    wd = wd_ref[0]
    bd = bd_ref[0]

    def prepare(blk, ring):
        vb_ring, ke_ring, qt_ring, kt_ring = ring
        r0 = pl.multiple_of(blk * rows, rows)
        code = dec_ref[0, pl.ds(r0, rows), :].astype(BF16)
        logit = jnp.dot(code, wd, preferred_element_type=F32) + bd
        la = (jnp.minimum(logit, 0.0) - jnp.log(1.0 + jnp.exp(-jnp.abs(logit)))) * (1.0 / GLA_TAU)
        k = k_ref[0, pl.ds(r0, rows), :]
        vb_ring[...] = v_ref[0, pl.ds(r0, rows), :].astype(BF16)
        if with_output:
            q = q_ref[0, pl.ds(r0, rows), :] * scale
        for dirn in range(2):
            b, b_all = _chunk_cumsum(la[:, dirn * hp:(dirn + 1) * hp], reverse=dirn == 1)
            ke_ring[dirn] = (k * jnp.exp(b_all - b)).astype(BF16)
            for ci in range(cpb):
                dl_ref[dirn, blk * cpb + ci] = jnp.exp(b_all[ci * c:ci * c + SUBLANES, :])
            if with_output:
                qt = (q * jnp.exp(b)).astype(BF16)
                qt_ref[dirn, pl.ds(r0, rows), :] = qt
                qt_ring[dirn] = qt
                kt_ring[dirn] = (k * jnp.exp(-b)).T.astype(BF16)

    def multiply(blk, ring):
        vb_ring, ke_ring, qt_ring, kt_ring = ring
        r0 = pl.multiple_of(blk * rows, rows)
        vb = vb_ring[...]
        for dirn in range(2):
            ke = ke_ring[dirn]
            for ci in range(cpb):
                sl = slice(ci * c, (ci + 1) * c)
                upd = lax.dot_general(ke[sl], vb[sl], _TN, preferred_element_type=F32)
                u_ref[dirn, blk * cpb + ci, 0] = upd[0:GLA_DK, 0:GLA_DV]
                u_ref[dirn, blk * cpb + ci, 1] = upd[GLA_DK:hp, GLA_DV:vp]
            if not with_output:
                continue
            qt = qt_ring[dirn]
            kt_t = kt_ring[dirn]
            zt = jnp.zeros((hp, 2 * c), BF16)
            zv = jnp.zeros((2 * c, GLA_DV), BF16)
            parts = []
            for cp in range(cpb // 2):
                sl = slice(cp * 2 * c, (cp + 1) * 2 * c)
                kt_cp = kt_t[:, sl]
                kbd = jnp.concatenate([jnp.where(dk_head0, kt_cp, zt), jnp.where(dk_head0, zt, kt_cp)], axis=1)
                sc = jnp.dot(qt[sl], kbd, preferred_element_type=F32)
                p = jnp.where(keep_f if dirn == 0 else keep_b, sc, 0.0).astype(BF16)
                v_cp = vb[sl]
                vbd = jnp.concatenate(
                    [jnp.concatenate([v_cp[:, 0:GLA_DV], zv], axis=1),
                     jnp.concatenate([zv, v_cp[:, GLA_DV:vp]], axis=1)], axis=0)
                parts.append(jnp.dot(p, vbd, preferred_element_type=F32))
            o_intra = parts[0] if len(parts) == 1 else jnp.concatenate(parts, axis=0)
            if dirn == 0:
                o_ref[pl.ds(r0, rows), :] = o_intra
            else:
                o_ref[pl.ds(r0, rows), :] += o_intra

    zs = jnp.zeros((GLA_DK, GLA_DV), BF16)

    def sweep_block(dirn, blk, s0, s1):
        r0 = blk * rows if isinstance(blk, int) else pl.multiple_of(blk * rows, rows)
        for cj in range(cpb):
            ci = cj if dirn == 0 else cpb - 1 - cj
            chunk = blk * cpb + ci
            if with_output:
                rs = pl.ds(r0 + ci * c, c)
                sbd = jnp.concatenate(
                    [jnp.concatenate([s0.astype(BF16), zs], axis=1),
                     jnp.concatenate([zs, s1.astype(BF16)], axis=1)], axis=0)
                o_ref[rs, :] += jnp.dot(qt_ref[dirn, rs, :], sbd, preferred_element_type=F32)
            drow = dl_ref[dirn, chunk][0:1, :]
            dcol = jnp.broadcast_to(drow, (hp, hp)).T
            s0 = s0 * dcol[0:GLA_DK, :] + u_ref[dirn, chunk, 0]
            s1 = s1 * dcol[GLA_DK:hp, :] + u_ref[dirn, chunk, 1]
        return s0, s1

    def gate_block(blk):
        rs = pl.ds(pl.multiple_of(blk * rows, rows), rows)
        o = o_ref[rs, :]
        og = og_ref[0, rs, :]
        g = g_ref[...]
        o0 = _rms(o[:, 0:GLA_DV], g)
        o1 = _rms(o[:, GLA_DV:vp], g)
        gate = og * jax.nn.sigmoid(og)
        y_ref[0, rs, :] = (jnp.concatenate([o0, o1], axis=1) * gate).astype(y_ref.dtype)

    s0, s1 = s0_ref[0, 0, 0], s0_ref[0, 0, 1]
    prepare(0, rings[0])
    for i in range(n_blk):
        if i + 1 < n_blk:
            prepare(i + 1, rings[(i + 1) % nr])
        multiply(i, rings[i % nr])
        if i > 0:
            s0, s1 = sweep_block(0, i - 1, s0, s1)
    s0, s1 = sweep_block(0, n_blk - 1, s0, s1)
    sfin_ref[0, 0, 0] = s0
    sfin_ref[0, 0, 1] = s1

    def bwd_step(i, carry):
        blk = n_blk - 1 - i
        s0, s1 = sweep_block(1, blk, *carry)
        if with_output:
            gate_block(blk)
        return s0, s1

    s0, s1 = lax.fori_loop(0, n_blk, bwd_step, (s0_ref[0, 1, 0], s0_ref[0, 1, 1]), unroll=min(4, n_blk))
    sfin_ref[0, 1, 0] = s0
    sfin_ref[0, 1, 1] = s1


def _gla(zq, zk, zv, zog, zdec, l, wd_p, bd_p, g_gla, s0, with_output):
    bsz, t, _ = zq.shape
    hp, vp = 2 * GLA_DK, 2 * GLA_DV
    n_chunks = t // GLA_CHUNK
    seq = lambda wd: pl.BlockSpec((1, t, wd), lambda b, p: (b, 0, p))
    st_spec = pl.BlockSpec((1, 2, 2, GLA_DK, GLA_DV), lambda b, p: (b, 0, p, 0, 0))
    out_shape = [jax.ShapeDtypeStruct((bsz, 2, GLA_HEADS, GLA_DK, GLA_DV), F32)]
    out_specs = [st_spec]
    rows = min(t, GLA_BLOCK)
    nr = GLA_RINGS
    scratch = [pltpu.VMEM((2, n_chunks, 2, GLA_DK, GLA_DV), F32), pltpu.VMEM((2, n_chunks, SUBLANES, hp), F32)]
    staging = [pltpu.VMEM((rows, vp), BF16) for _ in range(nr)] + [pltpu.VMEM((2, rows, hp), BF16) for _ in range(nr)]
    if with_output:
        out_shape = [jax.ShapeDtypeStruct((bsz, t, GLA_V), BF16)] + out_shape
        out_specs = [seq(vp)] + out_specs
        scratch = scratch + [pltpu.VMEM((2, t, hp), BF16), pltpu.VMEM((t, vp), F32)]
        staging = staging + [pltpu.VMEM((2, rows, hp), BF16) for _ in range(nr)]
        staging = staging + [pltpu.VMEM((2, hp, rows), BF16) for _ in range(nr)]
    scratch = scratch + staging
    outs = pl.pallas_call(
        functools.partial(_gla_body, t=t, with_output=with_output),
        out_shape=out_shape,
        grid=(bsz, 2),
        in_specs=[
            seq(hp), seq(hp), seq(vp), seq(vp),
            pl.BlockSpec((1, t, DEC_PAD), lambda b, p: (b, 0, 0)),
            pl.BlockSpec((None, 1, DEC_PAD, 2 * hp), lambda b, p: (l, p, 0, 0)),
            pl.BlockSpec((None, 1, 1, 2 * hp), lambda b, p: (l, p, 0, 0)),
            _layer_spec(g_gla, l),
            st_spec,
        ],
        out_specs=out_specs,
        scratch_shapes=scratch,
        compiler_params=pltpu.CompilerParams(vmem_limit_bytes=VMEM_LIMIT),
        name="gla",
    )(zq, zk, zv, zog, zdec, wd_p, bd_p, g_gla, s0)
    if with_output:
        return outs[0], outs[1]
    return None, outs[0]


RG_BLOCK = 512


def _sigmoid_of_twice(x):
    return 0.5 * jnp.tanh(x) + 0.5


def _rg_body(rx_ref, rg_ref, wc_ref, bc_ref, wg_ref, bg_ref, lam_ref, h0_ref, *rest, t, line, with_output):
    if with_output:
        y_ref, hfin_ref, hf_ref, ab_ref, bb_ref = rest
    else:
        hfin_ref, hf_ref, ab_ref, bb_ref = rest
    w = RG_W
    rows = min(t, RG_BLOCK)
    n_blk = t // rows
    grp = SUBLANES
    gpb = rows // grp
    pos = lax.broadcasted_iota(jnp.int32, (rows, w), 0) & (line - 1)
    sub = lax.broadcasted_iota(jnp.int32, (rows // grp, grp, w), 1)
    wc = wc_ref[...]
    bc = bc_ref[...]
    wg = wg_ref[...]
    bg = bg_ref[...]
    sp = [jax.nn.softplus(-lam_ref[d:d + 1, :]) * (-RG_C) for d in range(2)]

    def pre_step(blk, hf):
        r0 = pl.multiple_of(blk * rows, rows)
        rs = pl.ds(r0, rows)
        x = rx_ref[0, rs, :]
        xm1 = jnp.where(pos >= 1, pltpu.roll(x, 1, 0), 0.0)
        xp1 = jnp.where(pos <= line - 2, pltpu.roll(x, rows - 1, 0), 0.0)
        xp2 = jnp.where(pos <= line - 3, pltpu.roll(x, rows - 2, 0), 0.0)
        u = bc + xm1 * wc[0:1, :] + x * wc[1:2, :] + xp1 * wc[2:3, :] + xp2 * wc[3:4, :]
        gates = jnp.dot(u.astype(BF16), wg, preferred_element_type=F32) + bg
        coef = []
        for dirn in range(2):
            r = _sigmoid_of_twice(gates[:, (2 * dirn) * w:(2 * dirn + 1) * w])
            i = _sigmoid_of_twice(gates[:, (2 * dirn + 1) * w:(2 * dirn + 2) * w])
            log_a = r * sp[dirn]
            a = jnp.exp(log_a)
            bx = jnp.sqrt(-jnp.tanh(log_a) * (a * a + 1.0)) * (i * u)
            a3 = a.reshape(rows // grp, grp, w)
            b3 = bx.reshape(rows // grp, grp, w)
            for s in (1, 2, 4):
                if dirn == 0:
                    ok = sub >= s
                    sh = s
                else:
                    ok = sub <= grp - 1 - s
                    sh = grp - s
                b3 = jnp.where(ok, a3 * pltpu.roll(b3, sh, 1) + b3, b3)
                a3 = jnp.where(ok, a3 * pltpu.roll(a3, sh, 1), a3)
            coef.append((a3.reshape(rows, w), b3.reshape(rows, w)))
        ab_ref[rs, :] = coef[1][0]
        bb_ref[rs, :] = coef[1][1]
        af, bf = coef[0]
        for gi in range(gpb):
            g0 = gi * grp
            hfull = bf[g0:g0 + grp, :] + af[g0:g0 + grp, :] * hf
            hf_ref[pl.ds(r0 + g0, grp), :] = hfull
            hf = hfull[grp - 1:grp, :]
        return hf

    hf = lax.fori_loop(0, n_blk, pre_step, h0_ref[0, 0:1, :])

    def bwd_step(j, hb):
        blk = n_blk - 1 - j
        r0 = pl.multiple_of(blk * rows, rows)
        parts = []
        for gi in range(gpb - 1, -1, -1):
            g0 = r0 + gi * grp
            hbull = bb_ref[pl.ds(g0, grp), :] + ab_ref[pl.ds(g0, grp), :] * hb
            hb = hbull[0:1, :]
            parts.append(hbull)
        if with_output:
            hbwd = jnp.concatenate(parts[::-1], axis=0)
            rs = pl.ds(r0, rows)
            h = hf_ref[rs, :] + hbwd
            y_ref[0, rs, :] = (h * jax.nn.gelu(rg_ref[0, rs, :])).astype(y_ref.dtype)
        return hb

    hb = lax.fori_loop(0, n_blk, bwd_step, h0_ref[0, 1:2, :])
    hfin_ref[0, 0:1, :] = hf
    hfin_ref[0, 1:2, :] = hb
    hfin_ref[0, 2:grp, :] = jnp.zeros((grp - 2, w), F32)


def _rg(zrx, zrg, l, w_conv, b_conv, w_gate, b_gate, lam, h0, line, with_output):
    bsz, t, w = zrx.shape
    seq = pl.BlockSpec((1, t, w), lambda b: (b, 0, 0))
    st_spec = pl.BlockSpec((1, SUBLANES, w), lambda b: (b, 0, 0))
    out_shape = [jax.ShapeDtypeStruct((bsz, SUBLANES, w), F32)]
    out_specs = [st_spec]
    if with_output:
        out_shape = [jax.ShapeDtypeStruct((bsz, t, w), BF16)] + out_shape
        out_specs = [seq] + out_specs
    outs = pl.pallas_call(
        functools.partial(_rg_body, t=t, line=line, with_output=with_output),
        out_shape=out_shape,
        grid=(bsz,),
        in_specs=[
            seq, seq,
            _layer_spec(w_conv, l), _layer_spec(b_conv, l),
            _layer_spec(w_gate, l), _layer_spec(b_gate, l),
            _layer_spec(lam, l),
            st_spec,
        ],
        out_specs=out_specs,
        scratch_shapes=[pltpu.VMEM((t, w), F32) for _ in range(3)],
        compiler_params=pltpu.CompilerParams(vmem_limit_bytes=VMEM_LIMIT),
        name="rglru",
    )(zrx, zrg, w_conv, b_conv, w_gate, b_gate, lam, h0)
    if with_output:
        return outs[0], outs[1]
    return None, outs[0]


FFT_PITCH = 72


def _dft_mats(n):
    j = np.arange(n)
    ang = 2.0 * np.pi * ((j[:, None] * j[None, :]) % n) / n
    return np.cos(ang), np.sin(ang)


def _fourier_consts(t):
    n2 = GRID_W
    n1 = t // n2
    cc, sc = _dft_mats(FNET_GD)
    eye2 = np.eye(LANES // FNET_GD)
    m3 = np.concatenate([np.kron(eye2, cc), -np.kron(eye2, sc)], axis=0) / 8.0
    c2, s2 = _dft_mats(n2)
    consts = {"m3": jnp.asarray(m3, F32)}
    if n1 == n2:
        consts["cs"] = jnp.asarray(np.concatenate([c2, s2], axis=0) / 8.0, F32)
        consts["m2"] = jnp.asarray(np.block([[c2, -s2], [s2, c2]]) / 8.0, F32)
        wk = (np.arange(n2)[:, None] * np.arange(n1)[None, :]).reshape(-1)
        ang = 2.0 * np.pi * wk / t
        consts["twc"] = jnp.asarray(np.repeat(np.cos(ang)[:, None], LANES, axis=1), F32)
        consts["tws"] = jnp.asarray(np.repeat(np.sin(ang)[:, None], LANES, axis=1), F32)
    else:
        ct, st = _dft_mats(t)
        consts["cs"] = jnp.asarray(np.concatenate([ct, st], axis=0) / np.sqrt(t), F32)
    return consts


def _fft_body(xa_ref, xb_ref, cs_ref, m2_ref, m3_ref, twc_ref, tws_ref, y_ref, *scr, t):
    n = GRID_W
    p = FFT_PITCH
    cs = cs_ref[...].astype(BF16)
    m2 = m2_ref[...].astype(BF16)
    m3 = m3_ref[...].astype(BF16)
    halves = ((xa_ref,) + tuple(scr[0:4]), (xb_ref,) + tuple(scr[4:8]))

    def stage1(h, i):
        x_ref, sr_ref, si_ref, _, _ = halves[h]
        w0 = 2 * i
        xw = jnp.concatenate([x_ref[0, pl.ds(w0 + j, n, stride=p), :] for j in range(2)], axis=1)
        a = jnp.dot(cs, xw.astype(BF16), preferred_element_type=F32)
        for j in range(2):
            ar = a[0:n, j * LANES:(j + 1) * LANES]
            ai = a[n:2 * n, j * LANES:(j + 1) * LANES]
            r0 = pl.multiple_of((w0 + j) * n, n)
            tc = twc_ref[pl.ds(r0, n), :]
            ts = tws_ref[pl.ds(r0, n), :]
            sr_ref[pl.ds(w0 + j, n, stride=p), :] = ar * tc - ai * ts
            si_ref[pl.ds(w0 + j, n, stride=p), :] = ar * ts + ai * tc

    def stage2(h, i):
        _, sr_ref, si_ref, dr_ref, di_ref = halves[h]
        k0 = 2 * i
        cols = []
        for j in range(2):
            r0 = pl.multiple_of((k0 + j) * p, SUBLANES)
            cols.append(jnp.concatenate([sr_ref[pl.ds(r0, n), :], si_ref[pl.ds(r0, n), :]], axis=0))
        xb = jnp.concatenate(cols, axis=1).astype(BF16)
        d = jnp.dot(m2, xb, preferred_element_type=F32)
        for j in range(2):
            dr_ref[pl.ds(k0 + j, n, stride=p), :] = d[0:n, j * LANES:(j + 1) * LANES]
            di_ref[pl.ds(k0 + j, n, stride=p), :] = d[n:2 * n, j * LANES:(j + 1) * LANES]

    per = 8

    def stage3(h, i):
        _, _, _, dr_ref, di_ref = halves[h]
        blocks = []
        for j in range(per):
            r0 = pl.multiple_of((i * per + j) * p, SUBLANES)
            blocks.append(jnp.concatenate([dr_ref[pl.ds(r0, n), :], di_ref[pl.ds(r0, n), :]], axis=1))
        d = jnp.concatenate(blocks, axis=0).astype(BF16)
        rs = pl.ds(pl.multiple_of(i * per * n, per * n), per * n)
        y_ref[0, rs, h * LANES:(h + 1) * LANES] = jnp.dot(d, m3, preferred_element_type=F32).astype(y_ref.dtype)

    def loop(trips, unroll, *stages):
        def body(i, carry):
            for fn in stages:
                fn(i)
            return carry
        lax.fori_loop(0, trips, body, 0, unroll=unroll)

    ratio = (n // 2) // (n // per)
    loop(n // 2, 8, lambda i: stage1(0, i))
    loop(n // 2, 8, lambda i: stage2(0, i), lambda i: stage1(1, i))
    loop(n // per, 2, lambda i: stage3(0, i), lambda i: [stage2(1, ratio * i + k) for k in range(ratio)])
    loop(n // per, 2, lambda i: stage3(1, i))


def _dense_dft_body(x_ref, cs_ref, m3_ref, y_ref, *, t):
    a = jnp.dot(cs_ref[...].astype(BF16), x_ref[0].astype(BF16), preferred_element_type=F32)
    d = jnp.concatenate([a[0:t], a[t:2 * t]], axis=1).astype(BF16)
    y_ref[0] = jnp.dot(d, m3_ref[...].astype(BF16), preferred_element_type=F32).astype(y_ref.dtype)


def _fourier(zf, consts, t):
    bsz, t_in, w = zf.shape
    seq = pl.BlockSpec((1, t, LANES), lambda b, j: (b, 0, j))
    seq_in = pl.BlockSpec((1, t_in, LANES), lambda b, j: (b, 0, j))
    common = dict(
        out_shape=jax.ShapeDtypeStruct((bsz, t, w), BF16),
        grid=(bsz, w // LANES),
        out_specs=seq,
        compiler_params=pltpu.CompilerParams(vmem_limit_bytes=VMEM_LIMIT),
    )
    if "m2" in consts:
        full = pl.BlockSpec((1, t, w), lambda b: (b, 0, 0))
        half = lambda j: pl.BlockSpec((1, t_in, LANES), lambda b: (b, 0, j))
        return pl.pallas_call(
            functools.partial(_fft_body, t=t),
            out_shape=jax.ShapeDtypeStruct((bsz, t, w), BF16),
            grid=(bsz,),
            in_specs=[half(0), half(1), _const_spec(consts["cs"].shape), _const_spec(consts["m2"].shape),
                      _const_spec(consts["m3"].shape), _const_spec((t, LANES)), _const_spec((t, LANES))],
            out_specs=full,
            scratch_shapes=[pltpu.VMEM((GRID_W * FFT_PITCH, LANES), F32) for _ in range(8)],
            compiler_params=pltpu.CompilerParams(vmem_limit_bytes=VMEM_LIMIT),
            name="fourier_fft",
        )(zf, zf, consts["cs"], consts["m2"], consts["m3"], consts["twc"], consts["tws"])
    return pl.pallas_call(
        functools.partial(_dense_dft_body, t=t),
        in_specs=[seq_in, _const_spec(consts["cs"].shape), _const_spec(consts["m3"].shape)],
        name="fourier_dense", **common,
    )(zf, consts["cs"], consts["m3"])


FF_PIECES = (1536, 1280)


def _outffn_body(yg_ref, yf_ref, yr_ref, x_ref, mod_ref, gpm_ref, gpf_ref, gqf_ref,
                 wo_ref, wg_ref, wu_ref, wd_ref, o_ref):
    tm = x_ref.shape[1]
    sub = min(tm, SUB_ROWS)
    subs = [slice(r * sub, (r + 1) * sub) for r in range(tm // sub)]
    mixes = []
    for rs in subs:
        y = jnp.concatenate([yg_ref[0, rs, :], yf_ref[0, rs, :], yr_ref[0, rs, :]], axis=1)
        mixes.append(jnp.dot(y, wo_ref[...], preferred_element_type=F32))
    x1s, hfs = [], []
    for rs, mix in zip(subs, mixes):
        x1 = x_ref[0, rs, :] + mod_ref[0, 2:3, :] * _rms(mix, gpm_ref[...])
        x1s.append(x1)
        hfs.append((_rms(x1, gpf_ref[...]) * (1.0 + mod_ref[0, 4:5, :]) + mod_ref[0, 3:4, :]).astype(BF16))
    for rs, x1, hf in zip(subs, x1s, hfs):
        acc = None
        off = 0
        for fw in FF_PIECES:
            gate = jnp.dot(hf, wg_ref[:, off:off + fw], preferred_element_type=F32)
            up = jnp.dot(hf, wu_ref[:, off:off + fw], preferred_element_type=F32)
            act = (gate * jax.nn.sigmoid(gate) * up).astype(BF16)
            part = jnp.dot(act, wd_ref[off:off + fw, :], preferred_element_type=F32)
            acc = part if acc is None else acc + part
            off += fw
        o_ref[0, rs, :] = x1 + mod_ref[0, 5:6, :] * _rms(acc, gqf_ref[...])


def _outffn(yg, yf, yr, x, mod, mod_row, l, gpm, gpf, gqf, wo, wg, wu, wd, tm):
    bsz, t, d = x.shape
    if mod_row is None:
        mod_map = lambda b, i: (l, b, 0, 0)
    else:
        mod_map = lambda b, i: (l, mod_row, 0, 0)
    tile = lambda wdt: pl.BlockSpec((1, tm, wdt), lambda b, i: (b, i, 0))
    return pl.pallas_call(
        _outffn_body,
        out_shape=jax.ShapeDtypeStruct((bsz, t, d), F32),
        grid=(bsz, t // tm),
        in_specs=[
            tile(GLA_V), tile(FNET_W), tile(RG_W), tile(d),
            pl.BlockSpec((None, 1, 6, d), mod_map),
            _layer_spec(gpm, l), _layer_spec(gpf, l), _layer_spec(gqf, l),
            _layer_spec(wo, l), _layer_spec(wg, l), _layer_spec(wu, l), _layer_spec(wd, l),
        ],
        out_specs=tile(d),
        compiler_params=pltpu.CompilerParams(vmem_limit_bytes=VMEM_LIMIT),
        name="outffn",
    )(yg, yf, yr, x, mod, gpm, gpf, gqf, wo, wg, wu, wd)


def _prep_w_in(w_in):
    off_dec = 2 * GLA_QK + GLA_V
    off_og = off_dec + 2 * GLA_RANK
    pad = jnp.zeros(w_in.shape[:2] + (DEC_PAD - 2 * GLA_RANK,), w_in.dtype)
    return jnp.concatenate([w_in[..., 0:off_dec], w_in[..., off_og:], w_in[..., off_dec:off_og], pad], axis=-1).astype(BF16)


def _prep_dec(w_dec, b_dec):
    depth = w_dec.shape[0]
    hp = 2 * GLA_DK
    wr = w_dec.reshape(depth, 2, GLA_RANK, 2, hp)
    wbd = jnp.einsum("ldrpj,de->lpdrej", wr, jnp.eye(2, dtype=w_dec.dtype))
    wbd = wbd.reshape(depth, 2, 2 * GLA_RANK, 2 * hp)
    wbd = jnp.pad(wbd, ((0, 0), (0, 0), (0, DEC_PAD - 2 * GLA_RANK), (0, 0)))
    bd = b_dec.reshape(depth, 2, 2, hp).transpose(0, 2, 1, 3).reshape(depth, 2, 1, 2 * hp)
    return wbd.astype(BF16), bd


def _block_diag(w):
    h, hd = w.shape[-3], w.shape[-2]
    out = jnp.einsum("...hij,hg->...higj", w, jnp.eye(h, dtype=w.dtype))
    return out.reshape(w.shape[:-3] + (h * hd, h * hd))


def _prep_rg(w_a, b_a, w_x, b_x):
    bda, bdx = _block_diag(w_a), _block_diag(w_x)
    wg = jnp.concatenate([bda[:, 0], bdx[:, 0], bda[:, 1], bdx[:, 1]], axis=-1)
    bg = jnp.concatenate([b_a[:, 0], b_x[:, 0], b_a[:, 1], b_x[:, 1]], axis=-1)[:, None, :]
    return (0.5 * wg).astype(BF16), 0.5 * bg


def _mixer(h, mod, mod_row, l, lw, consts, s0, h0, line, tm, with_output):
    t = h.shape[1]
    f_pitch = FFT_PITCH if "m2" in consts else GRID_W
    zq, zk, zv, zog, zf, zrx, zrg, zdec = _inproj(h, mod, mod_row, l, lw["g_pre_mix"], lw["w_in"], tm, f_pitch)
    yg, s_fin = _gla(zq, zk, zv, zog, zdec, l, lw["wd"], lw["bd"], lw["g_gla"], s0, with_output)
    yr, h_fin = _rg(zrx, zrg, l, lw["w_conv"], lw["b_conv"], lw["w_gate"], lw["b_gate"], lw["lam"], h0, line, with_output)
    yf = _fourier(zf, consts, t) if with_output else None
    return (yg, yf, yr), s_fin, h_fin


def kernel(x, c, ctx, c_ctx, w_ada, b_ada, g_pre_mix, g_post_mix, g_pre_ffn, g_post_ffn, w_in, w_dec, b_dec, g_gla,
           w_conv, b_conv, w_rg_a, b_rg_a, w_rg_x, b_rg_x, rg_lam, w_out, w_ffn_gate, w_ffn_up, w_ffn_down):
    bsz, t, d = x.shape
    t_ctx = ctx.shape[1]
    depth = w_ada.shape[0]

    c_all = jnp.concatenate([c, c_ctx[None, :], jnp.zeros((MOD_ROWS - bsz - 1, d), F32)], axis=0)
    mod = _ada(c_all, w_ada, b_ada).reshape(depth, MOD_ROWS, 6, d)

    consts_lat = _fourier_consts(t)
    consts_ctx = _fourier_consts(t_ctx)
    s_zero = jnp.zeros((bsz, 2, GLA_HEADS, GLA_DK, GLA_DV), F32)
    h_zero = jnp.zeros((bsz, SUBLANES, RG_W), F32)
    rows = lambda v: v[:, None, :]

    wd_p, bd_p = _prep_dec(w_dec, b_dec)
    w_gate, b_gate = _prep_rg(w_rg_a, b_rg_a, w_rg_x, b_rg_x)
    lw = dict(g_pre_mix=rows(g_pre_mix), w_in=_prep_w_in(w_in), wd=wd_p, bd=bd_p, g_gla=rows(g_gla),
              w_conv=w_conv, b_conv=rows(b_conv), w_gate=w_gate, b_gate=b_gate, lam=rg_lam)
    ffn = (rows(g_post_mix), rows(g_pre_ffn), rows(g_post_ffn), w_out.astype(BF16),
           w_ffn_gate.astype(BF16), w_ffn_up.astype(BF16), w_ffn_down.astype(BF16))

    tm = min(t, ROW_TILE)
    assert t % tm == 0 and tm % GRID_W == 0 and t_ctx % GRID_W == 0
    h_ctx = ctx
    for l in range(depth):
        last = l == depth - 1
        ys_c, s_ctx, hs_ctx = _mixer(h_ctx, mod, bsz, l, lw, consts_ctx, s_zero, h_zero, t_ctx, t_ctx, not last)
        ys_l, _, _ = _mixer(x, mod, None, l, lw, consts_lat, s_ctx, hs_ctx, GRID_W, tm, True)
        x = _outffn(*ys_l, x, mod, None, l, *ffn, tm)
        if not last:
            h_ctx = _outffn(*ys_c, h_ctx, mod, bsz, l, *ffn, t_ctx)
    return x
```

```python
import functools

import jax
import jax.numpy as jnp
import numpy as np
from jax import lax
from jax.experimental import pallas as pl
from jax.experimental.pallas import tpu as pltpu

F32 = jnp.float32
BF16 = jnp.bfloat16

D_MODEL = 1024
DEPTH = 4
GRID_W = 64
EPS = 1e-6

GLA_HEADS = 4
GLA_DK = 64
GLA_DV = 128
GLA_QK = GLA_HEADS * GLA_DK
GLA_V = GLA_HEADS * GLA_DV
GLA_RANK = 16
GLA_TAU = 16.0
GLA_CHUNK = 64
FNET_W = 256
FNET_GD = 64
RG_HEADS = 4
RG_HD = 64
RG_W = 256
RG_C = 8.0
CONV_W = 4
D_FF = 2816

LANES = 128
SUBLANES = 8
DEC_PAD = LANES
N_IN_PAD = 2 * GLA_QK + 2 * GLA_V + FNET_W + 2 * RG_W + DEC_PAD
MOD_ROWS = 16
VMEM_LIMIT = 56 * 1024 * 1024

_NT = (((1,), (1,)), ((), ()))
_TN = (((0,), (0,)), ((), ()))


def _rms(x, g):
    return x * lax.rsqrt(jnp.mean(x * x, axis=-1, keepdims=True) + EPS) * g


def _const_spec(shape):
    nd = len(shape)
    return pl.BlockSpec(shape, lambda *_: (0,) * nd, pipeline_mode=pl.Buffered(1))


def _layer_spec(arr, l):
    nd = arr.ndim - 1
    return pl.BlockSpec((None,) + arr.shape[1:], lambda *_: (l,) + (0,) * nd, pipeline_mode=pl.Buffered(1))


def _ada_body(c_ref, w_ref, b_ref, o_ref):
    c = c_ref[...]
    s = (c * jax.nn.sigmoid(c)).astype(BF16)
    o_ref[0] = jnp.dot(s, w_ref[0].astype(BF16), preferred_element_type=F32) + b_ref[0]


def _ada(c_all, w_ada, b_ada):
    depth, d, n = w_ada.shape
    tn = 1536
    return pl.pallas_call(
        _ada_body,
        out_shape=jax.ShapeDtypeStruct((depth, MOD_ROWS, n), F32),
        grid=(depth, n // tn),
        in_specs=[
            pl.BlockSpec((MOD_ROWS, d), lambda l, j: (0, 0)),
            pl.BlockSpec((1, d, tn), lambda l, j: (l, 0, j)),
            pl.BlockSpec((1, 1, tn), lambda l, j: (l, 0, j)),
        ],
        out_specs=pl.BlockSpec((1, MOD_ROWS, tn), lambda l, j: (l, 0, j)),
        compiler_params=pltpu.CompilerParams(vmem_limit_bytes=VMEM_LIMIT),
        name="ada",
    )(c_all, w_ada, b_ada.reshape(depth, 1, n))


ROW_TILE = 1024
SUB_ROWS = 256

_IN_WIDTHS = (GLA_QK, GLA_QK, GLA_V, GLA_V, FNET_W, RG_W, RG_W, DEC_PAD)
_IN_F = 4


def _inproj_body(x_ref, mod_ref, g_ref, w_ref, *out_refs, f_pitch):
    tm = x_ref.shape[1]
    sub = min(tm, SUB_ROWS)
    for r in range(tm // sub):
        rs = slice(r * sub, (r + 1) * sub)
        h = _rms(x_ref[0, rs, :], g_ref[...]) * (1.0 + mod_ref[0, 1:2, :]) + mod_ref[0, 0:1, :]
        h = h.astype(BF16)
        off = 0
        for idx, (o_ref, wd) in enumerate(zip(out_refs, _IN_WIDTHS)):
            z = jnp.dot(h, w_ref[:, off:off + wd], preferred_element_type=F32)
            off += wd
            if idx == _IN_F and f_pitch != GRID_W:
                pad = jnp.zeros((f_pitch - GRID_W, wd), F32)
                for ln in range(sub // GRID_W):
                    base = (r * (sub // GRID_W) + ln) * f_pitch
                    o_ref[0, base:base + GRID_W, :] = z[ln * GRID_W:(ln + 1) * GRID_W, :]
                    o_ref[0, base + GRID_W:base + f_pitch, :] = pad
            else:
                o_ref[0, rs, :] = z


def _inproj(x, mod, mod_row, l, g, w_in_p, tm, f_pitch):
    bsz, t, d = x.shape
    if mod_row is None:
        mod_map = lambda b, i: (l, b, 0, 0)
    else:
        mod_map = lambda b, i: (l, mod_row, 0, 0)
    rows_of = lambda idx, n: n // GRID_W * f_pitch if idx == _IN_F else n
    return pl.pallas_call(
        functools.partial(_inproj_body, f_pitch=f_pitch),
        out_shape=[jax.ShapeDtypeStruct((bsz, rows_of(i, t), wd), F32) for i, wd in enumerate(_IN_WIDTHS)],
        grid=(bsz, t // tm),
        in_specs=[
            pl.BlockSpec((1, tm, d), lambda b, i: (b, i, 0)),
            pl.BlockSpec((None, 1, 6, d), mod_map),
            _layer_spec(g, l),
            _layer_spec(w_in_p, l),
        ],
        out_specs=[pl.BlockSpec((1, rows_of(i, tm), wd), lambda b, i: (b, i, 0)) for i, wd in enumerate(_IN_WIDTHS)],
        compiler_params=pltpu.CompilerParams(vmem_limit_bytes=VMEM_LIMIT),
        name="inproj",
    )(x, mod, g, w_in_p)


GLA_BLOCK = 256
GLA_RINGS = 8


def _chunk_cumsum(x, reverse):
    rows, lanes = x.shape
    g = SUBLANES
    c = GLA_CHUNK
    x3 = x.reshape(rows // g, g, lanes)
    sub = lax.broadcasted_iota(jnp.int32, x3.shape, 1)
    for s in (1, 2, 4):
        if reverse:
            x3 = x3 + jnp.where(sub < g - s, pltpu.roll(x3, g - s, 1), 0.0)
        else:
            x3 = x3 + jnp.where(sub >= s, pltpu.roll(x3, s, 1), 0.0)
    edge = x3[:, 0:1, :] if reverse else x3[:, g - 1:g, :]
    tot = jnp.broadcast_to(edge, x3.shape).reshape(rows, lanes)
    pos = lax.broadcasted_iota(jnp.int32, (rows, lanes), 0) & (c - 1)
    acc = tot
    for s in (8, 16, 32):
        if reverse:
            acc = acc + jnp.where(pos < c - s, pltpu.roll(acc, rows - s, 0), 0.0)
        else:
            acc = acc + jnp.where(pos >= s, pltpu.roll(acc, s, 0), 0.0)
    sums = x3.reshape(rows, lanes) + (acc - tot)
    parts = []
    for ci in range(rows // c):
        r = ci * c if reverse else ci * c + c - 1
        parts.append(jnp.broadcast_to(acc[r:r + 1, :], (c, lanes)))
    total = parts[0] if len(parts) == 1 else jnp.concatenate(parts, axis=0)
    return sums, total


def _gla_body(q_ref, k_ref, v_ref, og_ref, dec_ref, wd_ref, bd_ref, g_ref, s0_ref, *rest, t, with_output):
    nr = GLA_RINGS
    if with_output:
        y_ref, sfin_ref, u_ref, dl_ref, qt_ref, o_ref = rest[:6]
        bufs = rest[6:]
        rings = tuple((bufs[r], bufs[nr + r], bufs[2 * nr + r], bufs[3 * nr + r]) for r in range(nr))
    else:
        sfin_ref, u_ref, dl_ref = rest[:3]
        bufs = rest[3:]
        rings = tuple((bufs[r], bufs[nr + r], None, None) for r in range(nr))
    c = GLA_CHUNK
    hp = 2 * GLA_DK
    vp = 2 * GLA_DV
    rows = min(t, GLA_BLOCK)
    cpb = rows // c
    n_blk = t // rows
    scale = GLA_DK ** -0.5

    dk_head0 = lax.broadcasted_iota(jnp.int32, (hp, 2 * c), 0) < GLA_DK
    ar = lax.broadcasted_iota(jnp.int32, (2 * c, 4 * c), 0)
    ac = lax.broadcasted_iota(jnp.int32, (2 * c, 4 * c), 1)
    same_chunk = (ar // c) == ((ac // c) & 1)
    keep_f = same_chunk & ((ar & (c - 1)) >= (ac & (c - 1)))
    keep_b = same_chunk & ((ar & (c - 1)) <= (ac & (c - 1)))
    wd = wd_ref[0]
    bd = bd_ref[0]

    def prepare(blk, ring):
        vb_ring, ke_ring, qt_ring, kt_ring = ring
        r0 = pl.multiple_of(blk * rows, rows)
        code = dec_ref[0, pl.ds(r0, rows), :].astype(BF16)
        logit = jnp.dot(code, wd, preferred_element_type=F32) + bd
        la = (jnp.minimum(logit, 0.0) - jnp.log(1.0 + jnp.exp(-jnp.abs(logit)))) * (1.0 / GLA_TAU)
        k = k_ref[0, pl.ds(r0, rows), :]
        vb_ring[...] = v_ref[0, pl.ds(r0, rows), :].astype(BF16)
        if with_output:
            q = q_ref[0, pl.ds(r0, rows), :] * scale
        for dirn in range(2):
            b, b_all = _chunk_cumsum(la[:, dirn * hp:(dirn + 1) * hp], reverse=dirn == 1)
            ke_ring[dirn] = (k * jnp.exp(b_all - b)).astype(BF16)
            for ci in range(cpb):
                dl_ref[dirn, blk * cpb + ci] = jnp.exp(b_all[ci * c:ci * c + SUBLANES, :])
            if with_output:
                qt = (q * jnp.exp(b)).astype(BF16)
                qt_ref[dirn, pl.ds(r0, rows), :] = qt
                qt_ring[dirn] = qt
                kt_ring[dirn] = (k * jnp.exp(-b)).T.astype(BF16)

    def multiply(blk, ring):
        vb_ring, ke_ring, qt_ring, kt_ring = ring
        r0 = pl.multiple_of(blk * rows, rows)
        vb = vb_ring[...]
        for dirn in range(2):
            ke = ke_ring[dirn]
            for ci in range(cpb):
                sl = slice(ci * c, (ci + 1) * c)
                upd = lax.dot_general(ke[sl], vb[sl], _TN, preferred_element_type=F32)
                u_ref[dirn, blk * cpb + ci, 0] = upd[0:GLA_DK, 0:GLA_DV]
                u_ref[dirn, blk * cpb + ci, 1] = upd[GLA_DK:hp, GLA_DV:vp]
            if not with_output:
                continue
            qt = qt_ring[dirn]
            kt_t = kt_ring[dirn]
            zt = jnp.zeros((hp, 2 * c), BF16)
            zv = jnp.zeros((2 * c, GLA_DV), BF16)
            parts = []
            for cp in range(cpb // 2):
                sl = slice(cp * 2 * c, (cp + 1) * 2 * c)
                kt_cp = kt_t[:, sl]
                kbd = jnp.concatenate([jnp.where(dk_head0, kt_cp, zt), jnp.where(dk_head0, zt, kt_cp)], axis=1)
                sc = jnp.dot(qt[sl], kbd, preferred_element_type=F32)
                p = jnp.where(keep_f if dirn == 0 else keep_b, sc, 0.0).astype(BF16)
                v_cp = vb[sl]
                vbd = jnp.concatenate(
                    [jnp.concatenate([v_cp[:, 0:GLA_DV], zv], axis=1),
                     jnp.concatenate([zv, v_cp[:, GLA_DV:vp]], axis=1)], axis=0)
                parts.append(jnp.dot(p, vbd, preferred_element_type=F32))
            o_intra = parts[0] if len(parts) == 1 else jnp.concatenate(parts, axis=0)
            if dirn == 0:
                o_ref[pl.ds(r0, rows), :] = o_intra
            else:
                o_ref[pl.ds(r0, rows), :] += o_intra

    zs = jnp.zeros((GLA_DK, GLA_DV), BF16)

    def sweep_block(dirn, blk, s0, s1):
        r0 = blk * rows if isinstance(blk, int) else pl.multiple_of(blk * rows, rows)
        for cj in range(cpb):
            ci = cj if dirn == 0 else cpb - 1 - cj
            chunk = blk * cpb + ci
            if with_output:
                rs = pl.ds(r0 + ci * c, c)
                sbd = jnp.concatenate(
                    [jnp.concatenate([s0.astype(BF16), zs], axis=1),
                     jnp.concatenate([zs, s1.astype(BF16)], axis=1)], axis=0)
                o_ref[rs, :] += jnp.dot(qt_ref[dirn, rs, :], sbd, preferred_element_type=F32)
            drow = dl_ref[dirn, chunk][0:1, :]
            dcol = jnp.broadcast_to(drow, (hp, hp)).T
            s0 = s0 * dcol[0:GLA_DK, :] + u_ref[dirn, chunk, 0]
            s1 = s1 * dcol[GLA_DK:hp, :] + u_ref[dirn, chunk, 1]
        return s0, s1

    def gate_block(blk):
        rs = pl.ds(pl.multiple_of(blk * rows, rows), rows)
        o = o_ref[rs, :]
        og = og_ref[0, rs, :]
        g = g_ref[...]
        o0 = _rms(o[:, 0:GLA_DV], g)
        o1 = _rms(o[:, GLA_DV:vp], g)
        gate = og * jax.nn.sigmoid(og)
        y_ref[0, rs, :] = (jnp.concatenate([o0, o1], axis=1) * gate).astype(y_ref.dtype)

    s0, s1 = s0_ref[0, 0, 0], s0_ref[0, 0, 1]
    prepare(0, rings[0])
    for i in range(n_blk):
        if i + 1 < n_blk:
            prepare(i + 1, rings[(i + 1) % nr])
        multiply(i, rings[i % nr])
        if i > 0:
            s0, s1 = sweep_block(0, i - 1, s0, s1)
    s0, s1 = sweep_block(0, n_blk - 1, s0, s1)
    sfin_ref[0, 0, 0] = s0
    sfin_ref[0, 0, 1] = s1

    def bwd_step(i, carry):
        blk = n_blk - 1 - i
        s0, s1 = sweep_block(1, blk, *carry)
        if with_output:
            gate_block(blk)
        return s0, s1

    s0, s1 = lax.fori_loop(0, n_blk, bwd_step, (s0_ref[0, 1, 0], s0_ref[0, 1, 1]), unroll=min(4, n_blk))
    sfin_ref[0, 1, 0] = s0
    sfin_ref[0, 1, 1] = s1


def _gla(zq, zk, zv, zog, zdec, l, wd_p, bd_p, g_gla, s0, with_output):
    bsz, t, _ = zq.shape
    hp, vp = 2 * GLA_DK, 2 * GLA_DV
    n_chunks = t // GLA_CHUNK
    seq = lambda wd: pl.BlockSpec((1, t, wd), lambda b, p: (b, 0, p))
    st_spec = pl.BlockSpec((1, 2, 2, GLA_DK, GLA_DV), lambda b, p: (b, 0, p, 0, 0))
    out_shape = [jax.ShapeDtypeStruct((bsz, 2, GLA_HEADS, GLA_DK, GLA_DV), F32)]
    out_specs = [st_spec]
    rows = min(t, GLA_BLOCK)
    nr = GLA_RINGS
    scratch = [pltpu.VMEM((2, n_chunks, 2, GLA_DK, GLA_DV), F32), pltpu.VMEM((2, n_chunks, SUBLANES, hp), F32)]
    staging = [pltpu.VMEM((rows, vp), BF16) for _ in range(nr)] + [pltpu.VMEM((2, rows, hp), BF16) for _ in range(nr)]
    if with_output:
        out_shape = [jax.ShapeDtypeStruct((bsz, t, GLA_V), BF16)] + out_shape
        out_specs = [seq(vp)] + out_specs
        scratch = scratch + [pltpu.VMEM((2, t, hp), BF16), pltpu.VMEM((t, vp), F32)]
        staging = staging + [pltpu.VMEM((2, rows, hp), BF16) for _ in range(nr)]
        staging = staging + [pltpu.VMEM((2, hp, rows), BF16) for _ in range(nr)]
    scratch = scratch + staging
    outs = pl.pallas_call(
        functools.partial(_gla_body, t=t, with_output=with_output),
        out_shape=out_shape,
        grid=(bsz, 2),
        in_specs=[
            seq(hp), seq(hp), seq(vp), seq(vp),
            pl.BlockSpec((1, t, DEC_PAD), lambda b, p: (b, 0, 0)),
            pl.BlockSpec((None, 1, DEC_PAD, 2 * hp), lambda b, p: (l, p, 0, 0)),
            pl.BlockSpec((None, 1, 1, 2 * hp), lambda b, p: (l, p, 0, 0)),
            _layer_spec(g_gla, l),
            st_spec,
        ],
        out_specs=out_specs,
        scratch_shapes=scratch,
        compiler_params=pltpu.CompilerParams(vmem_limit_bytes=VMEM_LIMIT),
        name="gla",
    )(zq, zk, zv, zog, zdec, wd_p, bd_p, g_gla, s0)
    if with_output:
        return outs[0], outs[1]
    return None, outs[0]


RG_BLOCK = 512


def _sigmoid_of_twice(x):
    return 0.5 * jnp.tanh(x) + 0.5


def _rg_body(rx_ref, rg_ref, wc_ref, bc_ref, wg_ref, bg_ref, lam_ref, h0_ref, *rest, t, line, with_output):
    if with_output:
        y_ref, hfin_ref, hf_ref, ab_ref, bb_ref = rest
    else:
        hfin_ref, hf_ref, ab_ref, bb_ref = rest
    w = RG_W
    rows = min(t, RG_BLOCK)
    n_blk = t // rows
    grp = SUBLANES
    gpb = rows // grp
    pos = lax.broadcasted_iota(jnp.int32, (rows, w), 0) & (line - 1)
    sub = lax.broadcasted_iota(jnp.int32, (rows // grp, grp, w), 1)
    wc = wc_ref[...]
    bc = bc_ref[...]
    wg = wg_ref[...]
    bg = bg_ref[...]
    sp = [jax.nn.softplus(-lam_ref[d:d + 1, :]) * (-RG_C) for d in range(2)]

    def pre_step(blk, hf):
        r0 = pl.multiple_of(blk * rows, rows)
        rs = pl.ds(r0, rows)
        x = rx_ref[0, rs, :]
        xm1 = jnp.where(pos >= 1, pltpu.roll(x, 1, 0), 0.0)
        xp1 = jnp.where(pos <= line - 2, pltpu.roll(x, rows - 1, 0), 0.0)
        xp2 = jnp.where(pos <= line - 3, pltpu.roll(x, rows - 2, 0), 0.0)
        u = bc + xm1 * wc[0:1, :] + x * wc[1:2, :] + xp1 * wc[2:3, :] + xp2 * wc[3:4, :]
        gates = jnp.dot(u.astype(BF16), wg, preferred_element_type=F32) + bg
        coef = []
        for dirn in range(2):
            r = _sigmoid_of_twice(gates[:, (2 * dirn) * w:(2 * dirn + 1) * w])
            i = _sigmoid_of_twice(gates[:, (2 * dirn + 1) * w:(2 * dirn + 2) * w])
            log_a = r * sp[dirn]
            a = jnp.exp(log_a)
            bx = jnp.sqrt(-jnp.tanh(log_a) * (a * a + 1.0)) * (i * u)
            a3 = a.reshape(rows // grp, grp, w)
            b3 = bx.reshape(rows // grp, grp, w)
            for s in (1, 2, 4):
                if dirn == 0:
                    ok = sub >= s
                    sh = s
                else:
                    ok = sub <= grp - 1 - s
                    sh = grp - s
                b3 = jnp.where(ok, a3 * pltpu.roll(b3, sh, 1) + b3, b3)
                a3 = jnp.where(ok, a3 * pltpu.roll(a3, sh, 1), a3)
            coef.append((a3.reshape(rows, w), b3.reshape(rows, w)))
        ab_ref[rs, :] = coef[1][0]
        bb_ref[rs, :] = coef[1][1]
        af, bf = coef[0]
        for gi in range(gpb):
            g0 = gi * grp
            hfull = bf[g0:g0 + grp, :] + af[g0:g0 + grp, :] * hf
            hf_ref[pl.ds(r0 + g0, grp), :] = hfull
            hf = hfull[grp - 1:grp, :]
        return hf

    hf = lax.fori_loop(0, n_blk, pre_step, h0_ref[0, 0:1, :])

    def bwd_step(j, hb):
        blk = n_blk - 1 - j
        r0 = pl.multiple_of(blk * rows, rows)
        parts = []
        for gi in range(gpb - 1, -1, -1):
            g0 = r0 + gi * grp
            hbull = bb_ref[pl.ds(g0, grp), :] + ab_ref[pl.ds(g0, grp), :] * hb
            hb = hbull[0:1, :]
            parts.append(hbull)
        if with_output:
            hbwd = jnp.concatenate(parts[::-1], axis=0)
            rs = pl.ds(r0, rows)
            h = hf_ref[rs, :] + hbwd
            y_ref[0, rs, :] = (h * jax.nn.gelu(rg_ref[0, rs, :])).astype(y_ref.dtype)
        return hb

    hb = lax.fori_loop(0, n_blk, bwd_step, h0_ref[0, 1:2, :])
    hfin_ref[0, 0:1, :] = hf
    hfin_ref[0, 1:2, :] = hb
    hfin_ref[0, 2:grp, :] = jnp.zeros((grp - 2, w), F32)


def _rg(zrx, zrg, l, w_conv, b_conv, w_gate, b_gate, lam, h0, line, with_output):
    bsz, t, w = zrx.shape
    seq = pl.BlockSpec((1, t, w), lambda b: (b, 0, 0))
    st_spec = pl.BlockSpec((1, SUBLANES, w), lambda b: (b, 0, 0))
    out_shape = [jax.ShapeDtypeStruct((bsz, SUBLANES, w), F32)]
    out_specs = [st_spec]
    if with_output:
        out_shape = [jax.ShapeDtypeStruct((bsz, t, w), BF16)] + out_shape
        out_specs = [seq] + out_specs
    outs = pl.pallas_call(
        functools.partial(_rg_body, t=t, line=line, with_output=with_output),
        out_shape=out_shape,
        grid=(bsz,),
        in_specs=[
            seq, seq,
            _layer_spec(w_conv, l), _layer_spec(b_conv, l),
            _layer_spec(w_gate, l), _layer_spec(b_gate, l),
            _layer_spec(lam, l),
            st_spec,
        ],
        out_specs=out_specs,
        scratch_shapes=[pltpu.VMEM((t, w), F32) for _ in range(3)],
        compiler_params=pltpu.CompilerParams(vmem_limit_bytes=VMEM_LIMIT),
        name="rglru",
    )(zrx, zrg, w_conv, b_conv, w_gate, b_gate, lam, h0)
    if with_output:
        return outs[0], outs[1]
    return None, outs[0]


FFT_PITCH = 72


def _dft_mats(n):
    j = np.arange(n)
    ang = 2.0 * np.pi * ((j[:, None] * j[None, :]) % n) / n
    return np.cos(ang), np.sin(ang)


def _fourier_consts(t):
    n2 = GRID_W
    n1 = t // n2
    cc, sc = _dft_mats(FNET_GD)
    eye2 = np.eye(LANES // FNET_GD)
    m3 = np.concatenate([np.kron(eye2, cc), -np.kron(eye2, sc)], axis=0) / 8.0
    c2, s2 = _dft_mats(n2)
    consts = {"m3": jnp.asarray(m3, F32)}
    if n1 == n2:
        consts["cs"] = jnp.asarray(np.concatenate([c2, s2], axis=0) / 8.0, F32)
        consts["m2"] = jnp.asarray(np.block([[c2, -s2], [s2, c2]]) / 8.0, F32)
        wk = (np.arange(n2)[:, None] * np.arange(n1)[None, :]).reshape(-1)
        ang = 2.0 * np.pi * wk / t
        consts["twc"] = jnp.asarray(np.repeat(np.cos(ang)[:, None], LANES, axis=1), F32)
        consts["tws"] = jnp.asarray(np.repeat(np.sin(ang)[:, None], LANES, axis=1), F32)
    else:
        ct, st = _dft_mats(t)
        consts["cs"] = jnp.asarray(np.concatenate([ct, st], axis=0) / np.sqrt(t), F32)
    return consts


def _fft_body(xa_ref, xb_ref, cs_ref, m2_ref, m3_ref, twc_ref, tws_ref, y_ref, *scr, t):
    n = GRID_W
    p = FFT_PITCH
    cs = cs_ref[...].astype(BF16)
    m2 = m2_ref[...].astype(BF16)
    m3 = m3_ref[...].astype(BF16)
    halves = ((xa_ref,) + tuple(scr[0:4]), (xb_ref,) + tuple(scr[4:8]))

    def stage1(h, i):
        x_ref, sr_ref, si_ref, _, _ = halves[h]
        w0 = 2 * i
        xw = jnp.concatenate([x_ref[0, pl.ds(w0 + j, n, stride=p), :] for j in range(2)], axis=1)
        a = jnp.dot(cs, xw.astype(BF16), preferred_element_type=F32)
        for j in range(2):
            ar = a[0:n, j * LANES:(j + 1) * LANES]
            ai = a[n:2 * n, j * LANES:(j + 1) * LANES]
            r0 = pl.multiple_of((w0 + j) * n, n)
            tc = twc_ref[pl.ds(r0, n), :]
            ts = tws_ref[pl.ds(r0, n), :]
            sr_ref[pl.ds(w0 + j, n, stride=p), :] = ar * tc - ai * ts
            si_ref[pl.ds(w0 + j, n, stride=p), :] = ar * ts + ai * tc

    def stage2(h, i):
        _, sr_ref, si_ref, dr_ref, di_ref = halves[h]
        k0 = 2 * i
        cols = []
        for j in range(2):
            r0 = pl.multiple_of((k0 + j) * p, SUBLANES)
            cols.append(jnp.concatenate([sr_ref[pl.ds(r0, n), :], si_ref[pl.ds(r0, n), :]], axis=0))
        xb = jnp.concatenate(cols, axis=1).astype(BF16)
        d = jnp.dot(m2, xb, preferred_element_type=F32)
        for j in range(2):
            dr_ref[pl.ds(k0 + j, n, stride=p), :] = d[0:n, j * LANES:(j + 1) * LANES]
            di_ref[pl.ds(k0 + j, n, stride=p), :] = d[n:2 * n, j * LANES:(j + 1) * LANES]

    per = 8

    def stage3(h, i):
        _, _, _, dr_ref, di_ref = halves[h]
        blocks = []
        for j in range(per):
            r0 = pl.multiple_of((i * per + j) * p, SUBLANES)
            blocks.append(jnp.concatenate([dr_ref[pl.ds(r0, n), :], di_ref[pl.ds(r0, n), :]], axis=1))
        d = jnp.concatenate(blocks, axis=0).astype(BF16)
        rs = pl.ds(pl.multiple_of(i * per * n, per * n), per * n)
        y_ref[0, rs, h * LANES:(h + 1) * LANES] = jnp.dot(d, m3, preferred_element_type=F32).astype(y_ref.dtype)

    def loop(trips, unroll, *stages):
        def body(i, carry):
            for fn in stages:
                fn(i)
            return carry
        lax.fori_loop(0, trips, body, 0, unroll=unroll)

    ratio = (n // 2) // (n // per)
    loop(n // 2, 16, lambda i: stage1(0, i))
    loop(n // 2, 16, lambda i: stage2(0, i), lambda i: stage1(1, i))
    loop(n // per, 4, lambda i: stage3(0, i), lambda i: [stage2(1, ratio * i + k) for k in range(ratio)])
    loop(n // per, 4, lambda i: stage3(1, i))


def _dense_dft_body(x_ref, cs_ref, m3_ref, y_ref, *, t):
    a = jnp.dot(cs_ref[...].astype(BF16), x_ref[0].astype(BF16), preferred_element_type=F32)
    d = jnp.concatenate([a[0:t], a[t:2 * t]], axis=1).astype(BF16)
    y_ref[0] = jnp.dot(d, m3_ref[...].astype(BF16), preferred_element_type=F32).astype(y_ref.dtype)


def _fourier(zf, consts, t):
    bsz, t_in, w = zf.shape
    seq = pl.BlockSpec((1, t, LANES), lambda b, j: (b, 0, j))
    seq_in = pl.BlockSpec((1, t_in, LANES), lambda b, j: (b, 0, j))
    common = dict(
        out_shape=jax.ShapeDtypeStruct((bsz, t, w), BF16),
        grid=(bsz, w // LANES),
        out_specs=seq,
        compiler_params=pltpu.CompilerParams(vmem_limit_bytes=VMEM_LIMIT),
    )
    if "m2" in consts:
        full = pl.BlockSpec((1, t, w), lambda b: (b, 0, 0))
        half = lambda j: pl.BlockSpec((1, t_in, LANES), lambda b: (b, 0, j))
        return pl.pallas_call(
            functools.partial(_fft_body, t=t),
            out_shape=jax.ShapeDtypeStruct((bsz, t, w), BF16),
            grid=(bsz,),
            in_specs=[half(0), half(1), _const_spec(consts["cs"].shape), _const_spec(consts["m2"].shape),
                      _const_spec(consts["m3"].shape), _const_spec((t, LANES)), _const_spec((t, LANES))],
            out_specs=full,
            scratch_shapes=[pltpu.VMEM((GRID_W * FFT_PITCH, LANES), F32) for _ in range(8)],
            compiler_params=pltpu.CompilerParams(vmem_limit_bytes=VMEM_LIMIT),
            name="fourier_fft",
        )(zf, zf, consts["cs"], consts["m2"], consts["m3"], consts["twc"], consts["tws"])
    return pl.pallas_call(
        functools.partial(_dense_dft_body, t=t),
        in_specs=[seq_in, _const_spec(consts["cs"].shape), _const_spec(consts["m3"].shape)],
        name="fourier_dense", **common,
    )(zf, consts["cs"], consts["m3"])


FF_PIECES = (1536, 1280)


def _outffn_body(yg_ref, yf_ref, yr_ref, x_ref, mod_ref, gpm_ref, gpf_ref, gqf_ref,
                 wo_ref, wg_ref, wu_ref, wd_ref, o_ref):
    tm = x_ref.shape[1]
    sub = min(tm, SUB_ROWS)
    subs = [slice(r * sub, (r + 1) * sub) for r in range(tm // sub)]
    mixes = []
    for rs in subs:
        y = jnp.concatenate([yg_ref[0, rs, :], yf_ref[0, rs, :], yr_ref[0, rs, :]], axis=1)
        mixes.append(jnp.dot(y, wo_ref[...], preferred_element_type=F32))
    x1s, hfs = [], []
    for rs, mix in zip(subs, mixes):
        x1 = x_ref[0, rs, :] + mod_ref[0, 2:3, :] * _rms(mix, gpm_ref[...])
        x1s.append(x1)
        hfs.append((_rms(x1, gpf_ref[...]) * (1.0 + mod_ref[0, 4:5, :]) + mod_ref[0, 3:4, :]).astype(BF16))
    for rs, x1, hf in zip(subs, x1s, hfs):
        acc = None
        off = 0
        for fw in FF_PIECES:
            gate = jnp.dot(hf, wg_ref[:, off:off + fw], preferred_element_type=F32)
            up = jnp.dot(hf, wu_ref[:, off:off + fw], preferred_element_type=F32)
            act = (gate * jax.nn.sigmoid(gate) * up).astype(BF16)
            part = jnp.dot(act, wd_ref[off:off + fw, :], preferred_element_type=F32)
            acc = part if acc is None else acc + part
            off += fw
        o_ref[0, rs, :] = x1 + mod_ref[0, 5:6, :] * _rms(acc, gqf_ref[...])


def _outffn(yg, yf, yr, x, mod, mod_row, l, gpm, gpf, gqf, wo, wg, wu, wd, tm):
    bsz, t, d = x.shape
    if mod_row is None:
        mod_map = lambda b, i: (l, b, 0, 0)
    else:
        mod_map = lambda b, i: (l, mod_row, 0, 0)
    tile = lambda wdt: pl.BlockSpec((1, tm, wdt), lambda b, i: (b, i, 0))
    return pl.pallas_call(
        _outffn_body,
        out_shape=jax.ShapeDtypeStruct((bsz, t, d), F32),
        grid=(bsz, t // tm),
        in_specs=[
            tile(GLA_V), tile(FNET_W), tile(RG_W), tile(d),
            pl.BlockSpec((None, 1, 6, d), mod_map),
            _layer_spec(gpm, l), _layer_spec(gpf, l), _layer_spec(gqf, l),
            _layer_spec(wo, l), _layer_spec(wg, l), _layer_spec(wu, l), _layer_spec(wd, l),
        ],
        out_specs=tile(d),
        compiler_params=pltpu.CompilerParams(vmem_limit_bytes=VMEM_LIMIT),
        name="outffn",
    )(yg, yf, yr, x, mod, gpm, gpf, gqf, wo, wg, wu, wd)


def _prep_w_in(w_in):
    off_dec = 2 * GLA_QK + GLA_V
    off_og = off_dec + 2 * GLA_RANK
    pad = jnp.zeros(w_in.shape[:2] + (DEC_PAD - 2 * GLA_RANK,), w_in.dtype)
    return jnp.concatenate([w_in[..., 0:off_dec], w_in[..., off_og:], w_in[..., off_dec:off_og], pad], axis=-1).astype(BF16)


def _prep_dec(w_dec, b_dec):
    depth = w_dec.shape[0]
    hp = 2 * GLA_DK
    wr = w_dec.reshape(depth, 2, GLA_RANK, 2, hp)
    wbd = jnp.einsum("ldrpj,de->lpdrej", wr, jnp.eye(2, dtype=w_dec.dtype))
    wbd = wbd.reshape(depth, 2, 2 * GLA_RANK, 2 * hp)
    wbd = jnp.pad(wbd, ((0, 0), (0, 0), (0, DEC_PAD - 2 * GLA_RANK), (0, 0)))
    bd = b_dec.reshape(depth, 2, 2, hp).transpose(0, 2, 1, 3).reshape(depth, 2, 1, 2 * hp)
    return wbd.astype(BF16), bd


def _block_diag(w):
    h, hd = w.shape[-3], w.shape[-2]
    out = jnp.einsum("...hij,hg->...higj", w, jnp.eye(h, dtype=w.dtype))
    return out.reshape(w.shape[:-3] + (h * hd, h * hd))


def _prep_rg(w_a, b_a, w_x, b_x):
    bda, bdx = _block_diag(w_a), _block_diag(w_x)
    wg = jnp.concatenate([bda[:, 0], bdx[:, 0], bda[:, 1], bdx[:, 1]], axis=-1)
    bg = jnp.concatenate([b_a[:, 0], b_x[:, 0], b_a[:, 1], b_x[:, 1]], axis=-1)[:, None, :]
    return (0.5 * wg).astype(BF16), 0.5 * bg


def _mixer(h, mod, mod_row, l, lw, consts, s0, h0, line, tm, with_output):
    t = h.shape[1]
    f_pitch = FFT_PITCH if "m2" in consts else GRID_W
    zq, zk, zv, zog, zf, zrx, zrg, zdec = _inproj(h, mod, mod_row, l, lw["g_pre_mix"], lw["w_in"], tm, f_pitch)
    yg, s_fin = _gla(zq, zk, zv, zog, zdec, l, lw["wd"], lw["bd"], lw["g_gla"], s0, with_output)
    yr, h_fin = _rg(zrx, zrg, l, lw["w_conv"], lw["b_conv"], lw["w_gate"], lw["b_gate"], lw["lam"], h0, line, with_output)
    yf = _fourier(zf, consts, t) if with_output else None
    return (yg, yf, yr), s_fin, h_fin


def kernel(x, c, ctx, c_ctx, w_ada, b_ada, g_pre_mix, g_post_mix, g_pre_ffn, g_post_ffn, w_in, w_dec, b_dec, g_gla,
           w_conv, b_conv, w_rg_a, b_rg_a, w_rg_x, b_rg_x, rg_lam, w_out, w_ffn_gate, w_ffn_up, w_ffn_down):
    bsz, t, d = x.shape
    t_ctx = ctx.shape[1]
    depth = w_ada.shape[0]

    c_all = jnp.concatenate([c, c_ctx[None, :], jnp.zeros((MOD_ROWS - bsz - 1, d), F32)], axis=0)
    mod = _ada(c_all, w_ada, b_ada).reshape(depth, MOD_ROWS, 6, d)

    consts_lat = _fourier_consts(t)
    consts_ctx = _fourier_consts(t_ctx)
    s_zero = jnp.zeros((bsz, 2, GLA_HEADS, GLA_DK, GLA_DV), F32)
    h_zero = jnp.zeros((bsz, SUBLANES, RG_W), F32)
    rows = lambda v: v[:, None, :]

    wd_p, bd_p = _prep_dec(w_dec, b_dec)
    w_gate, b_gate = _prep_rg(w_rg_a, b_rg_a, w_rg_x, b_rg_x)
    lw = dict(g_pre_mix=rows(g_pre_mix), w_in=_prep_w_in(w_in), wd=wd_p, bd=bd_p, g_gla=rows(g_gla),
              w_conv=w_conv, b_conv=rows(b_conv), w_gate=w_gate, b_gate=b_gate, lam=rg_lam)
    ffn = (rows(g_post_mix), rows(g_pre_ffn), rows(g_post_ffn), w_out.astype(BF16),
           w_ffn_gate.astype(BF16), w_ffn_up.astype(BF16), w_ffn_down.astype(BF16))

    tm = min(t, ROW_TILE)
    assert t % tm == 0 and tm % GRID_W == 0 and t_ctx % GRID_W == 0
    h_ctx = ctx
    for l in range(depth):
        last = l == depth - 1
        ys_c, s_ctx, hs_ctx = _mixer(h_ctx, mod, bsz, l, lw, consts_ctx, s_zero, h_zero, t_ctx, t_ctx, not last)
        ys_l, _, _ = _mixer(x, mod, None, l, lw, consts_lat, s_ctx, hs_ctx, GRID_W, tm, True)
        x = _outffn(*ys_l, x, mod, None, l, *ffn, tm)
        if not last:
            h_ctx = _outffn(*ys_c, h_ctx, mod, bsz, l, *ffn, t_ctx)
    return x
```
